```python
import math, functools
import jax, jax.numpy as jnp
from jax import lax
import numpy as np

D_MODEL = 2048
BATCH = 32
SEQ = 256
DEPTH = 4
DEC_BATCH = 4
DEC_SEQ = 2048
PAST_LEN = 512

F32 = jnp.float32
EPS = 1e-6
NEG_INF = -1e30
ROPE_THETA = 10000.0
GRID_W = 64
BLOCK = 128
N_BRANCH = 4
BRANCH_W = D_MODEL // 4
MLA_HEADS = 4
MLA_NOPE = 128
MLA_ROPE = 64
MLA_QK = MLA_NOPE + MLA_ROPE
MLA_V = 128
MLA_Q_RANK = D_MODEL // 4
MLA_KV_RANK = D_MODEL // 4
SSM_W = BRANCH_W
SSM_CH = 16
SSM_GROUPS = SSM_W // SSM_CH
SSM_N = 64
DT_MIN = 1e-3
DT_MAX = 1e-1
SWA_HEADS = 8
SWA_KV_HEADS = 2
SWA_REP = SWA_HEADS // SWA_KV_HEADS
SWA_HD = 64
WINDOW = 128
DIFF_HEADS = 4
DIFF_HD = 64
DIFF_VD = 2 * DIFF_HD
D_FF = 5632
N_EXPERTS = 8
TOP_K = 2
D_FF_EXPERT = 7168
N_DENSE = (DEPTH + 1) // 2
N_MOE = DEPTH // 2
COL_MLA_Q = MLA_Q_RANK
COL_MLA_KV = MLA_KV_RANK + MLA_ROPE
COL_SSM = SSM_W
COL_SWA = (SWA_HEADS + 2 * SWA_KV_HEADS) * SWA_HD
COL_DIFF = 2 * DIFF_HEADS * 2 * DIFF_HD + DIFF_HEADS * DIFF_VD
COL_GATE = N_BRANCH * D_MODEL
SPLIT_SIZES = (COL_MLA_Q, COL_MLA_KV, COL_SSM, COL_SWA, COL_DIFF, COL_GATE)
IN_COLS = COL_MLA_Q + COL_MLA_KV + COL_SSM + COL_SWA + COL_DIFF + COL_GATE

kernel_name = 'hybrid_diffusion_prefix_trunk_step'


def rmsnorm(x, g):
    xf = x.astype(F32)
    y = xf * lax.rsqrt(jnp.mean(xf * xf, axis=-1, keepdims=True) + EPS)
    return (y * g.astype(F32)).astype(x.dtype)


def split_cols(p):
    idx = np.cumsum(SPLIT_SIZES)[:-1].tolist()
    return jnp.split(p, idx, axis=-1)


def axial_rope(x):
    t, d = x.shape[1], x.shape[-1]
    rows = t // GRID_W
    r, col = jnp.meshgrid(jnp.arange(rows, dtype=F32), jnp.arange(GRID_W, dtype=F32), indexing='ij')
    n_freq = d // 4
    inv = ROPE_THETA ** (-jnp.arange(n_freq, dtype=F32) / n_freq)
    ang = jnp.concatenate([r.reshape(-1, 1) * inv, col.reshape(-1, 1) * inv], axis=-1)
    bshape = (1, t) + (1,) * (x.ndim - 3) + (d // 2,)
    cos, sin = jnp.cos(ang).reshape(bshape), jnp.sin(ang).reshape(bshape)
    xf = x.astype(F32)
    x1, x2 = xf[..., : d // 2], xf[..., d // 2:]
    return jnp.concatenate([x1 * cos - x2 * sin, x2 * cos + x1 * sin], axis=-1).astype(x.dtype)


def sweep_query_blocks(fn, *qs):
    b, t = qs[0].shape[:2]
    nb = t // BLOCK
    blocks = tuple(jnp.moveaxis(q.reshape((b, nb, BLOCK) + q.shape[2:]), 1, 0) for q in qs)
    out = lax.map(lambda xs: fn(*xs), blocks)
    out = jnp.moveaxis(out, 0, 1)
    return out.reshape((b, t) + out.shape[3:])


def softmax_attend(q, k, v, scale, sink=None):
    s = jnp.einsum('bqkgd,bskd->bkgqs', q.astype(F32), k.astype(F32)) * scale
    if sink is None:
        p = jax.nn.softmax(s, axis=-1)
    else:
        sk = jnp.broadcast_to(sink.astype(F32)[None, :, :, None, None], s.shape[:-1] + (1,))
        p = jax.nn.softmax(jnp.concatenate([sk, s], axis=-1), axis=-1)[..., 1:]
    return jnp.einsum('bkgqs,bskd->bqkgd', p, v.astype(F32))


def mla_queries(p_q, g_qlat, w_uq, g_q):
    b, t = p_q.shape[:2]
    q = (rmsnorm(p_q, g_qlat) @ w_uq).reshape(b, t, MLA_HEADS, MLA_QK)
    return rmsnorm(q, g_q)


def mla_latents(p_kv, g_kvlat):
    return rmsnorm(p_kv[..., :MLA_KV_RANK], g_kvlat), p_kv[..., MLA_KV_RANK:]


def mla_keys_values(c_kv, k_rope, w_ukv, g_k):
    b, s = c_kv.shape[:2]
    kv = (c_kv @ w_ukv).reshape(b, s, MLA_HEADS, MLA_NOPE + MLA_V)
    k_nope, v = kv[..., :MLA_NOPE], kv[..., MLA_NOPE:]
    k_r = jnp.broadcast_to(k_rope[:, :, None, :], (b, s, MLA_HEADS, MLA_ROPE)).astype(k_nope.dtype)
    k = rmsnorm(jnp.concatenate([k_nope, k_r], axis=-1), g_k)
    return k, v


def rope_tail(x):
    return jnp.concatenate([x[..., :MLA_NOPE], axial_rope(x[..., MLA_NOPE:])], axis=-1)


def mla_attend(q, k, v):
    def blk(qb):
        o = softmax_attend(qb[:, :, :, None, :], k, v, MLA_QK ** -0.5)
        return o[:, :, :, 0, :].astype(qb.dtype)
    o = sweep_query_blocks(blk, q)
    return o.reshape(o.shape[:2] + (MLA_HEADS * MLA_V,))


def s5_scan(u, a_re, a_im, log_dt, b_re, b_im, c_re, c_im, h0, reverse):
    lam = lax.complex(a_re.astype(F32), a_im.astype(F32))
    lam_bar = jnp.exp(lam * jnp.exp(log_dt.astype(F32))[:, None])
    b_bar = ((lam_bar - 1.0) / lam)[..., None] * lax.complex(b_re.astype(F32), b_im.astype(F32))
    bu = jnp.einsum('btgi,gni->btgn', u.astype(jnp.complex64), b_bar)
    a = jnp.broadcast_to(lam_bar, bu.shape)

    def combine(e1, e2):
        a1, b1 = e1
        a2, b2 = e2
        return a1 * a2, a2 * b1 + b2

    a_cum, h = lax.associative_scan(combine, (a, bu), axis=1, reverse=reverse)
    h = h + a_cum * h0[:, None]
    y = jnp.einsum('btgn,gin->btgi', h, lax.complex(c_re.astype(F32), c_im.astype(F32))).real
    h_last = h[:, 0] if reverse else h[:, -1]
    return y, h_last


def s5_mixer(p_ssm, P, h0_f, h0_b):
    b, t = p_ssm.shape[:2]
    u = p_ssm.astype(F32)
    ug = u.reshape(b, t, SSM_GROUPS, SSM_CH)
    y_f, h_f = s5_scan(ug, P['ssm_a_re'][0], P['ssm_a_im'][0], P['ssm_log_dt'][0], P['ssm_b_re'][0],
                       P['ssm_b_im'][0], P['ssm_c_re'][0], P['ssm_c_im'][0], h0_f, False)
    y_b, h_b = s5_scan(ug, P['ssm_a_re'][1], P['ssm_a_im'][1], P['ssm_log_dt'][1], P['ssm_b_re'][1],
                       P['ssm_b_im'][1], P['ssm_c_re'][1], P['ssm_c_im'][1], h0_b, True)
    y = (y_f + y_b).reshape(b, t, SSM_W) + P['ssm_d'].astype(F32) * u
    g = jax.nn.gelu(y).astype(p_ssm.dtype)
    out = g * jax.nn.sigmoid(g @ P['w_ssm_glu'])
    state = jnp.stack([jnp.stack([h_f.real, h_f.imag], axis=1),
                       jnp.stack([h_b.real, h_b.imag], axis=1)], axis=1)
    return out, state


def state_to_complex(st):
    h_f = lax.complex(st[:, 0, 0].astype(F32), st[:, 0, 1].astype(F32))
    h_b = lax.complex(st[:, 1, 0].astype(F32), st[:, 1, 1].astype(F32))
    return h_f, h_b


def swa_project(p_swa, g_q, g_k):
    b, t = p_swa.shape[:2]
    q, k, v = jnp.split(p_swa, [SWA_HEADS * SWA_HD, (SWA_HEADS + SWA_KV_HEADS) * SWA_HD], axis=-1)
    q = rmsnorm(q.reshape(b, t, SWA_HEADS, SWA_HD), g_q)
    k = rmsnorm(k.reshape(b, t, SWA_KV_HEADS, SWA_HD), g_k)
    return q, k, v.reshape(b, t, SWA_KV_HEADS, SWA_HD)


def swa_context_attend(q, k, v, sink):
    sink_kg = sink.reshape(SWA_KV_HEADS, SWA_REP)

    def blk(qb):
        b, n = qb.shape[:2]
        o = softmax_attend(qb.reshape(b, n, SWA_KV_HEADS, SWA_REP, SWA_HD), k, v, SWA_HD ** -0.5, sink_kg)
        return o.reshape(b, n, SWA_HEADS * SWA_HD).astype(qb.dtype)
    return sweep_query_blocks(blk, q)


def swa_latent_attend(q, k, v, k_ctx, v_ctx, sink):
    b, t = q.shape[:2]
    nb = t // BLOCK
    c = k_ctx.shape[1]
    scale = SWA_HD ** -0.5
    qb = q.reshape(b, nb, BLOCK, SWA_KV_HEADS, SWA_REP, SWA_HD).astype(F32)
    pad = ((0, 0), (BLOCK, BLOCK), (0, 0), (0, 0))
    kb = jnp.pad(k, pad).reshape(b, nb + 2, BLOCK, SWA_KV_HEADS, SWA_HD)
    vb = jnp.pad(v, pad).reshape(b, nb + 2, BLOCK, SWA_KV_HEADS, SWA_HD)
    band = lambda z: jnp.concatenate([z[:, :-2], z[:, 1:-1], z[:, 2:]], axis=2)
    kw, vw = band(kb).astype(F32), band(vb).astype(F32)
    s_loc = jnp.einsum('bnqkgd,bnskd->bnkgqs', qb, kw) * scale
    qpos = jnp.arange(nb)[:, None] * BLOCK + jnp.arange(BLOCK)[None, :]
    kpos = jnp.arange(nb)[:, None] * BLOCK - BLOCK + jnp.arange(3 * BLOCK)[None, :]
    mask = ((jnp.abs(qpos[:, :, None] - kpos[:, None, :]) <= WINDOW)
            & (kpos >= 0)[:, None, :] & (kpos < t)[:, None, :])
    s_loc = jnp.where(mask[None, :, None, None], s_loc, NEG_INF)
    s_ctx = jnp.einsum('bnqkgd,bckd->bnkgqc', qb, k_ctx.astype(F32)) * scale
    sk = jnp.broadcast_to(sink.reshape(SWA_KV_HEADS, SWA_REP).astype(F32)[None, None, :, :, None, None],
                          s_ctx.shape[:-1] + (1,))
    p = jax.nn.softmax(jnp.concatenate([sk, s_ctx, s_loc], axis=-1), axis=-1)
    p_ctx, p_loc = p[..., 1:1 + c], p[..., 1 + c:]
    o = (jnp.einsum('bnkgqc,bckd->bnqkgd', p_ctx, v_ctx.astype(F32))
         + jnp.einsum('bnkgqs,bnskd->bnqkgd', p_loc, vw))
    return o.reshape(b, t, SWA_HEADS * SWA_HD).astype(q.dtype)


def diff_project(p_diff, g_q, g_k):
    b, t = p_diff.shape[:2]
    nq = DIFF_HEADS * 2 * DIFF_HD
    q, k, v = jnp.split(p_diff, [nq, 2 * nq], axis=-1)
    q = rmsnorm(q.reshape(b, t, DIFF_HEADS, 2, DIFF_HD), g_q)
    k = rmsnorm(k.reshape(b, t, DIFF_HEADS, 2, DIFF_HD), g_k)
    return q, k, v.reshape(b, t, DIFF_HEADS, DIFF_VD)


def diff_attend(q, k, v, lam_params, g_out, lambda_init):
    lp = lam_params.astype(F32)
    lam = jnp.exp(jnp.sum(lp[0] * lp[1], -1)) - jnp.exp(jnp.sum(lp[2] * lp[3], -1)) + lambda_init
    kf, vf = k.astype(F32), v.astype(F32)

    def blk(qb):
        s = jnp.einsum('bqhmd,bshmd->bhmqs', qb.astype(F32), kf) * (DIFF_HD ** -0.5)
        p = jax.nn.softmax(s, axis=-1)
        a = p[:, :, 0] - lam[None, :, None, None] * p[:, :, 1]
        return jnp.einsum('bhqs,bshd->bqhd', a, vf)
    o = sweep_query_blocks(blk, q)
    o = rmsnorm(o, g_out) * (1.0 - lambda_init)
    return o.reshape(o.shape[:2] + (DIFF_HEADS * DIFF_VD,)).astype(q.dtype)


def merge_branches(outs, p_gate, w_branch, w_out):
    stacked = jnp.stack(outs, axis=2)
    up = jnp.einsum('btnw,nwd->btnd', stacked, w_branch)
    gates = jax.nn.sigmoid(p_gate.reshape(p_gate.shape[:2] + (N_BRANCH, D_MODEL)))
    return jnp.sum(gates * up, axis=2) @ w_out


def swiglu(h, wg, wu, wd):
    return (jax.nn.silu(h @ wg) * (h @ wu)) @ wd


def moe_swiglu(h, w_router, wg, wu, wd):
    probs = jax.nn.softmax((h @ w_router).astype(F32), axis=-1)
    top_p, top_i = lax.top_k(probs, TOP_K)
    top_p = top_p / jnp.sum(top_p, axis=-1, keepdims=True)
    combine = jnp.sum(jax.nn.one_hot(top_i, N_EXPERTS, dtype=F32) * top_p[..., None], axis=-2)
    out = jnp.zeros(h.shape, F32)
    for e in range(N_EXPERTS):
        out = out + combine[..., e:e + 1] * swiglu(h, wg[e], wu[e], wd[e]).astype(F32)
    return out.astype(h.dtype)


def ada_mod(cvec, w_ada, b_ada):
    m = jax.nn.silu(cvec) @ w_ada + b_ada
    return jnp.split(m[:, None, :], 6, axis=-1)


def run_block(x, mods, g1, g2, mixer_fn, ffn_fn):
    sh1, sc1, gt1, sh2, sc2, gt2 = mods
    mix, aux = mixer_fn(rmsnorm(x, g1) * (1 + sc1) + sh1)
    x = x + gt1 * mix
    x = x + gt2 * ffn_fn(rmsnorm(x, g2) * (1 + sc2) + sh2)
    return x, aux


def mixers_context(h, P, lambda_init):
    b = h.shape[0]
    p_q, p_kv, p_ssm, p_swa, p_diff, p_gate = split_cols(h @ P['w_in'])
    c_kv, k_rope = mla_latents(p_kv, P['g_mla_kvlat'])
    k_m, v_m = mla_keys_values(c_kv, k_rope, P['w_mla_ukv'], P['g_mla_k'])
    q_m = mla_queries(p_q, P['g_mla_qlat'], P['w_mla_uq'], P['g_mla_q'])
    o_mla = mla_attend(q_m, k_m, v_m)
    zero = jnp.zeros((b, SSM_GROUPS, SSM_N), jnp.complex64)
    o_ssm, ssm_state = s5_mixer(p_ssm, P, zero, zero)
    q_s, k_s, v_s = swa_project(p_swa, P['g_swa_q'], P['g_swa_k'])
    o_swa = swa_context_attend(q_s, k_s, v_s, P['swa_sink'])
    q_d, k_d, v_d = diff_project(p_diff, P['g_diff_q'], P['g_diff_k'])
    o_diff = diff_attend(q_d, k_d, v_d, P['diff_lambda'], P['g_diff_out'], lambda_init)
    out = merge_branches((o_mla, o_ssm, o_swa, o_diff), p_gate, P['w_branch'], P['w_out'])
    return out, (c_kv, k_rope, ssm_state, k_s, v_s, k_d, v_d)


def mixers_latent(h, P, lambda_init, ctx):
    ckv_c, krope_c, st_c, swak_c, swav_c, diffk_c, diffv_c = ctx
    p_q, p_kv, p_ssm, p_swa, p_diff, p_gate = split_cols(h @ P['w_in'])
    q_m = rope_tail(mla_queries(p_q, P['g_mla_qlat'], P['w_mla_uq'], P['g_mla_q']))
    ckv_l, krope_l = mla_latents(p_kv, P['g_mla_kvlat'])
    k_l, v_l = mla_keys_values(ckv_l, krope_l, P['w_mla_ukv'], P['g_mla_k'])
    k_c, v_c = mla_keys_values(ckv_c, krope_c, P['w_mla_ukv'], P['g_mla_k'])
    o_mla = mla_attend(q_m, jnp.concatenate([k_c, rope_tail(k_l).astype(k_c.dtype)], axis=1),
                       jnp.concatenate([v_c, v_l.astype(v_c.dtype)], axis=1))
    h0_f, h0_b = state_to_complex(st_c)
    o_ssm, _ = s5_mixer(p_ssm, P, h0_f, h0_b)
    q_s, k_s, v_s = swa_project(p_swa, P['g_swa_q'], P['g_swa_k'])
    o_swa = swa_latent_attend(axial_rope(q_s), axial_rope(k_s), v_s, swak_c, swav_c, P['swa_sink'])
    q_d, k_d, v_d = diff_project(p_diff, P['g_diff_q'], P['g_diff_k'])
    o_diff = diff_attend(axial_rope(q_d),
                         jnp.concatenate([diffk_c, axial_rope(k_d).astype(diffk_c.dtype)], axis=1),
                         jnp.concatenate([diffv_c, v_d.astype(diffv_c.dtype)], axis=1),
                         P['diff_lambda'], P['g_diff_out'], lambda_init)
    out = merge_branches((o_mla, o_ssm, o_swa, o_diff), p_gate, P['w_branch'], P['w_out'])
    return out, None


def setup_inputs(seed: int = 0) -> dict:
    key = jax.random.key(seed)
    ks = iter(jax.random.split(key, 64))
    L = DEPTH
    G, N, CH = SSM_GROUPS, SSM_N, SSM_CH

    def nrm(shape, scale=1.0):
        return jax.random.normal(next(ks), shape, F32) * scale

    def gain(shape):
        return 1.0 + 0.02 * jax.random.normal(next(ks), shape, F32)

    a_im_base = jnp.pi * jnp.arange(N, dtype=F32)
    return {
        'x_prompt': nrm((BATCH, SEQ, D_MODEL)),
        'x_sample': nrm((DEC_BATCH, DEC_SEQ, D_MODEL)),
        'cache_mla_ckv': nrm((DEC_BATCH, L, PAST_LEN, MLA_KV_RANK)),
        'cache_mla_krope': nrm((DEC_BATCH, L, PAST_LEN, MLA_ROPE)),
        'state_ssm': nrm((DEC_BATCH, L, 2, 2, G, N), 0.3),
        'cache_swa_k': nrm((DEC_BATCH, L, PAST_LEN, SWA_KV_HEADS, SWA_HD)),
        'cache_swa_v': nrm((DEC_BATCH, L, PAST_LEN, SWA_KV_HEADS, SWA_HD)),
        'cache_diff_k': nrm((DEC_BATCH, L, PAST_LEN, DIFF_HEADS, 2, DIFF_HD)),
        'cache_diff_v': nrm((DEC_BATCH, L, PAST_LEN, DIFF_HEADS, DIFF_VD)),
        'c': nrm((DEC_BATCH, D_MODEL)),
        'c_ctx': nrm((D_MODEL,)),
        'w_ada': nrm((L, D_MODEL, 6 * D_MODEL), 0.5 * D_MODEL ** -0.5),
        'b_ada': nrm((L, 6 * D_MODEL), 0.01),
        'g_norm1': gain((L, D_MODEL)),
        'g_norm2': gain((L, D_MODEL)),
        'w_in': nrm((L, D_MODEL, IN_COLS), D_MODEL ** -0.5),
        'g_mla_qlat': gain((L, MLA_Q_RANK)),
        'g_mla_kvlat': gain((L, MLA_KV_RANK)),
        'w_mla_uq': nrm((L, MLA_Q_RANK, MLA_HEADS * MLA_QK), MLA_Q_RANK ** -0.5),
        'w_mla_ukv': nrm((L, MLA_KV_RANK, MLA_HEADS * (MLA_NOPE + MLA_V)), MLA_KV_RANK ** -0.5),
        'g_mla_q': gain((L, MLA_QK)),
        'g_mla_k': gain((L, MLA_QK)),
        'ssm_a_re': -0.5 + nrm((L, 2, G, N), 0.01),
        'ssm_a_im': a_im_base + nrm((L, 2, G, N), 0.01),
        'ssm_log_dt': jax.random.uniform(next(ks), (L, 2, G), F32, math.log(DT_MIN), math.log(DT_MAX)),
        'ssm_b_re': nrm((L, 2, G, N, CH), (2 * CH) ** -0.5),
        'ssm_b_im': nrm((L, 2, G, N, CH), (2 * CH) ** -0.5),
        'ssm_c_re': nrm((L, 2, G, CH, N), 2.0 * N ** -0.5),
        'ssm_c_im': nrm((L, 2, G, CH, N), 2.0 * N ** -0.5),
        'ssm_d': nrm((L, SSM_W)),
        'w_ssm_glu': nrm((L, SSM_W, SSM_W), SSM_W ** -0.5),
        'g_swa_q': gain((L, SWA_HD)),
        'g_swa_k': gain((L, SWA_HD)),
        'swa_sink': nrm((L, SWA_HEADS)),
        'g_diff_q': gain((L, 2, DIFF_HD)),
        'g_diff_k': gain((L, 2, DIFF_HD)),
        'diff_lambda': nrm((L, 4, DIFF_HEADS, DIFF_HD), 0.1),
        'g_diff_out': gain((L, DIFF_VD)),
        'w_branch': nrm((L, N_BRANCH, BRANCH_W, D_MODEL), BRANCH_W ** -0.5),
        'w_out': nrm((L, D_MODEL, D_MODEL), D_MODEL ** -0.5),
        'w_ffn_gate': nrm((N_DENSE, D_MODEL, D_FF), D_MODEL ** -0.5),
        'w_ffn_up': nrm((N_DENSE, D_MODEL, D_FF), D_MODEL ** -0.5),
        'w_ffn_down': nrm((N_DENSE, D_FF, D_MODEL), D_FF ** -0.5),
        'w_router': nrm((N_MOE, D_MODEL, N_EXPERTS), D_MODEL ** -0.5),
        'w_moe_gate': nrm((N_MOE, N_EXPERTS, D_MODEL, D_FF_EXPERT), D_MODEL ** -0.5),
        'w_moe_up': nrm((N_MOE, N_EXPERTS, D_MODEL, D_FF_EXPERT), D_MODEL ** -0.5),
        'w_moe_down': nrm((N_MOE, N_EXPERTS, D_FF_EXPERT, D_MODEL), D_FF_EXPERT ** -0.5),
    }


def reference(x_prompt, x_sample, cache_mla_ckv, cache_mla_krope, state_ssm, cache_swa_k, cache_swa_v,
              cache_diff_k, cache_diff_v, c, c_ctx, w_ada, b_ada, g_norm1, g_norm2, w_in,
              g_mla_qlat, g_mla_kvlat, w_mla_uq, w_mla_ukv, g_mla_q, g_mla_k,
              ssm_a_re, ssm_a_im, ssm_log_dt, ssm_b_re, ssm_b_im, ssm_c_re, ssm_c_im, ssm_d, w_ssm_glu,
              g_swa_q, g_swa_k, swa_sink, g_diff_q, g_diff_k, diff_lambda, g_diff_out,
              w_branch, w_out, w_ffn_gate, w_ffn_up, w_ffn_down,
              w_router, w_moe_gate, w_moe_up, w_moe_down):
    y_p = x_prompt
    y_s = x_sample
    ctx_lists = [[] for _ in range(7)]
    for l in range(DEPTH):
        P = {
            'w_in': w_in[l], 'g_mla_qlat': g_mla_qlat[l], 'g_mla_kvlat': g_mla_kvlat[l],
            'w_mla_uq': w_mla_uq[l], 'w_mla_ukv': w_mla_ukv[l], 'g_mla_q': g_mla_q[l], 'g_mla_k': g_mla_k[l],
            'ssm_a_re': ssm_a_re[l], 'ssm_a_im': ssm_a_im[l], 'ssm_log_dt': ssm_log_dt[l],
            'ssm_b_re': ssm_b_re[l], 'ssm_b_im': ssm_b_im[l], 'ssm_c_re': ssm_c_re[l], 'ssm_c_im': ssm_c_im[l],
            'ssm_d': ssm_d[l], 'w_ssm_glu': w_ssm_glu[l],
            'g_swa_q': g_swa_q[l], 'g_swa_k': g_swa_k[l], 'swa_sink': swa_sink[l],
            'g_diff_q': g_diff_q[l], 'g_diff_k': g_diff_k[l], 'diff_lambda': diff_lambda[l],
            'g_diff_out': g_diff_out[l], 'w_branch': w_branch[l], 'w_out': w_out[l],
        }
        lambda_init = 0.8 - 0.6 * math.exp(-0.3 * l)
        if l % 2 == 0:
            j = l // 2
            ffn_fn = functools.partial(swiglu, wg=w_ffn_gate[j], wu=w_ffn_up[j], wd=w_ffn_down[j])
        else:
            j = l // 2
            ffn_fn = functools.partial(moe_swiglu, w_router=w_router[j], wg=w_moe_gate[j],
                                       wu=w_moe_up[j], wd=w_moe_down[j])
        mods_ctx = ada_mod(c_ctx[None, :], w_ada[l], b_ada[l])
        y_p, ctx_l = run_block(y_p, mods_ctx, g_norm1[l], g_norm2[l],
                               functools.partial(mixers_context, P=P, lambda_init=lambda_init), ffn_fn)
        for lst, arr in zip(ctx_lists, ctx_l):
            lst.append(arr)
        cache_l = (cache_mla_ckv[:, l], cache_mla_krope[:, l], state_ssm[:, l], cache_swa_k[:, l],
                   cache_swa_v[:, l], cache_diff_k[:, l], cache_diff_v[:, l])
        mods_lat = ada_mod(c, w_ada[l], b_ada[l])
        y_s, _ = run_block(y_s, mods_lat, g_norm1[l], g_norm2[l],
                           functools.partial(mixers_latent, P=P, lambda_init=lambda_init, ctx=cache_l), ffn_fn)
    new_mla_ckv = jnp.stack(ctx_lists[0], axis=1)
    new_mla_krope = jnp.stack(ctx_lists[1], axis=1)
    new_state_ssm = jnp.stack(ctx_lists[2], axis=1)
    new_swa_k = jnp.stack(ctx_lists[3], axis=1)
    new_swa_v = jnp.stack(ctx_lists[4], axis=1)
    new_diff_k = jnp.stack(ctx_lists[5], axis=1)
    new_diff_v = jnp.stack(ctx_lists[6], axis=1)
    return (y_p, y_s, new_mla_ckv, new_mla_krope, new_state_ssm, new_swa_k, new_swa_v, new_diff_k, new_diff_v)
```

```python
import functools
import math

import numpy as np
import jax
import jax.numpy as jnp
from jax import lax
from jax.experimental import pallas as pl
from jax.experimental.pallas import tpu as pltpu

F32 = jnp.float32
BF16 = jnp.bfloat16
EPS = 1e-6
NEG_INF = -1e30
ROPE_THETA = 10000.0
GRID_W = 64
BLOCK = 128
N_BRANCH = 4
MLA_HEADS = 4
MLA_NOPE = 128
MLA_ROPE = 64
MLA_QK = MLA_NOPE + MLA_ROPE
MLA_V = 128
SSM_CH = 16
SSM_N = 64
SWA_HEADS = 8
SWA_KV_HEADS = 2
SWA_REP = SWA_HEADS // SWA_KV_HEADS
SWA_HD = 64
WINDOW = 128
DIFF_HEADS = 4
DIFF_HD = 64
DIFF_VD = 2 * DIFF_HD
N_EXPERTS = 8
TOP_K = 2

V7X_VMEM_REQUEST_CAP = 56 * 1024 * 1024


def _params(semantics, vmem_bytes):
    limit = int(min(max(vmem_bytes * 5 // 4 + (2 << 20), 16 << 20), V7X_VMEM_REQUEST_CAP))
    return pltpu.CompilerParams(dimension_semantics=semantics, vmem_limit_bytes=limit)


def _tile(n, want):
    t = min(n, want)
    while n % t:
        t -= 1
    return t


def _nbytes(shape, dtype):
    return int(np.prod(shape)) * jnp.dtype(dtype).itemsize


def _mm_kernel(x_ref, w_ref, o_ref, *scratch):
    if scratch:
        wbf_ref, = scratch

        @pl.when(pl.program_id(1) == 0)
        def _():
            wbf_ref[...] = w_ref[...].astype(BF16)
        w = wbf_ref[...]
    else:
        w = w_ref[...]
    o_ref[...] = jnp.dot(x_ref[...].astype(BF16), w, preferred_element_type=F32).astype(o_ref.dtype)


def matmul(x, w, lead=(), *, bm=1024, bn=512, out_dtype=F32):
    m, k = x.shape
    n = w.shape[-1]
    assert w.shape[-2] == k
    bm, bn = _tile(m, bm), _tile(n, bn)
    assert m % bm == 0 and n % bn == 0, (m, n, bm, bn)
    nl = len(lead)
    cast = w.dtype != BF16
    scratch = [pltpu.VMEM((k, bn), BF16)] if cast else []
    vmem = (2 * _nbytes((bm, k), x.dtype) + 2 * _nbytes((k, bn), w.dtype)
            + 2 * _nbytes((bm, bn), out_dtype) + (_nbytes((k, bn), BF16) if cast else 0))
    return pl.pallas_call(
        _mm_kernel,
        grid=(n // bn, m // bm),
        in_specs=[pl.BlockSpec((bm, k), lambda j, i: (i, 0)),
                  pl.BlockSpec((None,) * nl + (k, bn), lambda j, i: lead + (0, j))],
        out_specs=pl.BlockSpec((bm, bn), lambda j, i: (i, j)),
        out_shape=jax.ShapeDtypeStruct((m, n), out_dtype),
        scratch_shapes=scratch,
        compiler_params=_params(("arbitrary", "arbitrary"), vmem),
    )(x, w)


def _glu_kernel(x_ref, wg_ref, wu_ref, o_ref, wg_bf, wu_bf):
    @pl.when(pl.program_id(1) == 0)
    def _():
        wg_bf[...] = wg_ref[...].astype(BF16)
        wu_bf[...] = wu_ref[...].astype(BF16)
    x = x_ref[...]
    a = jnp.dot(x, wg_bf[...], preferred_element_type=F32)
    b = jnp.dot(x, wu_bf[...], preferred_element_type=F32)
    o_ref[...] = (a * jax.nn.sigmoid(a) * b).astype(o_ref.dtype)


def glu_up(x, wg, wu, lead, *, bm=1024, bn=512):
    m, k = x.shape
    n = wg.shape[-1]
    bm, bn = _tile(m, bm), _tile(n, bn)
    assert m % bm == 0 and n % bn == 0
    nl = len(lead)
    wspec = pl.BlockSpec((None,) * nl + (k, bn), lambda j, i: lead + (0, j))
    vmem = (2 * _nbytes((bm, k), x.dtype) + 4 * _nbytes((k, bn), F32) + 2 * _nbytes((k, bn), BF16)
            + 2 * _nbytes((bm, bn), BF16) + 3 * _nbytes((bm, bn), F32))
    return pl.pallas_call(
        _glu_kernel,
        grid=(n // bn, m // bm),
        in_specs=[pl.BlockSpec((bm, k), lambda j, i: (i, 0)), wspec, wspec],
        out_specs=pl.BlockSpec((bm, bn), lambda j, i: (i, j)),
        out_shape=jax.ShapeDtypeStruct((m, n), BF16),
        scratch_shapes=[pltpu.VMEM((k, bn), BF16), pltpu.VMEM((k, bn), BF16)],
        compiler_params=_params(("arbitrary", "arbitrary"), vmem),
    )(x, wg, wu)


def _is_new_expert(be_ref, i):
    return (i == 0) | (be_ref[i] != be_ref[jnp.maximum(i - 1, 0)])


def _moe_glu_kernel(be_ref, nb_ref, x_ref, wg_ref, wu_ref, o_ref, wg_bf, wu_bf):
    i = pl.program_id(1)

    @pl.when(_is_new_expert(be_ref, i))
    def _():
        wg_bf[...] = wg_ref[...].astype(BF16)
        wu_bf[...] = wu_ref[...].astype(BF16)

    @pl.when(i < nb_ref[0])
    def _():
        x = x_ref[...]
        a = jnp.dot(x, wg_bf[...], preferred_element_type=F32)
        b = jnp.dot(x, wu_bf[...], preferred_element_type=F32)
        o_ref[...] = (a * jax.nn.sigmoid(a) * b).astype(o_ref.dtype)

    @pl.when(i >= nb_ref[0])
    def _():
        o_ref[...] = jnp.zeros_like(o_ref)


def moe_glu_up(x, wg, wu, layer, block_expert, nblocks, *, bm, bn=512):
    r, k = x.shape
    n = wg.shape[-1]
    assert r % bm == 0 and n % bn == 0
    wspec = pl.BlockSpec((None, None, k, bn), lambda j, i, be, nb: (layer, be[i], 0, j))
    vmem = (2 * _nbytes((bm, k), x.dtype) + 4 * _nbytes((k, bn), F32) + 2 * _nbytes((k, bn), BF16)
            + 2 * _nbytes((bm, bn), BF16) + 3 * _nbytes((bm, bn), F32))
    return pl.pallas_call(
        _moe_glu_kernel,
        grid_spec=pltpu.PrefetchScalarGridSpec(
            num_scalar_prefetch=2,
            grid=(n // bn, r // bm),
            in_specs=[pl.BlockSpec((bm, k), lambda j, i, be, nb: (i, 0)), wspec, wspec],
            out_specs=pl.BlockSpec((bm, bn), lambda j, i, be, nb: (i, j)),
            scratch_shapes=[pltpu.VMEM((k, bn), BF16), pltpu.VMEM((k, bn), BF16)]),
        out_shape=jax.ShapeDtypeStruct((r, n), BF16),
        compiler_params=_params(("arbitrary", "arbitrary"), vmem),
    )(block_expert, nblocks, x, wg, wu)


def _moe_down_kernel(be_ref, nb_ref, x_ref, w_ref, o_ref, w_bf):
    i = pl.program_id(1)

    @pl.when(_is_new_expert(be_ref, i))
    def _():
        w_bf[...] = w_ref[...].astype(BF16)

    @pl.when(i < nb_ref[0])
    def _():
        o_ref[...] = jnp.dot(x_ref[...], w_bf[...], preferred_element_type=F32)

    @pl.when(i >= nb_ref[0])
    def _():
        o_ref[...] = jnp.zeros_like(o_ref)


def moe_down(x, w, layer, block_expert, nblocks, *, bm, bn=512):
    r, k = x.shape
    n = w.shape[-1]
    assert r % bm == 0 and n % bn == 0
    vmem = (2 * _nbytes((bm, k), x.dtype) + 2 * _nbytes((k, bn), F32) + _nbytes((k, bn), BF16)
            + 2 * _nbytes((bm, bn), F32))
    return pl.pallas_call(
        _moe_down_kernel,
        grid_spec=pltpu.PrefetchScalarGridSpec(
            num_scalar_prefetch=2,
            grid=(n // bn, r // bm),
            in_specs=[pl.BlockSpec((bm, k), lambda j, i, be, nb: (i, 0)),
                      pl.BlockSpec((None, None, k, bn), lambda j, i, be, nb: (layer, be[i], 0, j))],
            out_specs=pl.BlockSpec((bm, bn), lambda j, i, be, nb: (i, j)),
            scratch_shapes=[pltpu.VMEM((k, bn), BF16)]),
        out_shape=jax.ShapeDtypeStruct((r, n), F32),
        compiler_params=_params(("arbitrary", "arbitrary"), vmem),
    )(block_expert, nblocks, x, w)


def _qk(q, k):
    return lax.dot_general(q, k, (((1,), (1,)), ((), ())), preferred_element_type=F32)


def _attn_kernel(*refs, scale, use_sink):
    if use_sink:
        sink_ref, q_ref, k_ref, v_ref, o_ref = refs
    else:
        q_ref, k_ref, v_ref, o_ref = refs
    s = _qk(q_ref[...], k_ref[...]) * scale
    m = jnp.max(s, axis=-1, keepdims=True)
    if use_sink:
        sink = sink_ref[pl.program_id(1)]
        m = jnp.maximum(m, sink)
    p = jnp.exp(s - m)
    l = jnp.sum(p, axis=-1, keepdims=True)
    if use_sink:
        l = l + jnp.exp(sink - m)
    o = jnp.dot(p.astype(BF16), v_ref[...], preferred_element_type=F32)
    o_ref[...] = (o / l).astype(o_ref.dtype)


def attention(q, k, v, scale, sink=None, *, bq=512):
    b, h, t, d = q.shape
    hk, s, dv = k.shape[1], k.shape[2], v.shape[3]
    rep = h // hk
    bq = min(bq, t)
    assert t % bq == 0
    use_sink = sink is not None
    in_specs = [pl.BlockSpec((None, None, bq, d), lambda bi, hi, qi: (bi, hi, qi, 0)),
                pl.BlockSpec((None, None, s, d), lambda bi, hi, qi: (bi, hi // rep, 0, 0)),
                pl.BlockSpec((None, None, s, dv), lambda bi, hi, qi: (bi, hi // rep, 0, 0))]
    args = [q, k, v]
    if use_sink:
        in_specs = [pl.BlockSpec(memory_space=pltpu.SMEM)] + in_specs
        args = [sink.astype(F32)] + args
    vmem = 4 * _nbytes((bq, s), F32) + 2 * (_nbytes((bq, d), BF16) + _nbytes((s, d + dv), BF16)) + 4 * _nbytes((bq, dv), F32)
    return pl.pallas_call(
        functools.partial(_attn_kernel, scale=scale, use_sink=use_sink),
        grid=(b, h, t // bq),
        in_specs=in_specs,
        out_specs=pl.BlockSpec((None, None, bq, dv), lambda bi, hi, qi: (bi, hi, qi, 0)),
        out_shape=jax.ShapeDtypeStruct((b, h, t, dv), BF16),
        compiler_params=_params(("arbitrary",) * 3, vmem),
    )(*args)


def _diff_attn_kernel(lam_ref, q_ref, k_ref, v_ref, g_ref, o_ref, *, scale, post_scale):
    def probs(i):
        s = _qk(q_ref[i], k_ref[i]) * scale
        p = jnp.exp(s - jnp.max(s, axis=-1, keepdims=True))
        return p / jnp.sum(p, axis=-1, keepdims=True)
    a = probs(0) - lam_ref[pl.program_id(1)] * probs(1)
    o = jnp.dot(a.astype(BF16), v_ref[...], preferred_element_type=F32)
    o = o * lax.rsqrt(jnp.mean(o * o, axis=-1, keepdims=True) + EPS) * g_ref[...]
    o_ref[...] = (o * post_scale).astype(o_ref.dtype)


def diff_attention(q, k, v, lam, g_out, lambda_init, *, bq=512):
    b, h, _, t, d = q.shape
    s, dv = k.shape[3], v.shape[3]
    bq = min(bq, t)
    assert t % bq == 0
    vmem = 6 * _nbytes((bq, s), F32) + 4 * (_nbytes((bq, d), BF16) + _nbytes((s, d), BF16)) + 2 * _nbytes((s, dv), BF16) + 4 * _nbytes((bq, dv), F32)
    return pl.pallas_call(
        functools.partial(_diff_attn_kernel, scale=DIFF_HD ** -0.5, post_scale=1.0 - lambda_init),
        grid=(b, h, t // bq),
        in_specs=[pl.BlockSpec(memory_space=pltpu.SMEM),
                  pl.BlockSpec((None, None, 2, bq, d), lambda bi, hi, qi: (bi, hi, 0, qi, 0)),
                  pl.BlockSpec((None, None, 2, s, d), lambda bi, hi, qi: (bi, hi, 0, 0, 0)),
                  pl.BlockSpec((None, None, s, dv), lambda bi, hi, qi: (bi, hi, 0, 0)),
                  pl.BlockSpec((1, dv), lambda bi, hi, qi: (0, 0))],
        out_specs=pl.BlockSpec((None, None, bq, dv), lambda bi, hi, qi: (bi, hi, qi, 0)),
        out_shape=jax.ShapeDtypeStruct((b, h, t, dv), BF16),
        compiler_params=_params(("arbitrary",) * 3, vmem),
    )(lam.astype(F32), q, k, v, g_out.astype(F32).reshape(1, dv))


def _swa_lat_kernel(sink_ref, q_ref, kc_ref, vc_ref, k0_ref, k1_ref, k2_ref, v0_ref, v1_ref, v2_ref, o_ref,
                    *, scale, nblocks):
    hk, n = pl.program_id(1), pl.program_id(2)
    q = q_ref[...].reshape(SWA_REP * BLOCK, SWA_HD)
    rows = SWA_REP * BLOCK
    s_ctx = _qk(q, kc_ref[...]) * scale
    kw = jnp.concatenate([k0_ref[...], k1_ref[...], k2_ref[...]], axis=0)
    vw = jnp.concatenate([v0_ref[...], v1_ref[...], v2_ref[...]], axis=0)
    s_loc = _qk(q, kw) * scale
    r = lax.broadcasted_iota(jnp.int32, (rows, 3 * BLOCK), 0) & (BLOCK - 1)
    c = lax.broadcasted_iota(jnp.int32, (rows, 3 * BLOCK), 1)
    kpos = (n - 1) * BLOCK + c
    mask = (c - r >= BLOCK - WINDOW) & (c - r <= BLOCK + WINDOW) & (kpos >= 0) & (kpos < nblocks * BLOCK)
    s_loc = jnp.where(mask, s_loc, NEG_INF)
    g = lax.broadcasted_iota(jnp.int32, (rows, 1), 0) >> int(math.log2(BLOCK))
    sink = jnp.zeros((rows, 1), F32)
    for gi in range(SWA_REP):
        sink = jnp.where(g == gi, sink_ref[hk * SWA_REP + gi], sink)
    m = jnp.maximum(jnp.maximum(jnp.max(s_ctx, axis=-1, keepdims=True), jnp.max(s_loc, axis=-1, keepdims=True)), sink)
    p_ctx = jnp.exp(s_ctx - m)
    p_loc = jnp.exp(s_loc - m)
    l = jnp.sum(p_ctx, axis=-1, keepdims=True) + jnp.sum(p_loc, axis=-1, keepdims=True) + jnp.exp(sink - m)
    o = (jnp.dot(p_ctx.astype(BF16), vc_ref[...], preferred_element_type=F32)
         + jnp.dot(p_loc.astype(BF16), vw, preferred_element_type=F32)) / l
    o_ref[...] = o.reshape(SWA_REP, BLOCK, SWA_HD).astype(o_ref.dtype)


def swa_latent_attention(q, k, v, k_ctx, v_ctx, sink):
    b, hk, rep, t, d = q.shape
    c = k_ctx.shape[2]
    nb = t // BLOCK
    pad = ((0, 0), (0, 0), (BLOCK, BLOCK), (0, 0))
    kp, vp = jnp.pad(k, pad), jnp.pad(v, pad)
    band = [pl.BlockSpec((None, None, BLOCK, d), functools.partial(lambda bi, hi, ni, o: (bi, hi, ni + o, 0), o=o))
            for o in range(3)]
    ctx = pl.BlockSpec((None, None, c, d), lambda bi, hi, ni: (bi, hi, 0, 0))
    qspec = pl.BlockSpec((None, None, rep, BLOCK, d), lambda bi, hi, ni: (bi, hi, 0, ni, 0))
    vmem = 8 * _nbytes((rep * BLOCK, c + 3 * BLOCK), F32) + 4 * _nbytes((c, d), BF16) + (4 << 20)
    return pl.pallas_call(
        functools.partial(_swa_lat_kernel, scale=SWA_HD ** -0.5, nblocks=nb),
        grid=(b, hk, nb),
        in_specs=[pl.BlockSpec(memory_space=pltpu.SMEM), qspec, ctx, ctx] + band + band,
        out_specs=qspec,
        out_shape=jax.ShapeDtypeStruct(q.shape, BF16),
        compiler_params=_params(("arbitrary",) * 3, vmem),
    )(sink.astype(F32), q, k_ctx, v_ctx, kp, kp, kp, vp, vp, vp)


def _rms(x, g):
    xf = x.astype(F32)
    return xf * lax.rsqrt(jnp.mean(xf * xf, axis=-1, keepdims=True) + EPS) * g.astype(F32)


def _axial_rope(x):
    t, d = x.shape[1], x.shape[-1]
    rows = t // GRID_W
    r, col = jnp.meshgrid(jnp.arange(rows, dtype=F32), jnp.arange(GRID_W, dtype=F32), indexing='ij')
    n_freq = d // 4
    inv = ROPE_THETA ** (-jnp.arange(n_freq, dtype=F32) / n_freq)
    ang = jnp.concatenate([r.reshape(-1, 1) * inv, col.reshape(-1, 1) * inv], axis=-1)
    bshape = (1, t) + (1,) * (x.ndim - 3) + (d // 2,)
    cos, sin = jnp.cos(ang).reshape(bshape), jnp.sin(ang).reshape(bshape)
    x1, x2 = x[..., : d // 2], x[..., d // 2:]
    return jnp.concatenate([x1 * cos - x2 * sin, x2 * cos + x1 * sin], axis=-1)


def _rope_tail(x):
    return jnp.concatenate([x[..., :MLA_NOPE], _axial_rope(x[..., MLA_NOPE:])], axis=-1)


def _s5_scan(u, a_re, a_im, log_dt, b_re, b_im, c_re, c_im, h0, reverse):
    lam = lax.complex(a_re, a_im)
    lam_bar = jnp.exp(lam * jnp.exp(log_dt)[:, None])
    b_bar = ((lam_bar - 1.0) / lam)[..., None] * lax.complex(b_re, b_im)
    bu = jnp.einsum('btgi,gni->btgn', u.astype(jnp.complex64), b_bar, precision=lax.Precision.HIGHEST)
    a = jnp.broadcast_to(lam_bar, bu.shape)

    def combine(e1, e2):
        a1, b1 = e1
        a2, b2 = e2
        return a1 * a2, a2 * b1 + b2

    a_cum, h = lax.associative_scan(combine, (a, bu), axis=1, reverse=reverse)
    h = h + a_cum * h0[:, None]
    y = jnp.einsum('btgn,gin->btgi', h, lax.complex(c_re, c_im), precision=lax.Precision.HIGHEST).real
    h_last = h[:, 0] if reverse else h[:, -1]
    return y, h_last


def _s5_mixer(u, P, l, h0_f, h0_b):
    b, t, w = u.shape
    ug = u.reshape(b, t, w // SSM_CH, SSM_CH)
    ys, hs = [], []
    for d, (h0, rev) in enumerate(((h0_f, False), (h0_b, True))):
        y, hl = _s5_scan(ug, P['ssm_a_re'][l, d], P['ssm_a_im'][l, d], P['ssm_log_dt'][l, d], P['ssm_b_re'][l, d],
                         P['ssm_b_im'][l, d], P['ssm_c_re'][l, d], P['ssm_c_im'][l, d], h0, rev)
        ys.append(y)
        hs.append(hl)
    y = (ys[0] + ys[1]).reshape(b, t, w) + P['ssm_d'][l] * u
    state = jnp.stack([jnp.stack([hs[0].real, hs[0].imag], axis=1),
                       jnp.stack([hs[1].real, hs[1].imag], axis=1)], axis=1)
    return jax.nn.gelu(y), state


def _heads_first(x):
    return jnp.swapaxes(x, 1, 2)


def _moe_ffn(h_f32, h_bf, P, j, bm):
    m = h_f32.shape[0]
    logits = jnp.dot(h_f32, P['w_router'][j], precision=lax.Precision.HIGHEST)
    probs = jax.nn.softmax(logits, axis=-1)
    top_p, top_i = lax.top_k(probs, TOP_K)
    top_p = top_p / jnp.sum(top_p, axis=-1, keepdims=True)
    flat_e = top_i.reshape(-1).astype(jnp.int32)
    npairs = flat_e.shape[0]
    order = jnp.argsort(flat_e, stable=True).astype(jnp.int32)
    counts = jnp.sum(flat_e[:, None] == jnp.arange(N_EXPERTS, dtype=jnp.int32)[None, :], axis=0).astype(jnp.int32)
    blocks_per = (counts + bm - 1) // bm
    block_end = jnp.cumsum(blocks_per)
    pad_start = (block_end - blocks_per) * bm
    start = jnp.cumsum(counts) - counts
    sorted_e = flat_e[order]
    dest = pad_start[sorted_e] + jnp.arange(npairs, dtype=jnp.int32) - start[sorted_e]
    nrows = npairs + N_EXPERTS * bm
    row_token = jnp.zeros((nrows,), jnp.int32).at[dest].set(order // TOP_K)
    pos = jnp.zeros((npairs,), jnp.int32).at[order].set(dest).reshape(m, TOP_K)
    nblk = nrows // bm
    nused = block_end[-1:]
    block_expert = jnp.minimum(jnp.sum(jnp.arange(nblk, dtype=jnp.int32)[:, None] >= block_end[None, :], axis=1),
                               N_EXPERTS - 1).astype(jnp.int32)
    block_expert = jnp.where(jnp.arange(nblk) < nused[0], block_expert, block_expert[jnp.maximum(nused[0] - 1, 0)])
    x_sorted = jnp.take(h_bf, row_token, axis=0)
    act = moe_glu_up(x_sorted, P['w_moe_gate'], P['w_moe_up'], j, block_expert, nused, bm=bm)
    y_sorted = moe_down(act, P['w_moe_down'], j, block_expert, nused, bm=bm)
    out = jnp.zeros((m, y_sorted.shape[1]), F32)
    for kk in range(TOP_K):
        out = out + top_p[:, kk:kk + 1] * jnp.take(y_sorted, pos[:, kk], axis=0)
    return out


IN_SEGMENTS = ('q', 'ckv', 'ssm', 'swa', 'diff', 'gate', 'krope')


def _arrange_w_in(w_in_l):
    d = w_in_l.shape[0]
    q_rank = d // 4
    kv_rank = d // 4
    c0 = q_rank
    c1 = c0 + kv_rank
    c2 = c1 + MLA_ROPE
    pad = (-w_in_l.shape[1]) % 512
    return jnp.concatenate([w_in_l[:, :c1], w_in_l[:, c2:], w_in_l[:, c1:c2],
                            jnp.zeros((d, pad), w_in_l.dtype)], axis=1).astype(BF16)


def kernel(x_prompt, x_sample, cache_mla_ckv, cache_mla_krope, state_ssm, cache_swa_k, cache_swa_v, cache_diff_k, cache_diff_v, c, c_ctx, w_ada, b_ada, g_norm1, g_norm2, w_in, g_mla_qlat, g_mla_kvlat, w_mla_uq, w_mla_ukv, g_mla_q, g_mla_k, ssm_a_re, ssm_a_im, ssm_log_dt, ssm_b_re, ssm_b_im, ssm_c_re, ssm_c_im, ssm_d, w_ssm_glu, g_swa_q, g_swa_k, swa_sink, g_diff_q, g_diff_k, diff_lambda, g_diff_out, w_branch, w_out, w_ffn_gate, w_ffn_up, w_ffn_down, w_router, w_moe_gate, w_moe_up, w_moe_down):
    P = dict(ssm_a_re=ssm_a_re, ssm_a_im=ssm_a_im, ssm_log_dt=ssm_log_dt, ssm_b_re=ssm_b_re, ssm_b_im=ssm_b_im,
             ssm_c_re=ssm_c_re, ssm_c_im=ssm_c_im, ssm_d=ssm_d, w_router=w_router, w_moe_gate=w_moe_gate,
             w_moe_up=w_moe_up, w_moe_down=w_moe_down)
    bp, tp, d = x_prompt.shape
    bs, ts, _ = x_sample.shape
    depth = w_in.shape[0]
    past = cache_mla_ckv.shape[2]
    mp, ms = bp * tp, bs * ts
    m = mp + ms
    w = d // 4
    groups = w // SSM_CH
    x = jnp.concatenate([x_prompt.reshape(mp, d), x_sample.reshape(ms, d)], axis=0)

    seg = math.gcd(mp, ts)
    nseg = m // seg
    seg_mod = np.concatenate([np.zeros(mp // seg, np.int32), 1 + np.repeat(np.arange(bs, dtype=np.int32), ts // seg)])
    cvec = jnp.concatenate([c_ctx[None, :], c, jnp.zeros((8 - 1 - bs, d), F32)], axis=0)
    cvec = cvec * jax.nn.sigmoid(cvec)

    def per_row(v):
        return v[seg_mod][:, None, :]

    def modulated_norm(xx, g, sc, sh):
        y = _rms(xx, g).reshape(nseg, seg, d) * (1 + per_row(sc)) + per_row(sh)
        return y.reshape(m, d)

    def gated_add(xx, gt, upd):
        return (xx.reshape(nseg, seg, d) + per_row(gt) * upd.reshape(nseg, seg, d)).reshape(m, d)

    outs = [[] for _ in range(7)]
    for l in range(depth):
        lambda_init = 0.8 - 0.6 * math.exp(-0.3 * l)
        mods = matmul(cvec, w_ada, (l,), bm=8, bn=1024) + b_ada[l][None, :]
        sh1, sc1, gt1, sh2, sc2, gt2 = jnp.split(mods, 6, axis=-1)

        h = modulated_norm(x, g_norm1[l], sc1, sh1).astype(BF16)
        p = matmul(h, _arrange_w_in(w_in[l]), out_dtype=F32)
        o0 = 0
        seg_cols = {}
        for name, width in zip(IN_SEGMENTS, (w, w, w, (SWA_HEADS + 2 * SWA_KV_HEADS) * SWA_HD,
                                             2 * DIFF_HEADS * 2 * DIFF_HD + DIFF_HEADS * DIFF_VD, N_BRANCH * d, MLA_ROPE)):
            seg_cols[name] = p[:, o0:o0 + width]
            o0 += width

        c_kv = _rms(seg_cols['ckv'], g_mla_kvlat[l])
        k_rope = seg_cols['krope']
        q_lat = _rms(seg_cols['q'], g_mla_qlat[l]).astype(BF16)
        q_m = _rms(matmul(q_lat, w_mla_uq, (l,)).reshape(m, MLA_HEADS, MLA_QK), g_mla_q[l])
        ckv_all = jnp.concatenate([c_kv, cache_mla_ckv[:, l].reshape(bs * past, w)], axis=0).astype(BF16)
        krope_all = jnp.concatenate([k_rope, cache_mla_krope[:, l].reshape(bs * past, MLA_ROPE)], axis=0)
        kv = matmul(ckv_all, w_mla_ukv, (l,), bm=512).reshape(-1, MLA_HEADS, MLA_NOPE + MLA_V)
        k_r = jnp.broadcast_to(krope_all[:, None, :], (kv.shape[0], MLA_HEADS, MLA_ROPE))
        k_m = _rms(jnp.concatenate([kv[..., :MLA_NOPE], k_r], axis=-1), g_mla_k[l])
        v_m = kv[..., MLA_NOPE:]
        qp = _heads_first(q_m[:mp].reshape(bp, tp, MLA_HEADS, MLA_QK)).astype(BF16)
        kp_ = _heads_first(k_m[:mp].reshape(bp, tp, MLA_HEADS, MLA_QK)).astype(BF16)
        vp_ = _heads_first(v_m[:mp].reshape(bp, tp, MLA_HEADS, MLA_V)).astype(BF16)
        o_mla_p = attention(qp, kp_, vp_, MLA_QK ** -0.5)
        qs = _heads_first(_rope_tail(q_m[mp:].reshape(bs, ts, MLA_HEADS, MLA_QK))).astype(BF16)
        k_lat = _rope_tail(k_m[mp:m].reshape(bs, ts, MLA_HEADS, MLA_QK))
        k_cat = jnp.concatenate([k_m[m:].reshape(bs, past, MLA_HEADS, MLA_QK), k_lat], axis=1)
        v_cat = jnp.concatenate([v_m[m:].reshape(bs, past, MLA_HEADS, MLA_V), v_m[mp:m].reshape(bs, ts, MLA_HEADS, MLA_V)], axis=1)
        o_mla_s = attention(qs, _heads_first(k_cat).astype(BF16), _heads_first(v_cat).astype(BF16), MLA_QK ** -0.5)
        o_mla = jnp.concatenate([_heads_first(o_mla_p).reshape(mp, w), _heads_first(o_mla_s).reshape(ms, w)], axis=0)

        u = seg_cols['ssm']
        zero = jnp.zeros((bp, groups, SSM_N), jnp.complex64)
        g_p, ssm_state = _s5_mixer(u[:mp].reshape(bp, tp, w), P, l, zero, zero)
        st = state_ssm[:, l]
        g_s, _ = _s5_mixer(u[mp:].reshape(bs, ts, w), P, l, lax.complex(st[:, 0, 0], st[:, 0, 1]),
                           lax.complex(st[:, 1, 0], st[:, 1, 1]))
        g_all = jnp.concatenate([g_p.reshape(mp, w), g_s.reshape(ms, w)], axis=0)
        glu = matmul(g_all.astype(BF16), w_ssm_glu, (l,))
        o_ssm = (g_all * jax.nn.sigmoid(glu)).astype(BF16)

        ps = seg_cols['swa']
        nq = SWA_HEADS * SWA_HD
        nk = SWA_KV_HEADS * SWA_HD
        q_s = _rms(ps[:, :nq].reshape(m, SWA_HEADS, SWA_HD), g_swa_q[l])
        k_s = _rms(ps[:, nq:nq + nk].reshape(m, SWA_KV_HEADS, SWA_HD), g_swa_k[l])
        v_s = ps[:, nq + nk:].reshape(m, SWA_KV_HEADS, SWA_HD)
        o_swa_p = attention(_heads_first(q_s[:mp].reshape(bp, tp, SWA_HEADS, SWA_HD)).astype(BF16),
                            _heads_first(k_s[:mp].reshape(bp, tp, SWA_KV_HEADS, SWA_HD)).astype(BF16),
                            _heads_first(v_s[:mp].reshape(bp, tp, SWA_KV_HEADS, SWA_HD)).astype(BF16),
                            SWA_HD ** -0.5, swa_sink[l])
        q_sl = _axial_rope(q_s[mp:].reshape(bs, ts, SWA_KV_HEADS, SWA_REP, SWA_HD))
        k_sl = _axial_rope(k_s[mp:].reshape(bs, ts, SWA_KV_HEADS, SWA_HD))
        o_swa_s = swa_latent_attention(
            jnp.transpose(q_sl, (0, 2, 3, 1, 4)).astype(BF16), _heads_first(k_sl).astype(BF16),
            _heads_first(v_s[mp:].reshape(bs, ts, SWA_KV_HEADS, SWA_HD)).astype(BF16),
            _heads_first(cache_swa_k[:, l]).astype(BF16), _heads_first(cache_swa_v[:, l]).astype(BF16), swa_sink[l])
        o_swa = jnp.concatenate([_heads_first(o_swa_p).reshape(mp, w),
                                 jnp.transpose(o_swa_s, (0, 3, 1, 2, 4)).reshape(ms, w)], axis=0)

        pd = seg_cols['diff']
        ndq = DIFF_HEADS * 2 * DIFF_HD
        q_d = _rms(pd[:, :ndq].reshape(m, DIFF_HEADS, 2, DIFF_HD), g_diff_q[l])
        k_d = _rms(pd[:, ndq:2 * ndq].reshape(m, DIFF_HEADS, 2, DIFF_HD), g_diff_k[l])
        v_d = pd[:, 2 * ndq:].reshape(m, DIFF_HEADS, DIFF_VD)
        lp = diff_lambda[l]
        lam = jnp.exp(jnp.sum(lp[0] * lp[1], -1)) - jnp.exp(jnp.sum(lp[2] * lp[3], -1)) + lambda_init

        def maps_first(z, bb, tt):
            return jnp.transpose(z.reshape(bb, tt, DIFF_HEADS, 2, DIFF_HD), (0, 2, 3, 1, 4))

        o_diff_p = diff_attention(maps_first(q_d[:mp], bp, tp).astype(BF16), maps_first(k_d[:mp], bp, tp).astype(BF16),
                                  _heads_first(v_d[:mp].reshape(bp, tp, DIFF_HEADS, DIFF_VD)).astype(BF16),
                                  lam, g_diff_out[l], lambda_init)
        q_dl = _axial_rope(q_d[mp:].reshape(bs, ts, DIFF_HEADS, 2, DIFF_HD))
        k_dl = jnp.concatenate([cache_diff_k[:, l], _axial_rope(k_d[mp:].reshape(bs, ts, DIFF_HEADS, 2, DIFF_HD))], axis=1)
        v_dl = jnp.concatenate([cache_diff_v[:, l], v_d[mp:].reshape(bs, ts, DIFF_HEADS, DIFF_VD)], axis=1)
        o_diff_s = diff_attention(jnp.transpose(q_dl, (0, 2, 3, 1, 4)).astype(BF16),
                                  jnp.transpose(k_dl, (0, 2, 3, 1, 4)).astype(BF16),
                                  _heads_first(v_dl).astype(BF16), lam, g_diff_out[l], lambda_init)
        o_diff = jnp.concatenate([_heads_first(o_diff_p).reshape(mp, w), _heads_first(o_diff_s).reshape(ms, w)], axis=0)

        gates = seg_cols['gate']
        merged = jnp.zeros((m, d), F32)
        for n_, o_n in enumerate((o_mla, o_ssm, o_swa, o_diff)):
            merged = merged + jax.nn.sigmoid(gates[:, n_ * d:(n_ + 1) * d]) * matmul(o_n, w_branch, (l, n_))
        mix = matmul(merged.astype(BF16), w_out, (l,))
        x = gated_add(x, gt1, mix)

        h2 = modulated_norm(x, g_norm2[l], sc2, sh2)
        h2_bf = h2.astype(BF16)
        j = l // 2
        if l % 2 == 0:
            act = glu_up(h2_bf, w_ffn_gate, w_ffn_up, (j,))
            ffn = matmul(act, w_ffn_down, (j,), bm=512)
        else:
            ffn = _moe_ffn(h2, h2_bf, P, j, bm=256)
        x = gated_add(x, gt2, ffn)

        outs[0].append(c_kv[:mp].reshape(bp, tp, w))
        outs[1].append(k_rope[:mp].reshape(bp, tp, MLA_ROPE))
        outs[2].append(ssm_state)
        outs[3].append(k_s[:mp].reshape(bp, tp, SWA_KV_HEADS, SWA_HD))
        outs[4].append(v_s[:mp].reshape(bp, tp, SWA_KV_HEADS, SWA_HD))
        outs[5].append(k_d[:mp].reshape(bp, tp, DIFF_HEADS, 2, DIFF_HD))
        outs[6].append(v_d[:mp].reshape(bp, tp, DIFF_HEADS, DIFF_VD))

    return (x[:mp].reshape(bp, tp, d), x[mp:].reshape(bs, ts, d)) + tuple(jnp.stack(o, axis=1) for o in outs)
```

```python
import functools
import math

import numpy as np
import jax
import jax.numpy as jnp
from jax import lax
from jax.experimental import pallas as pl
from jax.experimental.pallas import tpu as pltpu

F32 = jnp.float32
BF16 = jnp.bfloat16
EPS = 1e-6
NEG_INF = -1e30
ROPE_THETA = 10000.0
GRID_W = 64
BLOCK = 128
N_BRANCH = 4
MLA_HEADS = 4
MLA_NOPE = 128
MLA_ROPE = 64
MLA_QK = MLA_NOPE + MLA_ROPE
MLA_V = 128
SSM_CH = 16
SSM_N = 64
SWA_HEADS = 8
SWA_KV_HEADS = 2
SWA_REP = SWA_HEADS // SWA_KV_HEADS
SWA_HD = 64
WINDOW = 128
DIFF_HEADS = 4
DIFF_HD = 64
DIFF_VD = 2 * DIFF_HD
N_EXPERTS = 8
TOP_K = 2

V7X_VMEM_REQUEST_CAP = 56 * 1024 * 1024


def _params(semantics, vmem_bytes):
    limit = int(min(max(vmem_bytes * 5 // 4 + (2 << 20), 16 << 20), V7X_VMEM_REQUEST_CAP))
    return pltpu.CompilerParams(dimension_semantics=semantics, vmem_limit_bytes=limit)


def _tile(n, want):
    t = min(n, want)
    while n % t:
        t -= 1
    return t


def _nbytes(shape, dtype):
    return int(np.prod(shape)) * jnp.dtype(dtype).itemsize


def _mm_kernel(x_ref, w_ref, o_ref, *scratch):
    if scratch:
        wbf_ref, = scratch

        @pl.when(pl.program_id(1) == 0)
        def _():
            wbf_ref[...] = w_ref[...].astype(BF16)
        w = wbf_ref[...]
    else:
        w = w_ref[...]
    o_ref[...] = jnp.dot(x_ref[...].astype(BF16), w, preferred_element_type=F32).astype(o_ref.dtype)


def matmul(x, w, lead=(), *, bm=1024, bn=512, out_dtype=F32):
    m, k = x.shape
    n = w.shape[-1]
    assert w.shape[-2] == k
    bm, bn = _tile(m, bm), _tile(n, bn)
    assert m % bm == 0 and n % bn == 0, (m, n, bm, bn)
    nl = len(lead)
    cast = w.dtype != BF16
    scratch = [pltpu.VMEM((k, bn), BF16)] if cast else []
    vmem = (2 * _nbytes((bm, k), x.dtype) + 2 * _nbytes((k, bn), w.dtype)
            + 2 * _nbytes((bm, bn), out_dtype) + (_nbytes((k, bn), BF16) if cast else 0))
    return pl.pallas_call(
        _mm_kernel, name="matmul",
        grid=(n // bn, m // bm),
        in_specs=[pl.BlockSpec((bm, k), lambda j, i: (i, 0)),
                  pl.BlockSpec((None,) * nl + (k, bn), lambda j, i: lead + (0, j))],
        out_specs=pl.BlockSpec((bm, bn), lambda j, i: (i, j)),
        out_shape=jax.ShapeDtypeStruct((m, n), out_dtype),
        scratch_shapes=scratch,
        compiler_params=_params(("arbitrary", "arbitrary"), vmem),
    )(x, w)


def _glu_kernel(x_ref, wg_ref, wu_ref, o_ref, wg_bf, wu_bf):
    @pl.when(pl.program_id(1) == 0)
    def _():
        wg_bf[...] = wg_ref[...].astype(BF16)
        wu_bf[...] = wu_ref[...].astype(BF16)
    x = x_ref[...]
    a = jnp.dot(x, wg_bf[...], preferred_element_type=F32)
    b = jnp.dot(x, wu_bf[...], preferred_element_type=F32)
    o_ref[...] = (a * jax.nn.sigmoid(a) * b).astype(o_ref.dtype)


def glu_up(x, wg, wu, lead, *, bm=1024, bn=512):
    m, k = x.shape
    n = wg.shape[-1]
    bm, bn = _tile(m, bm), _tile(n, bn)
    assert m % bm == 0 and n % bn == 0
    nl = len(lead)
    wspec = pl.BlockSpec((None,) * nl + (k, bn), lambda j, i: lead + (0, j))
    vmem = (2 * _nbytes((bm, k), x.dtype) + 4 * _nbytes((k, bn), F32) + 2 * _nbytes((k, bn), BF16)
            + 2 * _nbytes((bm, bn), BF16) + 3 * _nbytes((bm, bn), F32))
    return pl.pallas_call(
        _glu_kernel, name="glu_up",
        grid=(n // bn, m // bm),
        in_specs=[pl.BlockSpec((bm, k), lambda j, i: (i, 0)), wspec, wspec],
        out_specs=pl.BlockSpec((bm, bn), lambda j, i: (i, j)),
        out_shape=jax.ShapeDtypeStruct((m, n), BF16),
        scratch_shapes=[pltpu.VMEM((k, bn), BF16), pltpu.VMEM((k, bn), BF16)],
        compiler_params=_params(("arbitrary", "arbitrary"), vmem),
    )(x, wg, wu)


def _mm_res_kernel(a_ref, w_ref, x_ref, gt_ref, o_ref, wbf_ref):
    @pl.when(pl.program_id(1) == 0)
    def _():
        wbf_ref[...] = w_ref[...].astype(BF16)
    upd = jnp.dot(a_ref[...], wbf_ref[...], preferred_element_type=F32)
    o_ref[...] = x_ref[...] + gt_ref[...] * upd


def matmul_residual(a, w, lead, x, gt_seg, *, bm=512, bn=512):
    m, k = a.shape
    n = w.shape[-1]
    seg = m // gt_seg.shape[0]
    bm, bn = _tile(seg, bm), _tile(n, bn)
    nl = len(lead)
    vmem = (2 * _nbytes((bm, k), a.dtype) + 2 * _nbytes((k, bn), F32) + _nbytes((k, bn), BF16)
            + 5 * _nbytes((bm, bn), F32))
    return pl.pallas_call(
        _mm_res_kernel, name="matmul_residual",
        grid=(n // bn, m // bm),
        in_specs=[pl.BlockSpec((bm, k), lambda j, i: (i, 0)),
                  pl.BlockSpec((None,) * nl + (k, bn), lambda j, i: lead + (0, j)),
                  pl.BlockSpec((bm, bn), lambda j, i: (i, j)),
                  pl.BlockSpec((None, 1, bn), lambda j, i: (i * bm // seg, 0, j))],
        out_specs=pl.BlockSpec((bm, bn), lambda j, i: (i, j)),
        out_shape=jax.ShapeDtypeStruct((m, n), F32),
        scratch_shapes=[pltpu.VMEM((k, bn), BF16)],
        compiler_params=_params(("arbitrary", "arbitrary"), vmem),
    )(a, w, x, gt_seg)


def _merge_kernel(*refs):
    nb = N_BRANCH
    o_refs, g_refs = refs[:nb], refs[nb:2 * nb]
    wb_ref, out_ref, wbf_ref = refs[2 * nb:]

    @pl.when(pl.program_id(1) == 0)
    def _():
        wbf_ref[...] = wb_ref[...].astype(BF16)
    acc = None
    for n_ in range(nb):
        up = jnp.dot(o_refs[n_][...], wbf_ref[n_], preferred_element_type=F32)
        t = jax.nn.sigmoid(g_refs[n_][...]) * up
        acc = t if acc is None else acc + t
    out_ref[...] = acc.astype(out_ref.dtype)


def merge_branches(branch_outs, p, gate_col0, w_branch, layer, *, bm=512, bn=512):
    m, bw = branch_outs[0].shape
    dm = w_branch.shape[-1]
    bm, bn = _tile(m, bm), _tile(dm, bn)
    assert gate_col0 % bn == 0
    o_spec = pl.BlockSpec((bm, bw), lambda j, i: (i, 0))
    g_specs = [pl.BlockSpec((bm, bn), functools.partial(lambda j, i, c: (i, c + j), c=(gate_col0 + n_ * dm) // bn))
               for n_ in range(N_BRANCH)]
    vmem = (N_BRANCH * (2 * _nbytes((bm, bw), BF16) + 2 * _nbytes((bm, bn), F32) + 2 * _nbytes((bw, bn), F32)
                        + _nbytes((bw, bn), BF16)) + 6 * _nbytes((bm, bn), F32))
    return pl.pallas_call(
        _merge_kernel, name="merge_branches",
        grid=(dm // bn, m // bm),
        in_specs=[o_spec] * N_BRANCH + g_specs
                 + [pl.BlockSpec((None, N_BRANCH, bw, bn), lambda j, i: (layer, 0, 0, j))],
        out_specs=pl.BlockSpec((bm, bn), lambda j, i: (i, j)),
        out_shape=jax.ShapeDtypeStruct((m, dm), BF16),
        scratch_shapes=[pltpu.VMEM((N_BRANCH, bw, bn), BF16)],
        compiler_params=_params(("arbitrary", "arbitrary"), vmem),
    )(*branch_outs, *([p] * N_BRANCH), w_branch)


def _modnorm_kernel(*refs, with_router):
    if with_router:
        x_ref, g_ref, sc_ref, sh_ref, wr_ref, h_ref, logit_ref = refs
    else:
        x_ref, g_ref, sc_ref, sh_ref, h_ref = refs
    x = x_ref[...]
    y = x * lax.rsqrt(jnp.mean(x * x, axis=-1, keepdims=True) + EPS) * g_ref[...]
    h = y * (1 + sc_ref[...]) + sh_ref[...]
    h_ref[...] = h.astype(h_ref.dtype)
    if with_router:
        logit_ref[...] = jnp.dot(h, wr_ref[...], precision=lax.Precision.HIGHEST, preferred_element_type=F32)


def modulated_norm(x, g, sc_seg, sh_seg, w_router=None, router_layer=0, *, bm=256):
    m, d = x.shape
    seg = m // sc_seg.shape[0]
    bm = _tile(seg, bm)
    mod_spec = pl.BlockSpec((None, 1, d), lambda i: (i * bm // seg, 0, 0))
    in_specs = [pl.BlockSpec((bm, d), lambda i: (i, 0)), pl.BlockSpec((1, d), lambda i: (0, 0)), mod_spec, mod_spec]
    out_specs = [pl.BlockSpec((bm, d), lambda i: (i, 0))]
    out_shape = [jax.ShapeDtypeStruct((m, d), BF16)]
    args = [x, g.reshape(1, d), sc_seg, sh_seg]
    if w_router is not None:
        ne = w_router.shape[-1]
        in_specs.append(pl.BlockSpec((None, d, ne), lambda i: (router_layer, 0, 0)))
        out_specs.append(pl.BlockSpec((bm, ne), lambda i: (i, 0)))
        out_shape.append(jax.ShapeDtypeStruct((m, ne), F32))
        args.append(w_router)
    vmem = 8 * _nbytes((bm, d), F32) + 2 * _nbytes((d, 128), F32)
    out = pl.pallas_call(
        functools.partial(_modnorm_kernel, with_router=w_router is not None), name="modulated_norm",
        grid=(m // bm,),
        in_specs=in_specs, out_specs=out_specs, out_shape=out_shape,
        compiler_params=_params(("arbitrary",), vmem),
    )(*args)
    return out if w_router is not None else out[0]


S5_CHUNK = 64
S5_HALF = 256


def _s5_tables(a_re, a_im, log_dt, b_re, b_im, c_re, c_im):
    ndir, g, n = a_re.shape
    ch = b_re.shape[-1]
    gh = S5_HALF // ch
    nh = g // gh
    dt = jnp.exp(log_dt)[..., None]
    er, ei = a_re * dt, a_im * dt
    mag = jnp.exp(er)
    lb_re, lb_im = mag * jnp.cos(ei), mag * jnp.sin(ei)
    den = a_re * a_re + a_im * a_im
    xr, xi = lb_re - 1.0, lb_im
    co_re, co_im = (xr * a_re + xi * a_im) / den, (xi * a_re - xr * a_im) / den
    bb_re = co_re[..., None] * b_re - co_im[..., None] * b_im
    bb_im = co_re[..., None] * b_im + co_im[..., None] * b_re
    eye = jnp.eye(gh, dtype=F32)

    def blockdiag_in(z):
        z = z.reshape(ndir, nh, gh, n, ch)
        return jnp.einsum('dhgnj,gk->dhgjkn', z, eye).reshape(ndir, nh, gh * ch, gh * n)

    def blockdiag_out(z):
        z = z.reshape(ndir, nh, gh, ch, n)
        return jnp.einsum('dhgin,gk->dhkngi', z, eye).reshape(ndir, nh, gh * n, gh * ch)

    bmat = jnp.concatenate([blockdiag_in(bb_re), blockdiag_in(bb_im)], axis=-1).astype(BF16)
    cmat = jnp.concatenate([blockdiag_out(c_re), -blockdiag_out(c_im)], axis=-2).astype(BF16)

    def halves(z):
        return z.reshape(z.shape[:-2] + (nh, gh * n))

    steps = jnp.arange(S5_CHUNK, dtype=F32)
    expo = jnp.stack([steps, S5_CHUNK - 1 - steps])[:, :, None, None]

    def powers(sign):
        pr, pi = sign * expo * er[:, None], sign * expo * ei[:, None]
        pm = jnp.exp(pr)
        tab = jnp.concatenate([halves(pm * jnp.cos(pi)), halves(pm * jnp.sin(pi))], axis=-1)
        return jnp.swapaxes(tab, 1, 2)

    lam = jnp.concatenate([halves(lb_re), halves(lb_im)], axis=-1)[:, :, None, :]
    ti = np.arange(S5_CHUNK)
    tri = jnp.asarray(np.stack([ti[:, None] >= ti[None, :], ti[:, None] <= ti[None, :]]), BF16)
    return dict(bmat=bmat, cmat=cmat, p=powers(1.0), pinv=powers(-1.0), lam=lam, tri=tri)


def _s5_kernel(u_ref, bm_ref, cm_ref, p_ref, pinv_ref, lam_ref, tri_ref, h0_ref, y_ref, hT_ref, hs_ref, st_ref,
               *, nb, nchunks):
    d, c = pl.program_id(0), pl.program_id(3)
    L = S5_CHUNK
    half = p_ref.shape[-1] // 2

    @pl.when(c == 0)
    def _():
        st_ref[...] = h0_ref[...]

    u = u_ref[...].reshape(nb * L, u_ref.shape[-1]).astype(BF16)
    bu = jnp.dot(u, bm_ref[...], preferred_element_type=F32)
    p_re, p_im = p_ref[:, :half], p_ref[:, half:]
    q_re, q_im = pinv_ref[:, :half], pinv_ref[:, half:]
    lam_re, lam_im = lam_ref[:, :half], lam_ref[:, half:]
    tri = tri_ref[...]
    for i in range(nb):
        b_re, b_im = bu[i * L:(i + 1) * L, :half], bu[i * L:(i + 1) * L, half:]
        z = jnp.concatenate([b_re * q_re - b_im * q_im, b_re * q_im + b_im * q_re], axis=1)
        z_hi = z.astype(BF16)
        z_lo = (z - z_hi.astype(F32)).astype(BF16)
        w = jnp.dot(tri, z_hi, preferred_element_type=F32) + jnp.dot(tri, z_lo, preferred_element_type=F32)
        s_re, s_im = st_ref[i:i + 1, :half], st_ref[i:i + 1, half:]
        w_re = w[:, :half] + (lam_re * s_re - lam_im * s_im)
        w_im = w[:, half:] + (lam_re * s_im + lam_im * s_re)
        h = jnp.concatenate([p_re * w_re - p_im * w_im, p_re * w_im + p_im * w_re], axis=1)
        hs_ref[i * L:(i + 1) * L, :] = h.astype(BF16)
        st_ref[i:i + 1, :] = jnp.where(d == 0, h[L - 1:L, :], h[0:1, :])
    y = jnp.dot(hs_ref[...], cm_ref[...], preferred_element_type=F32)
    y_ref[...] = y.reshape(y_ref.shape)

    @pl.when(c == nchunks - 1)
    def _():
        hT_ref[...] = st_ref[...]


def s5_scan(p3, col0, seq0, n_seq, tabs, h0):
    _, t, _ = p3.shape
    ndir, nh, hc, ns2 = tabs['bmat'].shape
    L = S5_CHUNK
    nb = _tile(n_seq, 8)
    assert t % L == 0 and seq0 % nb == 0 and col0 % hc == 0
    nchunks = t // L
    sg0, cb0 = seq0 // nb, col0 // hc

    def tchunk(d, c):
        return c + d * (nchunks - 1 - 2 * c)

    tab = lambda r, cc: pl.BlockSpec((None, None, r, cc), lambda d, hf, sg, c: (d, hf, 0, 0))
    vmem = (4 * _nbytes((nb * L, hc), F32) + 4 * _nbytes((hc, ns2), BF16) + 4 * _nbytes((L, ns2), F32)
            + 3 * _nbytes((nb * L, ns2), F32) + 8 * _nbytes((L, ns2), F32))
    return pl.pallas_call(
        functools.partial(_s5_kernel, nb=nb, nchunks=nchunks), name="s5_scan",
        grid=(ndir, nh, n_seq // nb, nchunks),
        in_specs=[pl.BlockSpec((nb, L, hc), lambda d, hf, sg, c: (sg0 + sg, tchunk(d, c), cb0 + hf)),
                  tab(hc, ns2), tab(ns2, hc), tab(L, ns2), tab(L, ns2), tab(1, ns2),
                  pl.BlockSpec((None, L, L), lambda d, hf, sg, c: (d, 0, 0)),
                  pl.BlockSpec((None, None, nb, ns2), lambda d, hf, sg, c: (d, hf, sg, 0))],
        out_specs=[pl.BlockSpec((None, nb, L, hc), lambda d, hf, sg, c: (d, sg, tchunk(d, c), hf)),
                   pl.BlockSpec((None, None, nb, ns2), lambda d, hf, sg, c: (d, hf, sg, 0))],
        out_shape=[jax.ShapeDtypeStruct((ndir, n_seq, t, nh * hc), F32),
                   jax.ShapeDtypeStruct((ndir, nh, n_seq, ns2), F32)],
        scratch_shapes=[pltpu.VMEM((nb * L, ns2), BF16), pltpu.VMEM((nb, ns2), F32)],
        compiler_params=_params(("arbitrary",) * 4, vmem),
    )(p3, tabs['bmat'], tabs['cmat'], tabs['p'], tabs['pinv'], tabs['lam'], tabs['tri'], h0)


def _ssm_post_kernel(y_ref, u_ref, d_ref, w_ref, o_ref, wbf_ref):
    @pl.when(pl.program_id(0) == 0)
    def _():
        wbf_ref[...] = w_ref[...].astype(BF16)
    y = y_ref[0] + y_ref[1] + d_ref[...] * u_ref[...]
    g = jax.nn.gelu(y)
    z = jnp.dot(g.astype(BF16), wbf_ref[...], preferred_element_type=F32)
    o_ref[...] = (g * jax.nn.sigmoid(z)).astype(o_ref.dtype)


def ssm_post(y2, p, col0, d_vec, w_glu, layer, *, bm=1024):
    _, m, w = y2.shape
    bm = _tile(m, bm)
    assert col0 % w == 0
    vmem = 6 * _nbytes((bm, w), F32) * 2 + 3 * _nbytes((w, w), F32)
    return pl.pallas_call(
        _ssm_post_kernel, name="ssm_post",
        grid=(m // bm,),
        in_specs=[pl.BlockSpec((2, bm, w), lambda i: (0, i, 0)),
                  pl.BlockSpec((bm, w), lambda i: (i, col0 // w)),
                  pl.BlockSpec((1, w), lambda i: (0, 0)),
                  pl.BlockSpec((None, w, w), lambda i: (layer, 0, 0))],
        out_specs=pl.BlockSpec((bm, w), lambda i: (i, 0)),
        out_shape=jax.ShapeDtypeStruct((m, w), BF16),
        scratch_shapes=[pltpu.VMEM((w, w), BF16)],
        compiler_params=_params(("arbitrary",), vmem),
    )(y2, p, d_vec.reshape(1, w), w_glu)


def _is_new_expert(be_ref, i):
    return (i == 0) | (be_ref[i] != be_ref[jnp.maximum(i - 1, 0)])


def _moe_glu_kernel(be_ref, nb_ref, x_ref, wg_ref, wu_ref, o_ref, wg_bf, wu_bf):
    i = pl.program_id(1)

    @pl.when(_is_new_expert(be_ref, i))
    def _():
        wg_bf[...] = wg_ref[...].astype(BF16)
        wu_bf[...] = wu_ref[...].astype(BF16)

    @pl.when(i < nb_ref[0])
    def _():
        x = x_ref[...]
        a = jnp.dot(x, wg_bf[...], preferred_element_type=F32)
        b = jnp.dot(x, wu_bf[...], preferred_element_type=F32)
        o_ref[...] = (a * jax.nn.sigmoid(a) * b).astype(o_ref.dtype)

    @pl.when(i >= nb_ref[0])
    def _():
        o_ref[...] = jnp.zeros_like(o_ref)


def moe_glu_up(x, wg, wu, layer, block_expert, nblocks, *, bm, bn=512):
    r, k = x.shape
    n = wg.shape[-1]
    assert r % bm == 0 and n % bn == 0
    wspec = pl.BlockSpec((None, None, k, bn), lambda j, i, be, nb: (layer, be[i], 0, j))
    vmem = (2 * _nbytes((bm, k), x.dtype) + 4 * _nbytes((k, bn), F32) + 2 * _nbytes((k, bn), BF16)
            + 2 * _nbytes((bm, bn), BF16) + 3 * _nbytes((bm, bn), F32))
    return pl.pallas_call(
        _moe_glu_kernel, name="moe_glu_up",
        grid_spec=pltpu.PrefetchScalarGridSpec(
            num_scalar_prefetch=2,
            grid=(n // bn, r // bm),
            in_specs=[pl.BlockSpec((bm, k), lambda j, i, be, nb: (i, 0)), wspec, wspec],
            out_specs=pl.BlockSpec((bm, bn), lambda j, i, be, nb: (i, j)),
            scratch_shapes=[pltpu.VMEM((k, bn), BF16), pltpu.VMEM((k, bn), BF16)]),
        out_shape=jax.ShapeDtypeStruct((r, n), BF16),
        compiler_params=_params(("arbitrary", "arbitrary"), vmem),
    )(block_expert, nblocks, x, wg, wu)


def _moe_down_kernel(be_ref, nb_ref, x_ref, w_ref, o_ref, w_bf):
    i = pl.program_id(1)

    @pl.when(_is_new_expert(be_ref, i))
    def _():
        w_bf[...] = w_ref[...].astype(BF16)

    @pl.when(i < nb_ref[0])
    def _():
        o_ref[...] = jnp.dot(x_ref[...], w_bf[...], preferred_element_type=F32)

    @pl.when(i >= nb_ref[0])
    def _():
        o_ref[...] = jnp.zeros_like(o_ref)


def moe_down(x, w, layer, block_expert, nblocks, *, bm, bn=512):
    r, k = x.shape
    n = w.shape[-1]
    assert r % bm == 0 and n % bn == 0
    vmem = (2 * _nbytes((bm, k), x.dtype) + 2 * _nbytes((k, bn), F32) + _nbytes((k, bn), BF16)
            + 2 * _nbytes((bm, bn), F32))
    return pl.pallas_call(
        _moe_down_kernel, name="moe_down",
        grid_spec=pltpu.PrefetchScalarGridSpec(
            num_scalar_prefetch=2,
            grid=(n // bn, r // bm),
            in_specs=[pl.BlockSpec((bm, k), lambda j, i, be, nb: (i, 0)),
                      pl.BlockSpec((None, None, k, bn), lambda j, i, be, nb: (layer, be[i], 0, j))],
            out_specs=pl.BlockSpec((bm, bn), lambda j, i, be, nb: (i, j)),
            scratch_shapes=[pltpu.VMEM((k, bn), BF16)]),
        out_shape=jax.ShapeDtypeStruct((r, n), F32),
        compiler_params=_params(("arbitrary", "arbitrary"), vmem),
    )(block_expert, nblocks, x, w)


def _qk(q, k):
    return lax.dot_general(q, k, (((1,), (1,)), ((), ())), preferred_element_type=F32)


def _attn_kernel(*refs, scale, use_sink):
    if use_sink:
        sink_ref, q_ref, k_ref, v_ref, o_ref = refs
    else:
        q_ref, k_ref, v_ref, o_ref = refs
    s = _qk(q_ref[...], k_ref[...]) * scale
    m = jnp.max(s, axis=-1, keepdims=True)
    if use_sink:
        sink = sink_ref[pl.program_id(1)]
        m = jnp.maximum(m, sink)
    p = jnp.exp(s - m)
    l = jnp.sum(p, axis=-1, keepdims=True)
    if use_sink:
        l = l + jnp.exp(sink - m)
    o = jnp.dot(p.astype(BF16), v_ref[...], preferred_element_type=F32)
    o_ref[...] = (o / l).astype(o_ref.dtype)


def attention(q, k, v, scale, sink=None, *, bq=512):
    b, h, t, d = q.shape
    hk, s, dv = k.shape[1], k.shape[2], v.shape[3]
    rep = h // hk
    bq = min(bq, t)
    assert t % bq == 0
    use_sink = sink is not None
    in_specs = [pl.BlockSpec((None, None, bq, d), lambda bi, hi, qi: (bi, hi, qi, 0)),
                pl.BlockSpec((None, None, s, d), lambda bi, hi, qi: (bi, hi // rep, 0, 0)),
                pl.BlockSpec((None, None, s, dv), lambda bi, hi, qi: (bi, hi // rep, 0, 0))]
    args = [q, k, v]
    if use_sink:
        in_specs = [pl.BlockSpec(memory_space=pltpu.SMEM)] + in_specs
        args = [sink.astype(F32)] + args
    vmem = 4 * _nbytes((bq, s), F32) + 2 * (_nbytes((bq, d), BF16) + _nbytes((s, d + dv), BF16)) + 4 * _nbytes((bq, dv), F32)
    return pl.pallas_call(
        functools.partial(_attn_kernel, scale=scale, use_sink=use_sink), name="attention",
        grid=(b, h, t // bq),
        in_specs=in_specs,
        out_specs=pl.BlockSpec((None, None, bq, dv), lambda bi, hi, qi: (bi, hi, qi, 0)),
        out_shape=jax.ShapeDtypeStruct((b, h, t, dv), BF16),
        compiler_params=_params(("arbitrary",) * 3, vmem),
    )(*args)


def _diff_attn_kernel(lam_ref, q_ref, k_ref, v_ref, g_ref, o_ref, *, scale, post_scale):
    def probs(i):
        s = _qk(q_ref[i], k_ref[i]) * scale
        p = jnp.exp(s - jnp.max(s, axis=-1, keepdims=True))
        return p / jnp.sum(p, axis=-1, keepdims=True)
    a = probs(0) - lam_ref[pl.program_id(1)] * probs(1)
    o = jnp.dot(a.astype(BF16), v_ref[...], preferred_element_type=F32)
    o = o * lax.rsqrt(jnp.mean(o * o, axis=-1, keepdims=True) + EPS) * g_ref[...]
    o_ref[...] = (o * post_scale).astype(o_ref.dtype)


def diff_attention(q, k, v, lam, g_out, lambda_init, *, bq=512):
    b, h, _, t, d = q.shape
    s, dv = k.shape[3], v.shape[3]
    bq = min(bq, t)
    assert t % bq == 0
    vmem = 6 * _nbytes((bq, s), F32) + 4 * (_nbytes((bq, d), BF16) + _nbytes((s, d), BF16)) + 2 * _nbytes((s, dv), BF16) + 4 * _nbytes((bq, dv), F32)
    return pl.pallas_call(
        functools.partial(_diff_attn_kernel, scale=DIFF_HD ** -0.5, post_scale=1.0 - lambda_init), name="diff_attention",
        grid=(b, h, t // bq),
        in_specs=[pl.BlockSpec(memory_space=pltpu.SMEM),
                  pl.BlockSpec((None, None, 2, bq, d), lambda bi, hi, qi: (bi, hi, 0, qi, 0)),
                  pl.BlockSpec((None, None, 2, s, d), lambda bi, hi, qi: (bi, hi, 0, 0, 0)),
                  pl.BlockSpec((None, None, s, dv), lambda bi, hi, qi: (bi, hi, 0, 0)),
                  pl.BlockSpec((1, dv), lambda bi, hi, qi: (0, 0))],
        out_specs=pl.BlockSpec((None, None, bq, dv), lambda bi, hi, qi: (bi, hi, qi, 0)),
        out_shape=jax.ShapeDtypeStruct((b, h, t, dv), BF16),
        compiler_params=_params(("arbitrary",) * 3, vmem),
    )(lam.astype(F32), q, k, v, g_out.astype(F32).reshape(1, dv))


def _swa_lat_kernel(sink_ref, q_ref, kc_ref, vc_ref, k0_ref, k1_ref, k2_ref, v0_ref, v1_ref, v2_ref, o_ref,
                    *, scale, nblocks):
    hk, n = pl.program_id(1), pl.program_id(2)
    q = q_ref[...].reshape(SWA_REP * BLOCK, SWA_HD)
    rows = SWA_REP * BLOCK
    s_ctx = _qk(q, kc_ref[...]) * scale
    kw = jnp.concatenate([k0_ref[...], k1_ref[...], k2_ref[...]], axis=0)
    vw = jnp.concatenate([v0_ref[...], v1_ref[...], v2_ref[...]], axis=0)
    s_loc = _qk(q, kw) * scale
    r = lax.broadcasted_iota(jnp.int32, (rows, 3 * BLOCK), 0) & (BLOCK - 1)
    c = lax.broadcasted_iota(jnp.int32, (rows, 3 * BLOCK), 1)
    kpos = (n - 1) * BLOCK + c
    mask = (c - r >= BLOCK - WINDOW) & (c - r <= BLOCK + WINDOW) & (kpos >= 0) & (kpos < nblocks * BLOCK)
    s_loc = jnp.where(mask, s_loc, NEG_INF)
    g = lax.broadcasted_iota(jnp.int32, (rows, 1), 0) >> int(math.log2(BLOCK))
    sink = jnp.zeros((rows, 1), F32)
    for gi in range(SWA_REP):
        sink = jnp.where(g == gi, sink_ref[hk * SWA_REP + gi], sink)
    m = jnp.maximum(jnp.maximum(jnp.max(s_ctx, axis=-1, keepdims=True), jnp.max(s_loc, axis=-1, keepdims=True)), sink)
    p_ctx = jnp.exp(s_ctx - m)
    p_loc = jnp.exp(s_loc - m)
    l = jnp.sum(p_ctx, axis=-1, keepdims=True) + jnp.sum(p_loc, axis=-1, keepdims=True) + jnp.exp(sink - m)
    o = (jnp.dot(p_ctx.astype(BF16), vc_ref[...], preferred_element_type=F32)
         + jnp.dot(p_loc.astype(BF16), vw, preferred_element_type=F32)) / l
    o_ref[...] = o.reshape(SWA_REP, BLOCK, SWA_HD).astype(o_ref.dtype)


def swa_latent_attention(q, k, v, k_ctx, v_ctx, sink):
    b, hk, rep, t, d = q.shape
    c = k_ctx.shape[2]
    nb = t // BLOCK
    pad = ((0, 0), (0, 0), (BLOCK, BLOCK), (0, 0))
    kp, vp = jnp.pad(k, pad), jnp.pad(v, pad)
    band = [pl.BlockSpec((None, None, BLOCK, d), functools.partial(lambda bi, hi, ni, o: (bi, hi, ni + o, 0), o=o))
            for o in range(3)]
    ctx = pl.BlockSpec((None, None, c, d), lambda bi, hi, ni: (bi, hi, 0, 0))
    qspec = pl.BlockSpec((None, None, rep, BLOCK, d), lambda bi, hi, ni: (bi, hi, 0, ni, 0))
    vmem = 8 * _nbytes((rep * BLOCK, c + 3 * BLOCK), F32) + 4 * _nbytes((c, d), BF16) + (4 << 20)
    return pl.pallas_call(
        functools.partial(_swa_lat_kernel, scale=SWA_HD ** -0.5, nblocks=nb), name="swa_latent_attention",
        grid=(b, hk, nb),
        in_specs=[pl.BlockSpec(memory_space=pltpu.SMEM), qspec, ctx, ctx] + band + band,
        out_specs=qspec,
        out_shape=jax.ShapeDtypeStruct(q.shape, BF16),
        compiler_params=_params(("arbitrary",) * 3, vmem),
    )(sink.astype(F32), q, k_ctx, v_ctx, kp, kp, kp, vp, vp, vp)


def _rms(x, g):
    xf = x.astype(F32)
    return xf * lax.rsqrt(jnp.mean(xf * xf, axis=-1, keepdims=True) + EPS) * g.astype(F32)


def _axial_rope(x):
    t, d = x.shape[1], x.shape[-1]
    rows = t // GRID_W
    r, col = jnp.meshgrid(jnp.arange(rows, dtype=F32), jnp.arange(GRID_W, dtype=F32), indexing='ij')
    n_freq = d // 4
    inv = ROPE_THETA ** (-jnp.arange(n_freq, dtype=F32) / n_freq)
    ang = jnp.concatenate([r.reshape(-1, 1) * inv, col.reshape(-1, 1) * inv], axis=-1)
    bshape = (1, t) + (1,) * (x.ndim - 3) + (d // 2,)
    cos, sin = jnp.cos(ang).reshape(bshape), jnp.sin(ang).reshape(bshape)
    x1, x2 = x[..., : d // 2], x[..., d // 2:]
    return jnp.concatenate([x1 * cos - x2 * sin, x2 * cos + x1 * sin], axis=-1)


def _rope_tail(x):
    return jnp.concatenate([x[..., :MLA_NOPE], _axial_rope(x[..., MLA_NOPE:])], axis=-1)


def _state_to_halves(st):
    b, nd, _, g, n = st.shape
    gh = S5_HALF // SSM_CH
    z = st.reshape(b, nd, 2, g // gh, gh * n)
    return jnp.transpose(z, (1, 3, 0, 2, 4)).reshape(nd, g // gh, b, 2 * gh * n)


def _halves_to_state(h, groups):
    nd, nh, b, _ = h.shape
    z = h.reshape(nd, nh, b, 2, -1)
    return jnp.transpose(z, (2, 0, 3, 1, 4)).reshape(b, nd, 2, groups, SSM_N)


def _heads_first(x):
    return jnp.swapaxes(x, 1, 2)


def _moe_ffn(logits, h_bf, P, j, bm):
    m = h_bf.shape[0]
    probs = jax.nn.softmax(logits, axis=-1)
    top_p, top_i = lax.top_k(probs, TOP_K)
    top_p = top_p / jnp.sum(top_p, axis=-1, keepdims=True)
    flat_e = top_i.reshape(-1).astype(jnp.int32)
    npairs = flat_e.shape[0]
    order = jnp.argsort(flat_e, stable=True).astype(jnp.int32)
    counts = jnp.sum(flat_e[:, None] == jnp.arange(N_EXPERTS, dtype=jnp.int32)[None, :], axis=0).astype(jnp.int32)
    blocks_per = (counts + bm - 1) // bm
    block_end = jnp.cumsum(blocks_per)
    pad_start = (block_end - blocks_per) * bm
    start = jnp.cumsum(counts) - counts
    sorted_e = flat_e[order]
    dest = pad_start[sorted_e] + jnp.arange(npairs, dtype=jnp.int32) - start[sorted_e]
    nrows = npairs + N_EXPERTS * bm
    row_token = jnp.zeros((nrows,), jnp.int32).at[dest].set(order // TOP_K)
    pos = jnp.zeros((npairs,), jnp.int32).at[order].set(dest).reshape(m, TOP_K)
    nblk = nrows // bm
    nused = block_end[-1:]
    block_expert = jnp.minimum(jnp.sum(jnp.arange(nblk, dtype=jnp.int32)[:, None] >= block_end[None, :], axis=1),
                               N_EXPERTS - 1).astype(jnp.int32)
    block_expert = jnp.where(jnp.arange(nblk) < nused[0], block_expert, block_expert[jnp.maximum(nused[0] - 1, 0)])
    x_sorted = jnp.take(h_bf, row_token, axis=0)
    act = moe_glu_up(x_sorted, P['w_moe_gate'], P['w_moe_up'], j, block_expert, nused, bm=bm)
    y_sorted = moe_down(act, P['w_moe_down'], j, block_expert, nused, bm=bm)
    out = jnp.zeros((m, y_sorted.shape[1]), F32)
    for kk in range(TOP_K):
        out = out + top_p[:, kk:kk + 1] * jnp.take(y_sorted, pos[:, kk], axis=0)
    return out


IN_SEGMENTS = ('q', 'ckv', 'ssm', 'diff', 'gate', 'swa', 'krope')
SWA_COLS = (SWA_HEADS + 2 * SWA_KV_HEADS) * SWA_HD
DIFF_COLS = 2 * DIFF_HEADS * 2 * DIFF_HD + DIFF_HEADS * DIFF_VD


def _arrange_w_in(w_in_l):
    d = w_in_l.shape[0]
    w = d // 4
    c_kr = 2 * w
    c_ssm = c_kr + MLA_ROPE
    c_swa = c_ssm + w
    c_diff = c_swa + SWA_COLS
    c_gate = c_diff + DIFF_COLS
    pad = (-w_in_l.shape[1]) % 512
    return jnp.concatenate([w_in_l[:, :c_kr], w_in_l[:, c_ssm:c_swa], w_in_l[:, c_diff:], w_in_l[:, c_swa:c_diff],
                            w_in_l[:, c_kr:c_ssm], jnp.zeros((d, pad), w_in_l.dtype)], axis=1).astype(BF16)


def kernel(x_prompt, x_sample, cache_mla_ckv, cache_mla_krope, state_ssm, cache_swa_k, cache_swa_v, cache_diff_k, cache_diff_v, c, c_ctx, w_ada, b_ada, g_norm1, g_norm2, w_in, g_mla_qlat, g_mla_kvlat, w_mla_uq, w_mla_ukv, g_mla_q, g_mla_k, ssm_a_re, ssm_a_im, ssm_log_dt, ssm_b_re, ssm_b_im, ssm_c_re, ssm_c_im, ssm_d, w_ssm_glu, g_swa_q, g_swa_k, swa_sink, g_diff_q, g_diff_k, diff_lambda, g_diff_out, w_branch, w_out, w_ffn_gate, w_ffn_up, w_ffn_down, w_router, w_moe_gate, w_moe_up, w_moe_down):
    P = dict(ssm_a_re=ssm_a_re, ssm_a_im=ssm_a_im, ssm_log_dt=ssm_log_dt, ssm_b_re=ssm_b_re, ssm_b_im=ssm_b_im,
             ssm_c_re=ssm_c_re, ssm_c_im=ssm_c_im, ssm_d=ssm_d, w_router=w_router, w_moe_gate=w_moe_gate,
             w_moe_up=w_moe_up, w_moe_down=w_moe_down)
    bp, tp, d = x_prompt.shape
    bs, ts, _ = x_sample.shape
    depth = w_in.shape[0]
    past = cache_mla_ckv.shape[2]
    mp, ms = bp * tp, bs * ts
    m = mp + ms
    w = d // 4
    groups = w // SSM_CH
    x = jnp.concatenate([x_prompt.reshape(mp, d), x_sample.reshape(ms, d)], axis=0)

    seg = math.gcd(mp, ts)
    nseg = m // seg
    seg_mod = np.concatenate([np.zeros(mp // seg, np.int32), 1 + np.repeat(np.arange(bs, dtype=np.int32), ts // seg)])
    cvec = jnp.concatenate([c_ctx[None, :], c, jnp.zeros((8 - 1 - bs, d), F32)], axis=0)
    cvec = cvec * jax.nn.sigmoid(cvec)

    def per_seg(v):
        return v[seg_mod][:, None, :]

    def gated_add(xx, gt_seg, upd):
        return (xx.reshape(nseg, seg, d) + gt_seg * upd.reshape(nseg, seg, d)).reshape(m, d)

    outs = [[] for _ in range(7)]
    for l in range(depth):
        lambda_init = 0.8 - 0.6 * math.exp(-0.3 * l)
        mods = matmul(cvec, w_ada, (l,), bm=8, bn=1024) + b_ada[l][None, :]
        sh1, sc1, gt1, sh2, sc2, gt2 = (per_seg(v) for v in jnp.split(mods, 6, axis=-1))

        h = modulated_norm(x, g_norm1[l], sc1, sh1)
        p = matmul(h, _arrange_w_in(w_in[l]), out_dtype=F32)
        o0 = 0
        seg_cols, col0 = {}, {}
        for name, width in zip(IN_SEGMENTS, (w, w, w, DIFF_COLS, N_BRANCH * d, SWA_COLS, MLA_ROPE)):
            col0[name] = o0
            if name not in ('ssm', 'gate'):
                seg_cols[name] = p[:, o0:o0 + width]
            o0 += width

        c_kv = _rms(seg_cols['ckv'], g_mla_kvlat[l])
        k_rope = seg_cols['krope']
        q_lat = _rms(seg_cols['q'], g_mla_qlat[l]).astype(BF16)
        q_m = _rms(matmul(q_lat, w_mla_uq, (l,)).reshape(m, MLA_HEADS, MLA_QK), g_mla_q[l])
        ckv_all = jnp.concatenate([c_kv, cache_mla_ckv[:, l].reshape(bs * past, w)], axis=0).astype(BF16)
        krope_all = jnp.concatenate([k_rope, cache_mla_krope[:, l].reshape(bs * past, MLA_ROPE)], axis=0)
        kv = matmul(ckv_all, w_mla_ukv, (l,), bm=512).reshape(-1, MLA_HEADS, MLA_NOPE + MLA_V)
        k_r = jnp.broadcast_to(krope_all[:, None, :], (kv.shape[0], MLA_HEADS, MLA_ROPE))
        k_m = _rms(jnp.concatenate([kv[..., :MLA_NOPE], k_r], axis=-1), g_mla_k[l])
        v_m = kv[..., MLA_NOPE:]
        qp = _heads_first(q_m[:mp].reshape(bp, tp, MLA_HEADS, MLA_QK)).astype(BF16)
        kp_ = _heads_first(k_m[:mp].reshape(bp, tp, MLA_HEADS, MLA_QK)).astype(BF16)
        vp_ = _heads_first(v_m[:mp].reshape(bp, tp, MLA_HEADS, MLA_V)).astype(BF16)
        o_mla_p = attention(qp, kp_, vp_, MLA_QK ** -0.5)
        qs = _heads_first(_rope_tail(q_m[mp:].reshape(bs, ts, MLA_HEADS, MLA_QK))).astype(BF16)
        k_lat = _rope_tail(k_m[mp:m].reshape(bs, ts, MLA_HEADS, MLA_QK))
        k_cat = jnp.concatenate([k_m[m:].reshape(bs, past, MLA_HEADS, MLA_QK), k_lat], axis=1)
        v_cat = jnp.concatenate([v_m[m:].reshape(bs, past, MLA_HEADS, MLA_V), v_m[mp:m].reshape(bs, ts, MLA_HEADS, MLA_V)], axis=1)
        o_mla_s = attention(qs, _heads_first(k_cat).astype(BF16), _heads_first(v_cat).astype(BF16), MLA_QK ** -0.5)
        o_mla = jnp.concatenate([_heads_first(o_mla_p).reshape(mp, w), _heads_first(o_mla_s).reshape(ms, w)], axis=0)

        tabs = _s5_tables(ssm_a_re[l], ssm_a_im[l], ssm_log_dt[l], ssm_b_re[l], ssm_b_im[l], ssm_c_re[l], ssm_c_im[l])
        ncol = p.shape[1]
        h0_p = jnp.zeros((2, w // S5_HALF, bp, 2 * (S5_HALF // SSM_CH) * SSM_N), F32)
        y_p, hT_p = s5_scan(p.reshape(m // tp, tp, ncol), col0['ssm'], 0, bp, tabs, h0_p)
        y_s, _ = s5_scan(p.reshape(m // ts, ts, ncol), col0['ssm'], mp // ts, bs, tabs, _state_to_halves(state_ssm[:, l]))
        y2 = jnp.concatenate([y_p.reshape(2, mp, w), y_s.reshape(2, ms, w)], axis=1)
        o_ssm = ssm_post(y2, p, col0['ssm'], ssm_d[l], w_ssm_glu, l)
        ssm_state = _halves_to_state(hT_p, groups)

        ps = seg_cols['swa']
        nq = SWA_HEADS * SWA_HD
        nk = SWA_KV_HEADS * SWA_HD
        q_s = _rms(ps[:, :nq].reshape(m, SWA_HEADS, SWA_HD), g_swa_q[l])
        k_s = _rms(ps[:, nq:nq + nk].reshape(m, SWA_KV_HEADS, SWA_HD), g_swa_k[l])
        v_s = ps[:, nq + nk:].reshape(m, SWA_KV_HEADS, SWA_HD)
        o_swa_p = attention(_heads_first(q_s[:mp].reshape(bp, tp, SWA_HEADS, SWA_HD)).astype(BF16),
                            _heads_first(k_s[:mp].reshape(bp, tp, SWA_KV_HEADS, SWA_HD)).astype(BF16),
                            _heads_first(v_s[:mp].reshape(bp, tp, SWA_KV_HEADS, SWA_HD)).astype(BF16),
                            SWA_HD ** -0.5, swa_sink[l])
        q_sl = _axial_rope(q_s[mp:].reshape(bs, ts, SWA_KV_HEADS, SWA_REP, SWA_HD))
        k_sl = _axial_rope(k_s[mp:].reshape(bs, ts, SWA_KV_HEADS, SWA_HD))
        o_swa_s = swa_latent_attention(
            jnp.transpose(q_sl, (0, 2, 3, 1, 4)).astype(BF16), _heads_first(k_sl).astype(BF16),
            _heads_first(v_s[mp:].reshape(bs, ts, SWA_KV_HEADS, SWA_HD)).astype(BF16),
            _heads_first(cache_swa_k[:, l]).astype(BF16), _heads_first(cache_swa_v[:, l]).astype(BF16), swa_sink[l])
        o_swa = jnp.concatenate([_heads_first(o_swa_p).reshape(mp, w),
                                 jnp.transpose(o_swa_s, (0, 3, 1, 2, 4)).reshape(ms, w)], axis=0)

        pd = seg_cols['diff']
        ndq = DIFF_HEADS * 2 * DIFF_HD
        q_d = _rms(pd[:, :ndq].reshape(m, DIFF_HEADS, 2, DIFF_HD), g_diff_q[l])
        k_d = _rms(pd[:, ndq:2 * ndq].reshape(m, DIFF_HEADS, 2, DIFF_HD), g_diff_k[l])
        v_d = pd[:, 2 * ndq:].reshape(m, DIFF_HEADS, DIFF_VD)
        lp = diff_lambda[l]
        lam = jnp.exp(jnp.sum(lp[0] * lp[1], -1)) - jnp.exp(jnp.sum(lp[2] * lp[3], -1)) + lambda_init

        def maps_first(z, bb, tt):
            return jnp.transpose(z.reshape(bb, tt, DIFF_HEADS, 2, DIFF_HD), (0, 2, 3, 1, 4))

        o_diff_p = diff_attention(maps_first(q_d[:mp], bp, tp).astype(BF16), maps_first(k_d[:mp], bp, tp).astype(BF16),
                                  _heads_first(v_d[:mp].reshape(bp, tp, DIFF_HEADS, DIFF_VD)).astype(BF16),
                                  lam, g_diff_out[l], lambda_init)
        q_dl = _axial_rope(q_d[mp:].reshape(bs, ts, DIFF_HEADS, 2, DIFF_HD))
        k_dl = jnp.concatenate([cache_diff_k[:, l], _axial_rope(k_d[mp:].reshape(bs, ts, DIFF_HEADS, 2, DIFF_HD))], axis=1)
        v_dl = jnp.concatenate([cache_diff_v[:, l], v_d[mp:].reshape(bs, ts, DIFF_HEADS, DIFF_VD)], axis=1)
        o_diff_s = diff_attention(jnp.transpose(q_dl, (0, 2, 3, 1, 4)).astype(BF16),
                                  jnp.transpose(k_dl, (0, 2, 3, 1, 4)).astype(BF16),
                                  _heads_first(v_dl).astype(BF16), lam, g_diff_out[l], lambda_init)
        o_diff = jnp.concatenate([_heads_first(o_diff_p).reshape(mp, w), _heads_first(o_diff_s).reshape(ms, w)], axis=0)

        merged = merge_branches((o_mla, o_ssm, o_swa, o_diff), p, col0['gate'], w_branch, l)
        x = matmul_residual(merged, w_out, (l,), x, gt1)

        j = l // 2
        if l % 2 == 0:
            h2 = modulated_norm(x, g_norm2[l], sc2, sh2)
            act = glu_up(h2, w_ffn_gate, w_ffn_up, (j,))
            x = matmul_residual(act, w_ffn_down, (j,), x, gt2)
        else:
            h2, logits = modulated_norm(x, g_norm2[l], sc2, sh2, w_router, j)
            x = gated_add(x, gt2, _moe_ffn(logits, h2, P, j, bm=256))

        outs[0].append(c_kv[:mp].reshape(bp, tp, w))
        outs[1].append(k_rope[:mp].reshape(bp, tp, MLA_ROPE))
        outs[2].append(ssm_state)
        outs[3].append(k_s[:mp].reshape(bp, tp, SWA_KV_HEADS, SWA_HD))
        outs[4].append(v_s[:mp].reshape(bp, tp, SWA_KV_HEADS, SWA_HD))
        outs[5].append(k_d[:mp].reshape(bp, tp, DIFF_HEADS, 2, DIFF_HD))
        outs[6].append(v_d[:mp].reshape(bp, tp, DIFF_HEADS, DIFF_VD))

    return (x[:mp].reshape(bp, tp, d), x[mp:].reshape(bs, ts, d)) + tuple(jnp.stack(o, axis=1) for o in outs)
```

```python
import functools
import math

import numpy as np
import jax
import jax.numpy as jnp
from jax import lax
from jax.experimental import pallas as pl
from jax.experimental.pallas import tpu as pltpu

F32 = jnp.float32
BF16 = jnp.bfloat16
EPS = 1e-6
NEG_INF = -1e30
ROPE_THETA = 10000.0
GRID_W = 64
BLOCK = 128
N_BRANCH = 4
MLA_HEADS = 4
MLA_NOPE = 128
MLA_ROPE = 64
MLA_QK = MLA_NOPE + MLA_ROPE
MLA_V = 128
SSM_CH = 16
SSM_N = 64
SWA_HEADS = 8
SWA_KV_HEADS = 2
SWA_REP = SWA_HEADS // SWA_KV_HEADS
SWA_HD = 64
WINDOW = 128
DIFF_HEADS = 4
DIFF_HD = 64
DIFF_VD = 2 * DIFF_HD
N_EXPERTS = 8
TOP_K = 2

V7X_VMEM_REQUEST_CAP = 56 * 1024 * 1024


def _params(semantics, vmem_bytes):
    limit = int(min(max(vmem_bytes * 5 // 4 + (2 << 20), 16 << 20), V7X_VMEM_REQUEST_CAP))
    return pltpu.CompilerParams(dimension_semantics=semantics, vmem_limit_bytes=limit)


def _tile(n, want):
    t = min(n, want)
    while n % t:
        t -= 1
    return t


def _nbytes(shape, dtype):
    return int(np.prod(shape)) * jnp.dtype(dtype).itemsize


def _mm_kernel(x_ref, w_ref, o_ref, *scratch):
    if scratch:
        wbf_ref, = scratch

        @pl.when(pl.program_id(1) == 0)
        def _():
            wbf_ref[...] = w_ref[...].astype(BF16)
        w = wbf_ref[...]
    else:
        w = w_ref[...]
    o_ref[...] = jnp.dot(x_ref[...].astype(BF16), w, preferred_element_type=F32).astype(o_ref.dtype)


def matmul(x, w, lead=(), *, bm=1024, bn=512, out_dtype=F32):
    m, k = x.shape
    n = w.shape[-1]
    assert w.shape[-2] == k
    bm, bn = _tile(m, bm), _tile(n, bn)
    assert m % bm == 0 and n % bn == 0, (m, n, bm, bn)
    nl = len(lead)
    cast = w.dtype != BF16
    scratch = [pltpu.VMEM((k, bn), BF16)] if cast else []
    vmem = (2 * _nbytes((bm, k), x.dtype) + 2 * _nbytes((k, bn), w.dtype)
            + 2 * _nbytes((bm, bn), out_dtype) + (_nbytes((k, bn), BF16) if cast else 0))
    return pl.pallas_call(
        _mm_kernel, name="matmul",
        grid=(n // bn, m // bm),
        in_specs=[pl.BlockSpec((bm, k), lambda j, i: (i, 0)),
                  pl.BlockSpec((None,) * nl + (k, bn), lambda j, i: lead + (0, j))],
        out_specs=pl.BlockSpec((bm, bn), lambda j, i: (i, j)),
        out_shape=jax.ShapeDtypeStruct((m, n), out_dtype),
        scratch_shapes=scratch,
        compiler_params=_params(("arbitrary", "arbitrary"), vmem),
    )(x, w)


IN_TILE = 512
IN_SHIFT_UNIT = 128


def _inproj_kernel(a_idx, b_idx, shift_ref, x_ref, wa_ref, wb_ref, o_ref, wbf_ref, *, shift):
    j = pl.program_id(0)
    first = pl.program_id(1) == 0

    @pl.when(first & (shift_ref[j] == 0))
    def _():
        wbf_ref[...] = wa_ref[...].astype(BF16)

    @pl.when(first & (shift_ref[j] != 0))
    def _():
        wcat = jnp.concatenate([wa_ref[...], wb_ref[...]], axis=1)
        wbf_ref[...] = wcat[:, shift:shift + IN_TILE].astype(BF16)

    o_ref[...] = jnp.dot(x_ref[...], wbf_ref[...], preferred_element_type=F32)


def in_projection(h, w_in, layer, *, bm=1024):
    m, k = h.shape
    ntot = w_in.shape[-1]
    w = k // 4
    lead = 2 * w
    rest = ntot - lead - MLA_ROPE
    assert lead % IN_TILE == 0 and MLA_ROPE < IN_SHIFT_UNIT and IN_TILE % IN_SHIFT_UNIT == 0
    n_lead, n_rest = lead // IN_TILE, -(-rest // IN_TILE)
    last_unit = (ntot - 1) // IN_SHIFT_UNIT
    a_idx = list(range(n_lead)) + [n_lead + t for t in range(n_rest)] + [n_lead]
    b_idx = [0] * n_lead + [min((lead + IN_TILE * (t + 1)) // IN_SHIFT_UNIT, last_unit) for t in range(n_rest)] + [0]
    shifted = [0] * n_lead + [1] * n_rest + [0]
    ntiles = len(a_idx)
    col0 = dict(q=0, ckv=w, ssm=lead, swa=lead + w, diff=lead + w + SWA_COLS, gate=lead + w + SWA_COLS + DIFF_COLS,
                krope=(n_lead + n_rest) * IN_TILE)
    bm = _tile(m, bm)
    vmem = (2 * _nbytes((bm, k), BF16) + 2 * _nbytes((k, IN_TILE + IN_SHIFT_UNIT), F32) + _nbytes((k, IN_TILE), BF16)
            + 3 * _nbytes((k, IN_TILE + IN_SHIFT_UNIT), F32) + 2 * _nbytes((bm, IN_TILE), F32))
    p = pl.pallas_call(
        functools.partial(_inproj_kernel, shift=MLA_ROPE), name="in_projection",
        grid_spec=pltpu.PrefetchScalarGridSpec(
            num_scalar_prefetch=3,
            grid=(ntiles, m // bm),
            in_specs=[pl.BlockSpec((bm, k), lambda j, i, a, b, s: (i, 0)),
                      pl.BlockSpec((None, k, IN_TILE), lambda j, i, a, b, s: (layer, 0, a[j])),
                      pl.BlockSpec((None, k, IN_SHIFT_UNIT), lambda j, i, a, b, s: (layer, 0, b[j]))],
            out_specs=pl.BlockSpec((bm, IN_TILE), lambda j, i, a, b, s: (i, j)),
            scratch_shapes=[pltpu.VMEM((k, IN_TILE), BF16)]),
        out_shape=jax.ShapeDtypeStruct((m, ntiles * IN_TILE), F32),
        compiler_params=_params(("arbitrary", "arbitrary"), vmem),
    )(jnp.asarray(a_idx, jnp.int32), jnp.asarray(b_idx, jnp.int32), jnp.asarray(shifted, jnp.int32), h, w_in, w_in)
    return p, col0


def _glu_kernel(x_ref, wg_ref, wu_ref, o_ref, wg_bf, wu_bf):
    @pl.when(pl.program_id(1) == 0)
    def _():
        wg_bf[...] = wg_ref[...].astype(BF16)
        wu_bf[...] = wu_ref[...].astype(BF16)
    x = x_ref[...]
    a = jnp.dot(x, wg_bf[...], preferred_element_type=F32)
    b = jnp.dot(x, wu_bf[...], preferred_element_type=F32)
    o_ref[...] = (a * jax.nn.sigmoid(a) * b).astype(o_ref.dtype)


def glu_up(x, wg, wu, lead, *, bm=1024, bn=512):
    m, k = x.shape
    n = wg.shape[-1]
    bm, bn = _tile(m, bm), _tile(n, bn)
    assert m % bm == 0 and n % bn == 0
    nl = len(lead)
    wspec = pl.BlockSpec((None,) * nl + (k, bn), lambda j, i: lead + (0, j))
    vmem = (2 * _nbytes((bm, k), x.dtype) + 4 * _nbytes((k, bn), F32) + 2 * _nbytes((k, bn), BF16)
            + 2 * _nbytes((bm, bn), BF16) + 3 * _nbytes((bm, bn), F32))
    return pl.pallas_call(
        _glu_kernel, name="glu_up",
        grid=(n // bn, m // bm),
        in_specs=[pl.BlockSpec((bm, k), lambda j, i: (i, 0)), wspec, wspec],
        out_specs=pl.BlockSpec((bm, bn), lambda j, i: (i, j)),
        out_shape=jax.ShapeDtypeStruct((m, n), BF16),
        scratch_shapes=[pltpu.VMEM((k, bn), BF16), pltpu.VMEM((k, bn), BF16)],
        compiler_params=_params(("arbitrary", "arbitrary"), vmem),
    )(x, wg, wu)


def _mm_res_kernel(a_ref, w_ref, x_ref, gt_ref, o_ref, wbf_ref):
    @pl.when(pl.program_id(1) == 0)
    def _():
        wbf_ref[...] = w_ref[...].astype(BF16)
    upd = jnp.dot(a_ref[...], wbf_ref[...], preferred_element_type=F32)
    o_ref[...] = x_ref[...] + gt_ref[...] * upd


def matmul_residual(a, w, lead, x, gt_seg, *, bm=512, bn=512):
    m, k = a.shape
    n = w.shape[-1]
    seg = m // gt_seg.shape[0]
    bm, bn = _tile(seg, bm), _tile(n, bn)
    nl = len(lead)
    vmem = (2 * _nbytes((bm, k), a.dtype) + 2 * _nbytes((k, bn), F32) + _nbytes((k, bn), BF16)
            + 5 * _nbytes((bm, bn), F32))
    return pl.pallas_call(
        _mm_res_kernel, name="matmul_residual",
        grid=(n // bn, m // bm),
        in_specs=[pl.BlockSpec((bm, k), lambda j, i: (i, 0)),
                  pl.BlockSpec((None,) * nl + (k, bn), lambda j, i: lead + (0, j)),
                  pl.BlockSpec((bm, bn), lambda j, i: (i, j)),
                  pl.BlockSpec((None, 1, bn), lambda j, i: (i * bm // seg, 0, j))],
        out_specs=pl.BlockSpec((bm, bn), lambda j, i: (i, j)),
        out_shape=jax.ShapeDtypeStruct((m, n), F32),
        scratch_shapes=[pltpu.VMEM((k, bn), BF16)],
        compiler_params=_params(("arbitrary", "arbitrary"), vmem),
    )(a, w, x, gt_seg)


def _merge_kernel(*refs):
    nb = N_BRANCH
    o_refs, g_refs = refs[:nb], refs[nb:2 * nb]
    wb_ref, out_ref, wbf_ref = refs[2 * nb:]

    @pl.when(pl.program_id(1) == 0)
    def _():
        wbf_ref[...] = wb_ref[...].astype(BF16)
    acc = None
    for n_ in range(nb):
        up = jnp.dot(o_refs[n_][...], wbf_ref[n_], preferred_element_type=F32)
        t = jax.nn.sigmoid(g_refs[n_][...]) * up
        acc = t if acc is None else acc + t
    out_ref[...] = acc.astype(out_ref.dtype)


def merge_branches(branch_outs, p, gate_col0, w_branch, layer, *, bm=512, bn=512):
    m, bw = branch_outs[0].shape
    dm = w_branch.shape[-1]
    bm, bn = _tile(m, bm), _tile(dm, bn)
    assert gate_col0 % bn == 0
    o_spec = pl.BlockSpec((bm, bw), lambda j, i: (i, 0))
    g_specs = [pl.BlockSpec((bm, bn), functools.partial(lambda j, i, c: (i, c + j), c=(gate_col0 + n_ * dm) // bn))
               for n_ in range(N_BRANCH)]
    vmem = (N_BRANCH * (2 * _nbytes((bm, bw), BF16) + 2 * _nbytes((bm, bn), F32) + 2 * _nbytes((bw, bn), F32)
                        + _nbytes((bw, bn), BF16)) + 6 * _nbytes((bm, bn), F32))
    return pl.pallas_call(
        _merge_kernel, name="merge_branches",
        grid=(dm // bn, m // bm),
        in_specs=[o_spec] * N_BRANCH + g_specs
                 + [pl.BlockSpec((None, N_BRANCH, bw, bn), lambda j, i: (layer, 0, 0, j))],
        out_specs=pl.BlockSpec((bm, bn), lambda j, i: (i, j)),
        out_shape=jax.ShapeDtypeStruct((m, dm), BF16),
        scratch_shapes=[pltpu.VMEM((N_BRANCH, bw, bn), BF16)],
        compiler_params=_params(("arbitrary", "arbitrary"), vmem),
    )(*branch_outs, *([p] * N_BRANCH), w_branch)


def _modnorm_kernel(*refs, with_router):
    if with_router:
        x_ref, g_ref, sc_ref, sh_ref, wr_ref, h_ref, logit_ref = refs
    else:
        x_ref, g_ref, sc_ref, sh_ref, h_ref = refs
    x = x_ref[...]
    y = x * lax.rsqrt(jnp.mean(x * x, axis=-1, keepdims=True) + EPS) * g_ref[...]
    h = y * (1 + sc_ref[...]) + sh_ref[...]
    h_ref[...] = h.astype(h_ref.dtype)
    if with_router:
        logit_ref[...] = jnp.dot(h, wr_ref[...], precision=lax.Precision.HIGHEST, preferred_element_type=F32)


def modulated_norm(x, g, sc_seg, sh_seg, w_router=None, router_layer=0, *, bm=256):
    m, d = x.shape
    seg = m // sc_seg.shape[0]
    bm = _tile(seg, bm)
    mod_spec = pl.BlockSpec((None, 1, d), lambda i: (i * bm // seg, 0, 0))
    in_specs = [pl.BlockSpec((bm, d), lambda i: (i, 0)), pl.BlockSpec((1, d), lambda i: (0, 0)), mod_spec, mod_spec]
    out_specs = [pl.BlockSpec((bm, d), lambda i: (i, 0))]
    out_shape = [jax.ShapeDtypeStruct((m, d), BF16)]
    args = [x, g.reshape(1, d), sc_seg, sh_seg]
    if w_router is not None:
        ne = w_router.shape[-1]
        in_specs.append(pl.BlockSpec((None, d, ne), lambda i: (router_layer, 0, 0)))
        out_specs.append(pl.BlockSpec((bm, ne), lambda i: (i, 0)))
        out_shape.append(jax.ShapeDtypeStruct((m, ne), F32))
        args.append(w_router)
    vmem = 8 * _nbytes((bm, d), F32) + 2 * _nbytes((d, 128), F32)
    out = pl.pallas_call(
        functools.partial(_modnorm_kernel, with_router=w_router is not None), name="modulated_norm",
        grid=(m // bm,),
        in_specs=in_specs, out_specs=out_specs, out_shape=out_shape,
        compiler_params=_params(("arbitrary",), vmem),
    )(*args)
    return out if w_router is not None else out[0]


S5_CHUNK = 64
S5_HALF = 256


def _s5_tables(a_re, a_im, log_dt, b_re, b_im, c_re, c_im):
    ndir, g, n = a_re.shape
    ch = b_re.shape[-1]
    gh = S5_HALF // ch
    nh = g // gh
    dt = jnp.exp(log_dt)[..., None]
    er, ei = a_re * dt, a_im * dt
    mag = jnp.exp(er)
    lb_re, lb_im = mag * jnp.cos(ei), mag * jnp.sin(ei)
    den = a_re * a_re + a_im * a_im
    xr, xi = lb_re - 1.0, lb_im
    co_re, co_im = (xr * a_re + xi * a_im) / den, (xi * a_re - xr * a_im) / den
    bb_re = co_re[..., None] * b_re - co_im[..., None] * b_im
    bb_im = co_re[..., None] * b_im + co_im[..., None] * b_re
    eye = jnp.eye(gh, dtype=F32)

    def blockdiag_in(z):
        z = z.reshape(ndir, nh, gh, n, ch)
        return jnp.einsum('dhgnj,gk->dhgjkn', z, eye).reshape(ndir, nh, gh * ch, gh * n)

    def blockdiag_out(z):
        z = z.reshape(ndir, nh, gh, ch, n)
        return jnp.einsum('dhgin,gk->dhkngi', z, eye).reshape(ndir, nh, gh * n, gh * ch)

    bmat = jnp.concatenate([blockdiag_in(bb_re), blockdiag_in(bb_im)], axis=-1).astype(BF16)
    cmat = jnp.concatenate([blockdiag_out(c_re), -blockdiag_out(c_im)], axis=-2).astype(BF16)

    def halves(z):
        return z.reshape(z.shape[:-2] + (nh, gh * n))

    steps = jnp.arange(S5_CHUNK, dtype=F32)
    expo = jnp.stack([steps, S5_CHUNK - 1 - steps])[:, :, None, None]

    def powers(sign):
        pr, pi = sign * expo * er[:, None], sign * expo * ei[:, None]
        pm = jnp.exp(pr)
        tab = jnp.concatenate([halves(pm * jnp.cos(pi)), halves(pm * jnp.sin(pi))], axis=-1)
        return jnp.swapaxes(tab, 1, 2)

    lam = jnp.concatenate([halves(lb_re), halves(lb_im)], axis=-1)[:, :, None, :]
    ti = np.arange(S5_CHUNK)
    tri = jnp.asarray(np.stack([ti[:, None] >= ti[None, :], ti[:, None] <= ti[None, :]]), BF16)
    return dict(bmat=bmat, cmat=cmat, p=powers(1.0), pinv=powers(-1.0), lam=lam, tri=tri)


def _s5_kernel(u_ref, bm_ref, cm_ref, p_ref, pinv_ref, lam_ref, tri_ref, h0_ref, y_ref, hT_ref, hs_ref, st_ref,
               *, nb, nchunks):
    d, c = pl.program_id(0), pl.program_id(3)
    L = S5_CHUNK
    half = p_ref.shape[-1] // 2

    @pl.when(c == 0)
    def _():
        st_ref[...] = h0_ref[...]

    u = u_ref[...].reshape(nb * L, u_ref.shape[-1]).astype(BF16)
    bu = jnp.dot(u, bm_ref[...], preferred_element_type=F32)
    p_re, p_im = p_ref[:, :half], p_ref[:, half:]
    q_re, q_im = pinv_ref[:, :half], pinv_ref[:, half:]
    lam_re, lam_im = lam_ref[:, :half], lam_ref[:, half:]
    tri = tri_ref[...]
    for i in range(nb):
        b_re, b_im = bu[i * L:(i + 1) * L, :half], bu[i * L:(i + 1) * L, half:]
        z = jnp.concatenate([b_re * q_re - b_im * q_im, b_re * q_im + b_im * q_re], axis=1)
        z_hi = z.astype(BF16)
        z_lo = (z - z_hi.astype(F32)).astype(BF16)
        w = jnp.dot(tri, z_hi, preferred_element_type=F32) + jnp.dot(tri, z_lo, preferred_element_type=F32)
        s_re, s_im = st_ref[i:i + 1, :half], st_ref[i:i + 1, half:]
        w_re = w[:, :half] + (lam_re * s_re - lam_im * s_im)
        w_im = w[:, half:] + (lam_re * s_im + lam_im * s_re)
        h = jnp.concatenate([p_re * w_re - p_im * w_im, p_re * w_im + p_im * w_re], axis=1)
        hs_ref[i * L:(i + 1) * L, :] = h.astype(BF16)
        st_ref[i:i + 1, :] = jnp.where(d == 0, h[L - 1:L, :], h[0:1, :])
    y = jnp.dot(hs_ref[...], cm_ref[...], preferred_element_type=F32)
    y_ref[...] = y.reshape(y_ref.shape)

    @pl.when(c == nchunks - 1)
    def _():
        hT_ref[...] = st_ref[...]


def s5_scan(p3, col0, seq0, n_seq, tabs, h0):
    _, t, _ = p3.shape
    ndir, nh, hc, ns2 = tabs['bmat'].shape
    L = S5_CHUNK
    nb = _tile(n_seq, 8)
    assert t % L == 0 and seq0 % nb == 0 and col0 % hc == 0
    nchunks = t // L
    sg0, cb0 = seq0 // nb, col0 // hc

    def tchunk(d, c):
        return c + d * (nchunks - 1 - 2 * c)

    tab = lambda r, cc: pl.BlockSpec((None, None, r, cc), lambda d, hf, sg, c: (d, hf, 0, 0))
    vmem = (4 * _nbytes((nb * L, hc), F32) + 4 * _nbytes((hc, ns2), BF16) + 4 * _nbytes((L, ns2), F32)
            + 3 * _nbytes((nb * L, ns2), F32) + 8 * _nbytes((L, ns2), F32))
    return pl.pallas_call(
        functools.partial(_s5_kernel, nb=nb, nchunks=nchunks), name="s5_scan",
        grid=(ndir, nh, n_seq // nb, nchunks),
        in_specs=[pl.BlockSpec((nb, L, hc), lambda d, hf, sg, c: (sg0 + sg, tchunk(d, c), cb0 + hf)),
                  tab(hc, ns2), tab(ns2, hc), tab(L, ns2), tab(L, ns2), tab(1, ns2),
                  pl.BlockSpec((None, L, L), lambda d, hf, sg, c: (d, 0, 0)),
                  pl.BlockSpec((None, None, nb, ns2), lambda d, hf, sg, c: (d, hf, sg, 0))],
        out_specs=[pl.BlockSpec((None, nb, L, hc), lambda d, hf, sg, c: (d, sg, tchunk(d, c), hf)),
                   pl.BlockSpec((None, None, nb, ns2), lambda d, hf, sg, c: (d, hf, sg, 0))],
        out_shape=[jax.ShapeDtypeStruct((ndir, n_seq, t, nh * hc), F32),
                   jax.ShapeDtypeStruct((ndir, nh, n_seq, ns2), F32)],
        scratch_shapes=[pltpu.VMEM((nb * L, ns2), BF16), pltpu.VMEM((nb, ns2), F32)],
        compiler_params=_params(("arbitrary",) * 4, vmem),
    )(p3, tabs['bmat'], tabs['cmat'], tabs['p'], tabs['pinv'], tabs['lam'], tabs['tri'], h0)


def _ssm_post_kernel(y_ref, u_ref, d_ref, w_ref, o_ref, wbf_ref):
    @pl.when(pl.program_id(0) == 0)
    def _():
        wbf_ref[...] = w_ref[...].astype(BF16)
    y = y_ref[0] + y_ref[1] + d_ref[...] * u_ref[...]
    g = jax.nn.gelu(y)
    z = jnp.dot(g.astype(BF16), wbf_ref[...], preferred_element_type=F32)
    o_ref[...] = (g * jax.nn.sigmoid(z)).astype(o_ref.dtype)


def ssm_post(y2, p, col0, d_vec, w_glu, layer, *, bm=1024):
    _, m, w = y2.shape
    bm = _tile(m, bm)
    assert col0 % w == 0
    vmem = 6 * _nbytes((bm, w), F32) * 2 + 3 * _nbytes((w, w), F32)
    return pl.pallas_call(
        _ssm_post_kernel, name="ssm_post",
        grid=(m // bm,),
        in_specs=[pl.BlockSpec((2, bm, w), lambda i: (0, i, 0)),
                  pl.BlockSpec((bm, w), lambda i: (i, col0 // w)),
                  pl.BlockSpec((1, w), lambda i: (0, 0)),
                  pl.BlockSpec((None, w, w), lambda i: (layer, 0, 0))],
        out_specs=pl.BlockSpec((bm, w), lambda i: (i, 0)),
        out_shape=jax.ShapeDtypeStruct((m, w), BF16),
        scratch_shapes=[pltpu.VMEM((w, w), BF16)],
        compiler_params=_params(("arbitrary",), vmem),
    )(y2, p, d_vec.reshape(1, w), w_glu)


def _is_new_expert(be_ref, i):
    return (i == 0) | (be_ref[i] != be_ref[jnp.maximum(i - 1, 0)])


def _moe_glu_kernel(be_ref, nb_ref, x_ref, wg_ref, wu_ref, o_ref, wg_bf, wu_bf):
    i = pl.program_id(1)

    @pl.when(_is_new_expert(be_ref, i))
    def _():
        wg_bf[...] = wg_ref[...].astype(BF16)
        wu_bf[...] = wu_ref[...].astype(BF16)

    @pl.when(i < nb_ref[0])
    def _():
        x = x_ref[...]
        a = jnp.dot(x, wg_bf[...], preferred_element_type=F32)
        b = jnp.dot(x, wu_bf[...], preferred_element_type=F32)
        o_ref[...] = (a * jax.nn.sigmoid(a) * b).astype(o_ref.dtype)

    @pl.when(i >= nb_ref[0])
    def _():
        o_ref[...] = jnp.zeros_like(o_ref)


def moe_glu_up(x, wg, wu, layer, block_expert, nblocks, *, bm, bn=512):
    r, k = x.shape
    n = wg.shape[-1]
    assert r % bm == 0 and n % bn == 0
    wspec = pl.BlockSpec((None, None, k, bn), lambda j, i, be, nb: (layer, be[i], 0, j))
    vmem = (2 * _nbytes((bm, k), x.dtype) + 4 * _nbytes((k, bn), F32) + 2 * _nbytes((k, bn), BF16)
            + 2 * _nbytes((bm, bn), BF16) + 3 * _nbytes((bm, bn), F32))
    return pl.pallas_call(
        _moe_glu_kernel, name="moe_glu_up",
        grid_spec=pltpu.PrefetchScalarGridSpec(
            num_scalar_prefetch=2,
            grid=(n // bn, r // bm),
            in_specs=[pl.BlockSpec((bm, k), lambda j, i, be, nb: (i, 0)), wspec, wspec],
            out_specs=pl.BlockSpec((bm, bn), lambda j, i, be, nb: (i, j)),
            scratch_shapes=[pltpu.VMEM((k, bn), BF16), pltpu.VMEM((k, bn), BF16)]),
        out_shape=jax.ShapeDtypeStruct((r, n), BF16),
        compiler_params=_params(("arbitrary", "arbitrary"), vmem),
    )(block_expert, nblocks, x, wg, wu)


def _moe_down_kernel(be_ref, nb_ref, x_ref, w_ref, o_ref, w_bf):
    i = pl.program_id(1)

    @pl.when(_is_new_expert(be_ref, i))
    def _():
        w_bf[...] = w_ref[...].astype(BF16)

    @pl.when(i < nb_ref[0])
    def _():
        o_ref[...] = jnp.dot(x_ref[...], w_bf[...], preferred_element_type=F32)

    @pl.when(i >= nb_ref[0])
    def _():
        o_ref[...] = jnp.zeros_like(o_ref)


def moe_down(x, w, layer, block_expert, nblocks, *, bm, bn=512):
    r, k = x.shape
    n = w.shape[-1]
    assert r % bm == 0 and n % bn == 0
    vmem = (2 * _nbytes((bm, k), x.dtype) + 2 * _nbytes((k, bn), F32) + _nbytes((k, bn), BF16)
            + 2 * _nbytes((bm, bn), F32))
    return pl.pallas_call(
        _moe_down_kernel, name="moe_down",
        grid_spec=pltpu.PrefetchScalarGridSpec(
            num_scalar_prefetch=2,
            grid=(n // bn, r // bm),
            in_specs=[pl.BlockSpec((bm, k), lambda j, i, be, nb: (i, 0)),
                      pl.BlockSpec((None, None, k, bn), lambda j, i, be, nb: (layer, be[i], 0, j))],
            out_specs=pl.BlockSpec((bm, bn), lambda j, i, be, nb: (i, j)),
            scratch_shapes=[pltpu.VMEM((k, bn), BF16)]),
        out_shape=jax.ShapeDtypeStruct((r, n), F32),
        compiler_params=_params(("arbitrary", "arbitrary"), vmem),
    )(block_expert, nblocks, x, w)


def _qk(q, k):
    return lax.dot_general(q, k, (((1,), (1,)), ((), ())), preferred_element_type=F32)


def _row_sinks(sink_ref, first_head, rep, rows_per_head):
    r = lax.broadcasted_iota(jnp.int32, (rep * rows_per_head, 1), 0)
    sink = jnp.full((rep * rows_per_head, 1), sink_ref[first_head], F32)
    for g in range(1, rep):
        sink = jnp.where(r >= g * rows_per_head, sink_ref[first_head + g], sink)
    return sink


def _stack_heads(q_ref, heads, d):
    parts = [q_ref[:, h * d:(h + 1) * d] for h in heads]
    return parts[0] if len(parts) == 1 else jnp.concatenate(parts, axis=0)


def _attn_kernel(*refs, heads, kv_heads, dq, dv, scale, use_sink, use_ctx):
    refs = list(refs)
    sink_ref = refs.pop(0) if use_sink else None
    q_ref, k_ref, v_ref = refs[:3]
    kc_ref, vc_ref = (refs[3], refs[4]) if use_ctx else (None, None)
    o_ref = refs[-1]
    rep = heads // kv_heads
    bq = q_ref.shape[0]
    pieces = [None] * heads
    for j in range(kv_heads):
        q = _stack_heads(q_ref, range(j * rep, (j + 1) * rep), dq)
        s = _qk(q, k_ref[:, j * dq:(j + 1) * dq]) * scale
        m = jnp.max(s, axis=-1, keepdims=True)
        if use_ctx:
            s_c = _qk(q, kc_ref[:, j * dq:(j + 1) * dq].astype(BF16)) * scale
            m = jnp.maximum(m, jnp.max(s_c, axis=-1, keepdims=True))
        if use_sink:
            sink = _row_sinks(sink_ref, j * rep, rep, bq)
            m = jnp.maximum(m, sink)
        p = jnp.exp(s - m)
        l = jnp.sum(p, axis=-1, keepdims=True)
        o = jnp.dot(p.astype(BF16), v_ref[:, j * dv:(j + 1) * dv], preferred_element_type=F32)
        if use_ctx:
            p_c = jnp.exp(s_c - m)
            l = l + jnp.sum(p_c, axis=-1, keepdims=True)
            o = o + jnp.dot(p_c.astype(BF16), vc_ref[:, j * dv:(j + 1) * dv].astype(BF16), preferred_element_type=F32)
        if use_sink:
            l = l + jnp.exp(sink - m)
        o = o * (1.0 / l)
        for g in range(rep):
            pieces[j * rep + g] = o[g * bq:(g + 1) * bq]
    o_ref[...] = jnp.concatenate(pieces, axis=1).astype(o_ref.dtype)


def attention(q, k, v, *, heads, kv_heads, scale, sink=None, k_ctx=None, v_ctx=None, bq=512):
    b, t, qw = q.shape
    s = k.shape[1]
    dq, dv = qw // heads, v.shape[2] // kv_heads
    rep = heads // kv_heads
    bq = _tile(t, bq)
    use_sink, use_ctx = sink is not None, k_ctx is not None
    in_specs = [pl.BlockSpec((None, bq, qw), lambda bi, qi: (bi, qi, 0)),
                pl.BlockSpec((None, s, k.shape[2]), lambda bi, qi: (bi, 0, 0)),
                pl.BlockSpec((None, s, v.shape[2]), lambda bi, qi: (bi, 0, 0))]
    args = [q, k, v]
    c = 0
    if use_ctx:
        c = k_ctx.shape[1]
        in_specs += [pl.BlockSpec((None, c, k_ctx.shape[2]), lambda bi, qi: (bi, 0, 0)),
                     pl.BlockSpec((None, c, v_ctx.shape[2]), lambda bi, qi: (bi, 0, 0))]
        args += [k_ctx, v_ctx]
    if use_sink:
        in_specs = [pl.BlockSpec(memory_space=pltpu.SMEM)] + in_specs
        args = [sink.astype(F32)] + args
    vmem = (5 * _nbytes((rep * bq, s + c), F32) + 2 * _nbytes((bq, qw), BF16) + 2 * _nbytes((s, k.shape[2] + v.shape[2]), BF16)
            + 3 * _nbytes((c, k.shape[2] + v.shape[2]), F32) + 6 * _nbytes((bq, heads * dv), F32))
    return pl.pallas_call(
        functools.partial(_attn_kernel, heads=heads, kv_heads=kv_heads, dq=dq, dv=dv, scale=scale,
                          use_sink=use_sink, use_ctx=use_ctx), name="attention",
        grid=(b, t // bq),
        in_specs=in_specs,
        out_specs=pl.BlockSpec((None, bq, heads * dv), lambda bi, qi: (bi, qi, 0)),
        out_shape=jax.ShapeDtypeStruct((b, t, heads * dv), BF16),
        compiler_params=_params(("arbitrary",) * 2, vmem),
    )(*args)


def _diff_attn_kernel(*refs, heads, scale, post_scale, use_ctx):
    lam_ref, q_ref, k_ref, v_ref = refs[:4]
    kc_ref, vc_ref = (refs[4], refs[5]) if use_ctx else (None, None)
    g_ref, o_ref = refs[-2:]
    d, dv = DIFF_HD, DIFF_VD

    def probs(c0):
        q = q_ref[:, c0:c0 + d]
        s = _qk(q, k_ref[:, c0:c0 + d]) * scale
        m = jnp.max(s, axis=-1, keepdims=True)
        if use_ctx:
            s_c = _qk(q, kc_ref[:, c0:c0 + d].astype(BF16)) * scale
            m = jnp.maximum(m, jnp.max(s_c, axis=-1, keepdims=True))
        p = jnp.exp(s - m)
        l = jnp.sum(p, axis=-1, keepdims=True)
        if not use_ctx:
            return p * (1.0 / l), None
        p_c = jnp.exp(s_c - m)
        inv = 1.0 / (l + jnp.sum(p_c, axis=-1, keepdims=True))
        return p * inv, p_c * inv

    pieces = []
    for h in range(heads):
        p0, pc0 = probs(2 * h * d)
        p1, pc1 = probs((2 * h + 1) * d)
        lam = lam_ref[h]
        o = jnp.dot((p0 - lam * p1).astype(BF16), v_ref[:, h * dv:(h + 1) * dv], preferred_element_type=F32)
        if use_ctx:
            o = o + jnp.dot((pc0 - lam * pc1).astype(BF16), vc_ref[:, h * dv:(h + 1) * dv].astype(BF16),
                            preferred_element_type=F32)
        o = o * lax.rsqrt(jnp.mean(o * o, axis=-1, keepdims=True) + EPS) * g_ref[...]
        pieces.append(o * post_scale)
    o_ref[...] = jnp.concatenate(pieces, axis=1).astype(o_ref.dtype)


def diff_attention(q, k, v, lam, g_out, lambda_init, *, k_ctx=None, v_ctx=None, bq=512):
    b, t, qw = q.shape
    s, vw = k.shape[1], v.shape[2]
    heads = vw // DIFF_VD
    bq = _tile(t, bq)
    use_ctx = k_ctx is not None
    in_specs = [pl.BlockSpec(memory_space=pltpu.SMEM),
                pl.BlockSpec((None, bq, qw), lambda bi, qi: (bi, qi, 0)),
                pl.BlockSpec((None, s, qw), lambda bi, qi: (bi, 0, 0)),
                pl.BlockSpec((None, s, vw), lambda bi, qi: (bi, 0, 0))]
    args = [lam.astype(F32), q, k, v]
    c = 0
    if use_ctx:
        c = k_ctx.shape[1]
        in_specs += [pl.BlockSpec((None, c, qw), lambda bi, qi: (bi, 0, 0)),
                     pl.BlockSpec((None, c, vw), lambda bi, qi: (bi, 0, 0))]
        args += [k_ctx, v_ctx]
    in_specs.append(pl.BlockSpec((1, DIFF_VD), lambda bi, qi: (0, 0)))
    args.append(g_out.astype(F32).reshape(1, DIFF_VD))
    vmem = (8 * _nbytes((bq, s + c), F32) + 2 * _nbytes((bq, qw), BF16) + 2 * _nbytes((s, qw + vw), BF16)
            + 3 * _nbytes((c, qw + vw), F32) + 6 * _nbytes((bq, vw), F32))
    return pl.pallas_call(
        functools.partial(_diff_attn_kernel, heads=heads, scale=DIFF_HD ** -0.5, post_scale=1.0 - lambda_init,
                          use_ctx=use_ctx), name="diff_attention",
        grid=(b, t // bq),
        in_specs=in_specs,
        out_specs=pl.BlockSpec((None, bq, vw), lambda bi, qi: (bi, qi, 0)),
        out_shape=jax.ShapeDtypeStruct((b, t, vw), BF16),
        compiler_params=_params(("arbitrary",) * 2, vmem),
    )(*args)


def _swa_lat_kernel(sink_ref, q_ref, kc_ref, vc_ref, k0_ref, k1_ref, k2_ref, v0_ref, v1_ref, v2_ref, o_ref,
                    *, scale, nblocks):
    n = pl.program_id(1)
    rows = SWA_REP * BLOCK
    r = lax.broadcasted_iota(jnp.int32, (rows, 3 * BLOCK), 0) & (BLOCK - 1)
    c = lax.broadcasted_iota(jnp.int32, (rows, 3 * BLOCK), 1)
    kpos = (n - 1) * BLOCK + c
    mask = (c - r >= BLOCK - WINDOW) & (c - r <= BLOCK + WINDOW) & (kpos >= 0) & (kpos < nblocks * BLOCK)
    pieces = [None] * SWA_HEADS
    for j in range(SWA_KV_HEADS):
        cs = slice(j * SWA_HD, (j + 1) * SWA_HD)
        q = _stack_heads(q_ref, range(j * SWA_REP, (j + 1) * SWA_REP), SWA_HD)
        kw = jnp.concatenate([k0_ref[:, cs], k1_ref[:, cs], k2_ref[:, cs]], axis=0)
        vw = jnp.concatenate([v0_ref[:, cs], v1_ref[:, cs], v2_ref[:, cs]], axis=0)
        s_ctx = _qk(q, kc_ref[:, cs].astype(BF16)) * scale
        s_loc = jnp.where(mask, _qk(q, kw) * scale, NEG_INF)
        sink = _row_sinks(sink_ref, j * SWA_REP, SWA_REP, BLOCK)
        m = jnp.maximum(jnp.maximum(jnp.max(s_ctx, axis=-1, keepdims=True), jnp.max(s_loc, axis=-1, keepdims=True)), sink)
        p_ctx = jnp.exp(s_ctx - m)
        p_loc = jnp.exp(s_loc - m)
        l = jnp.sum(p_ctx, axis=-1, keepdims=True) + jnp.sum(p_loc, axis=-1, keepdims=True) + jnp.exp(sink - m)
        o = (jnp.dot(p_ctx.astype(BF16), vc_ref[:, cs].astype(BF16), preferred_element_type=F32)
             + jnp.dot(p_loc.astype(BF16), vw, preferred_element_type=F32)) * (1.0 / l)
        for g in range(SWA_REP):
            pieces[j * SWA_REP + g] = o[g * BLOCK:(g + 1) * BLOCK]
    o_ref[...] = jnp.concatenate(pieces, axis=1).astype(o_ref.dtype)


def swa_latent_attention(q, k, v, k_ctx, v_ctx, sink):
    b, t, qw = q.shape
    kw = k.shape[2]
    c = k_ctx.shape[1]
    nb = t // BLOCK
    pad = ((0, 0), (BLOCK, BLOCK), (0, 0))
    kp, vp = jnp.pad(k, pad), jnp.pad(v, pad)
    band = [pl.BlockSpec((None, BLOCK, kw), functools.partial(lambda bi, ni, o: (bi, ni + o, 0), o=o)) for o in range(3)]
    ctx = pl.BlockSpec((None, c, kw), lambda bi, ni: (bi, 0, 0))
    qspec = pl.BlockSpec((None, BLOCK, qw), lambda bi, ni: (bi, ni, 0))
    vmem = 8 * _nbytes((SWA_REP * BLOCK, c + 3 * BLOCK), F32) + 6 * _nbytes((c, kw), F32) + (4 << 20)
    return pl.pallas_call(
        functools.partial(_swa_lat_kernel, scale=SWA_HD ** -0.5, nblocks=nb), name="swa_latent_attention",
        grid=(b, nb),
        in_specs=[pl.BlockSpec(memory_space=pltpu.SMEM), qspec, ctx, ctx] + band + band,
        out_specs=qspec,
        out_shape=jax.ShapeDtypeStruct(q.shape, BF16),
        compiler_params=_params(("arbitrary",) * 2, vmem),
    )(sink.astype(F32), q, k_ctx, v_ctx, kp, kp, kp, vp, vp, vp)


def _rms(x, g):
    xf = x.astype(F32)
    return xf * lax.rsqrt(jnp.mean(xf * xf, axis=-1, keepdims=True) + EPS) * g.astype(F32)


def _axial_rope(x):
    t, d = x.shape[1], x.shape[-1]
    rows = t // GRID_W
    r, col = jnp.meshgrid(jnp.arange(rows, dtype=F32), jnp.arange(GRID_W, dtype=F32), indexing='ij')
    n_freq = d // 4
    inv = ROPE_THETA ** (-jnp.arange(n_freq, dtype=F32) / n_freq)
    ang = jnp.concatenate([r.reshape(-1, 1) * inv, col.reshape(-1, 1) * inv], axis=-1)
    bshape = (1, t) + (1,) * (x.ndim - 3) + (d // 2,)
    cos, sin = jnp.cos(ang).reshape(bshape), jnp.sin(ang).reshape(bshape)
    x1, x2 = x[..., : d // 2], x[..., d // 2:]
    return jnp.concatenate([x1 * cos - x2 * sin, x2 * cos + x1 * sin], axis=-1)


def _rope_tail(x):
    return jnp.concatenate([x[..., :MLA_NOPE], _axial_rope(x[..., MLA_NOPE:])], axis=-1)


def _state_to_halves(st):
    b, nd, _, g, n = st.shape
    gh = S5_HALF // SSM_CH
    z = st.reshape(b, nd, 2, g // gh, gh * n)
    return jnp.transpose(z, (1, 3, 0, 2, 4)).reshape(nd, g // gh, b, 2 * gh * n)


def _halves_to_state(h, groups):
    nd, nh, b, _ = h.shape
    z = h.reshape(nd, nh, b, 2, -1)
    return jnp.transpose(z, (2, 0, 3, 1, 4)).reshape(b, nd, 2, groups, SSM_N)


def _moe_ffn(logits, h_bf, P, j, bm):
    m = h_bf.shape[0]
    probs = jax.nn.softmax(logits, axis=-1)
    top_p, top_i = lax.top_k(probs, TOP_K)
    top_p = top_p / jnp.sum(top_p, axis=-1, keepdims=True)
    flat_e = top_i.reshape(-1).astype(jnp.int32)
    npairs = flat_e.shape[0]
    order = jnp.argsort(flat_e, stable=True).astype(jnp.int32)
    counts = jnp.sum(flat_e[:, None] == jnp.arange(N_EXPERTS, dtype=jnp.int32)[None, :], axis=0).astype(jnp.int32)
    blocks_per = (counts + bm - 1) // bm
    block_end = jnp.cumsum(blocks_per)
    pad_start = (block_end - blocks_per) * bm
    start = jnp.cumsum(counts) - counts
    sorted_e = flat_e[order]
    dest = pad_start[sorted_e] + jnp.arange(npairs, dtype=jnp.int32) - start[sorted_e]
    nrows = npairs + N_EXPERTS * bm
    row_token = jnp.zeros((nrows,), jnp.int32).at[dest].set(order // TOP_K)
    pos = jnp.zeros((npairs,), jnp.int32).at[order].set(dest).reshape(m, TOP_K)
    nblk = nrows // bm
    nused = block_end[-1:]
    block_expert = jnp.minimum(jnp.sum(jnp.arange(nblk, dtype=jnp.int32)[:, None] >= block_end[None, :], axis=1),
                               N_EXPERTS - 1).astype(jnp.int32)
    block_expert = jnp.where(jnp.arange(nblk) < nused[0], block_expert, block_expert[jnp.maximum(nused[0] - 1, 0)])
    x_sorted = jnp.take(h_bf, row_token, axis=0)
    act = moe_glu_up(x_sorted, P['w_moe_gate'], P['w_moe_up'], j, block_expert, nused, bm=bm, bn=1024)
    y_sorted = moe_down(act, P['w_moe_down'], j, block_expert, nused, bm=bm)
    out = jnp.zeros((m, y_sorted.shape[1]), F32)
    for kk in range(TOP_K):
        out = out + top_p[:, kk:kk + 1] * jnp.take(y_sorted, pos[:, kk], axis=0)
    return out


SWA_COLS = (SWA_HEADS + 2 * SWA_KV_HEADS) * SWA_HD
DIFF_COLS = 2 * DIFF_HEADS * 2 * DIFF_HD + DIFF_HEADS * DIFF_VD


def kernel(x_prompt, x_sample, cache_mla_ckv, cache_mla_krope, state_ssm, cache_swa_k, cache_swa_v, cache_diff_k, cache_diff_v, c, c_ctx, w_ada, b_ada, g_norm1, g_norm2, w_in, g_mla_qlat, g_mla_kvlat, w_mla_uq, w_mla_ukv, g_mla_q, g_mla_k, ssm_a_re, ssm_a_im, ssm_log_dt, ssm_b_re, ssm_b_im, ssm_c_re, ssm_c_im, ssm_d, w_ssm_glu, g_swa_q, g_swa_k, swa_sink, g_diff_q, g_diff_k, diff_lambda, g_diff_out, w_branch, w_out, w_ffn_gate, w_ffn_up, w_ffn_down, w_router, w_moe_gate, w_moe_up, w_moe_down):
    P = dict(ssm_a_re=ssm_a_re, ssm_a_im=ssm_a_im, ssm_log_dt=ssm_log_dt, ssm_b_re=ssm_b_re, ssm_b_im=ssm_b_im,
             ssm_c_re=ssm_c_re, ssm_c_im=ssm_c_im, ssm_d=ssm_d, w_router=w_router, w_moe_gate=w_moe_gate,
             w_moe_up=w_moe_up, w_moe_down=w_moe_down)
    bp, tp, d = x_prompt.shape
    bs, ts, _ = x_sample.shape
    depth = w_in.shape[0]
    past = cache_mla_ckv.shape[2]
    mp, ms = bp * tp, bs * ts
    m = mp + ms
    w = d // 4
    groups = w // SSM_CH
    x = jnp.concatenate([x_prompt.reshape(mp, d), x_sample.reshape(ms, d)], axis=0)

    seg = math.gcd(mp, ts)
    nseg = m // seg
    seg_mod = np.concatenate([np.zeros(mp // seg, np.int32), 1 + np.repeat(np.arange(bs, dtype=np.int32), ts // seg)])
    cvec = jnp.concatenate([c_ctx[None, :], c, jnp.zeros((8 - 1 - bs, d), F32)], axis=0)
    cvec = cvec * jax.nn.sigmoid(cvec)

    def per_seg(v):
        return v[seg_mod][:, None, :]

    def gated_add(xx, gt_seg, upd):
        return (xx.reshape(nseg, seg, d) + gt_seg * upd.reshape(nseg, seg, d)).reshape(m, d)

    outs = [[] for _ in range(7)]
    for l in range(depth):
        lambda_init = 0.8 - 0.6 * math.exp(-0.3 * l)
        mods = matmul(cvec, w_ada, (l,), bm=8, bn=1024) + b_ada[l][None, :]
        sh1, sc1, gt1, sh2, sc2, gt2 = (per_seg(v) for v in jnp.split(mods, 6, axis=-1))

        h = modulated_norm(x, g_norm1[l], sc1, sh1)
        p, col0 = in_projection(h, w_in, l)
        seg_cols = {name: p[:, col0[name]:col0[name] + width]
                    for name, width in (('q', w), ('ckv', w), ('swa', SWA_COLS), ('diff', DIFF_COLS), ('krope', MLA_ROPE))}

        def tm(z, bb, tt):
            return z.reshape(bb, tt, -1).astype(BF16)

        c_kv = _rms(seg_cols['ckv'], g_mla_kvlat[l])
        k_rope = seg_cols['krope']
        q_lat = _rms(seg_cols['q'], g_mla_qlat[l]).astype(BF16)
        q_m = _rms(matmul(q_lat, w_mla_uq, (l,)).reshape(m, MLA_HEADS, MLA_QK), g_mla_q[l])
        ckv_all = jnp.concatenate([c_kv, cache_mla_ckv[:, l].reshape(bs * past, w)], axis=0).astype(BF16)
        krope_all = jnp.concatenate([k_rope, cache_mla_krope[:, l].reshape(bs * past, MLA_ROPE)], axis=0)
        kv = matmul(ckv_all, w_mla_ukv, (l,), bm=512).reshape(-1, MLA_HEADS, MLA_NOPE + MLA_V)
        k_r = jnp.broadcast_to(krope_all[:, None, :], (kv.shape[0], MLA_HEADS, MLA_ROPE))
        k_m = _rms(jnp.concatenate([kv[..., :MLA_NOPE], k_r], axis=-1), g_mla_k[l])
        v_m = kv[..., MLA_NOPE:]
        mla = dict(heads=MLA_HEADS, kv_heads=MLA_HEADS, scale=MLA_QK ** -0.5)
        o_mla_p = attention(tm(q_m[:mp], bp, tp), tm(k_m[:mp], bp, tp), tm(v_m[:mp], bp, tp), **mla)
        q_l = _rope_tail(q_m[mp:].reshape(bs, ts, MLA_HEADS, MLA_QK))
        k_l = _rope_tail(k_m[mp:m].reshape(bs, ts, MLA_HEADS, MLA_QK))
        o_mla_s = attention(tm(q_l, bs, ts), tm(k_l, bs, ts), tm(v_m[mp:m], bs, ts),
                            k_ctx=tm(k_m[m:], bs, past), v_ctx=tm(v_m[m:], bs, past), **mla)
        o_mla = jnp.concatenate([o_mla_p.reshape(mp, w), o_mla_s.reshape(ms, w)], axis=0)

        tabs = _s5_tables(ssm_a_re[l], ssm_a_im[l], ssm_log_dt[l], ssm_b_re[l], ssm_b_im[l], ssm_c_re[l], ssm_c_im[l])
        ncol = p.shape[1]
        h0_p = jnp.zeros((2, w // S5_HALF, bp, 2 * (S5_HALF // SSM_CH) * SSM_N), F32)
        y_p, hT_p = s5_scan(p.reshape(m // tp, tp, ncol), col0['ssm'], 0, bp, tabs, h0_p)
        y_s, _ = s5_scan(p.reshape(m // ts, ts, ncol), col0['ssm'], mp // ts, bs, tabs, _state_to_halves(state_ssm[:, l]))
        y2 = jnp.concatenate([y_p.reshape(2, mp, w), y_s.reshape(2, ms, w)], axis=1)
        o_ssm = ssm_post(y2, p, col0['ssm'], ssm_d[l], w_ssm_glu, l)
        ssm_state = _halves_to_state(hT_p, groups)

        ps = seg_cols['swa']
        nq = SWA_HEADS * SWA_HD
        nk = SWA_KV_HEADS * SWA_HD
        q_s = _rms(ps[:, :nq].reshape(m, SWA_HEADS, SWA_HD), g_swa_q[l])
        k_s = _rms(ps[:, nq:nq + nk].reshape(m, SWA_KV_HEADS, SWA_HD), g_swa_k[l])
        v_s = ps[:, nq + nk:]
        o_swa_p = attention(tm(q_s[:mp], bp, tp), tm(k_s[:mp], bp, tp), tm(v_s[:mp], bp, tp), heads=SWA_HEADS,
                            kv_heads=SWA_KV_HEADS, scale=SWA_HD ** -0.5, sink=swa_sink[l], bq=256)
        q_sl = _axial_rope(q_s[mp:].reshape(bs, ts, SWA_HEADS, SWA_HD))
        k_sl = _axial_rope(k_s[mp:].reshape(bs, ts, SWA_KV_HEADS, SWA_HD))
        o_swa_s = swa_latent_attention(tm(q_sl, bs, ts), tm(k_sl, bs, ts), tm(v_s[mp:], bs, ts),
                                       cache_swa_k[:, l].reshape(bs, past, nk), cache_swa_v[:, l].reshape(bs, past, nk),
                                       swa_sink[l])
        o_swa = jnp.concatenate([o_swa_p.reshape(mp, w), o_swa_s.reshape(ms, w)], axis=0)

        pd = seg_cols['diff']
        ndq = DIFF_HEADS * 2 * DIFF_HD
        q_d = _rms(pd[:, :ndq].reshape(m, DIFF_HEADS, 2, DIFF_HD), g_diff_q[l])
        k_d = _rms(pd[:, ndq:2 * ndq].reshape(m, DIFF_HEADS, 2, DIFF_HD), g_diff_k[l])
        v_d = pd[:, 2 * ndq:]
        lp = diff_lambda[l]
        lam = jnp.exp(jnp.sum(lp[0] * lp[1], -1)) - jnp.exp(jnp.sum(lp[2] * lp[3], -1)) + lambda_init
        o_diff_p = diff_attention(tm(q_d[:mp], bp, tp), tm(k_d[:mp], bp, tp), tm(v_d[:mp], bp, tp),
                                  lam, g_diff_out[l], lambda_init, bq=256)
        q_dl = _axial_rope(q_d[mp:].reshape(bs, ts, DIFF_HEADS, 2, DIFF_HD))
        k_dl = _axial_rope(k_d[mp:].reshape(bs, ts, DIFF_HEADS, 2, DIFF_HD))
        o_diff_s = diff_attention(tm(q_dl, bs, ts), tm(k_dl, bs, ts), tm(v_d[mp:], bs, ts), lam, g_diff_out[l], lambda_init,
                                  k_ctx=cache_diff_k[:, l].reshape(bs, past, ndq),
                                  v_ctx=cache_diff_v[:, l].reshape(bs, past, DIFF_HEADS * DIFF_VD), bq=256)
        o_diff = jnp.concatenate([o_diff_p.reshape(mp, w), o_diff_s.reshape(ms, w)], axis=0)

        merged = merge_branches((o_mla, o_ssm, o_swa, o_diff), p, col0['gate'], w_branch, l, bm=1024, bn=256)
        x = matmul_residual(merged, w_out, (l,), x, gt1)

        j = l // 2
        if l % 2 == 0:
            h2 = modulated_norm(x, g_norm2[l], sc2, sh2)
            act = glu_up(h2, w_ffn_gate, w_ffn_up, (j,))
            x = matmul_residual(act, w_ffn_down, (j,), x, gt2)
        else:
            h2, logits = modulated_norm(x, g_norm2[l], sc2, sh2, w_router, j)
            x = gated_add(x, gt2, _moe_ffn(logits, h2, P, j, bm=256))

        outs[0].append(c_kv[:mp].reshape(bp, tp, w))
        outs[1].append(k_rope[:mp].reshape(bp, tp, MLA_ROPE))
        outs[2].append(ssm_state)
        outs[3].append(k_s[:mp].reshape(bp, tp, SWA_KV_HEADS, SWA_HD))
        outs[4].append(v_s[:mp].reshape(bp, tp, SWA_KV_HEADS, SWA_HD))
        outs[5].append(k_d[:mp].reshape(bp, tp, DIFF_HEADS, 2, DIFF_HD))
        outs[6].append(v_d[:mp].reshape(bp, tp, DIFF_HEADS, DIFF_VD))

    return (x[:mp].reshape(bp, tp, d), x[mp:].reshape(bs, ts, d)) + tuple(jnp.stack(o, axis=1) for o in outs)
```

```python
import functools
import math

import numpy as np
import jax
import jax.numpy as jnp
from jax import lax
from jax.experimental import pallas as pl
from jax.experimental.pallas import tpu as pltpu

F32 = jnp.float32
BF16 = jnp.bfloat16
EPS = 1e-6
NEG_INF = -1e30
ROPE_THETA = 10000.0
GRID_W = 64
BLOCK = 128
N_BRANCH = 4
MLA_HEADS = 4
MLA_NOPE = 128
MLA_ROPE = 64
MLA_QK = MLA_NOPE + MLA_ROPE
MLA_V = 128
SSM_CH = 16
SSM_N = 64
SWA_HEADS = 8
SWA_KV_HEADS = 2
SWA_REP = SWA_HEADS // SWA_KV_HEADS
SWA_HD = 64
WINDOW = 128
DIFF_HEADS = 4
DIFF_HD = 64
DIFF_VD = 2 * DIFF_HD
N_EXPERTS = 8
TOP_K = 2

V7X_VMEM_REQUEST_CAP = 56 * 1024 * 1024


def _params(semantics, vmem_bytes):
    limit = int(min(max(vmem_bytes * 5 // 4 + (2 << 20), 16 << 20), V7X_VMEM_REQUEST_CAP))
    return pltpu.CompilerParams(dimension_semantics=semantics, vmem_limit_bytes=limit)


def _tile(n, want):
    t = min(n, want)
    while n % t:
        t -= 1
    return t


def _nbytes(shape, dtype):
    return int(np.prod(shape)) * jnp.dtype(dtype).itemsize


def _mm_kernel(x_ref, w_ref, o_ref, *scratch):
    if scratch:
        wbf_ref, = scratch

        @pl.when(pl.program_id(1) == 0)
        def _():
            wbf_ref[...] = w_ref[...].astype(BF16)
        w = wbf_ref[...]
    else:
        w = w_ref[...]
    o_ref[...] = jnp.dot(x_ref[...].astype(BF16), w, preferred_element_type=F32).astype(o_ref.dtype)


def matmul(x, w, lead=(), *, bm=1024, bn=512, out_dtype=F32, cols=None):
    m, k = x.shape
    assert w.shape[-2] == k
    cb0 = 0
    if cols is None:
        n = w.shape[-1]
        bn = _tile(n, bn)
    else:
        cb0, n = cols[0], cols[1] * bn
    bm = _tile(m, bm)
    assert m % bm == 0 and n % bn == 0, (m, n, bm, bn)
    lead = tuple(lead)
    nl = len(lead)
    cast = w.dtype != BF16
    scratch = [pltpu.VMEM((k, bn), BF16)] if cast else []
    vmem = (2 * _nbytes((bm, k), x.dtype) + 2 * _nbytes((k, bn), w.dtype)
            + 2 * _nbytes((bm, bn), out_dtype) + (_nbytes((k, bn), BF16) if cast else 0))
    return pl.pallas_call(
        _mm_kernel, name="matmul",
        grid=(n // bn, m // bm),
        in_specs=[pl.BlockSpec((bm, k), lambda j, i: (i, 0)),
                  pl.BlockSpec((None,) * nl + (k, bn), lambda j, i: lead + (0, cb0 + j))],
        out_specs=pl.BlockSpec((bm, bn), lambda j, i: (i, j)),
        out_shape=jax.ShapeDtypeStruct((m, n), out_dtype),
        scratch_shapes=scratch,
        compiler_params=_params(("arbitrary", "arbitrary"), vmem),
    )(x, w)


IN_TILE = 1024
IN_SHIFT_UNIT = 128
IN_CAST_ROWS = 512


def _inproj_kernel(a_idx, b_idx, shift_ref, x_ref, wa_ref, wb_ref, o_ref, wbf_ref, *, shift):
    j = pl.program_id(0)
    first = pl.program_id(1) == 0

    @pl.when(first & (shift_ref[j] == 0))
    def _():
        wbf_ref[...] = wa_ref[...].astype(BF16)

    @pl.when(first & (shift_ref[j] != 0))
    def _():
        for r in range(0, wa_ref.shape[0], IN_CAST_ROWS):
            rows = slice(r, r + IN_CAST_ROWS)
            wcat = jnp.concatenate([wa_ref[rows, :], wb_ref[rows, :]], axis=1)
            wbf_ref[rows, :] = wcat[:, shift:shift + IN_TILE].astype(BF16)

    o_ref[...] = jnp.dot(x_ref[...], wbf_ref[...], preferred_element_type=F32)


def in_projection(h, w_in, layer, *, bm=1024):
    m, k = h.shape
    ntot = w_in.shape[-1]
    w = k // 4
    lead = 2 * w
    rest = ntot - lead - MLA_ROPE
    assert lead % IN_TILE == 0 and MLA_ROPE < IN_SHIFT_UNIT and IN_TILE % IN_SHIFT_UNIT == 0 and k % IN_CAST_ROWS == 0
    n_lead, n_rest = lead // IN_TILE, -(-rest // IN_TILE)
    last_unit = (ntot - 1) // IN_SHIFT_UNIT
    a_idx = list(range(n_lead)) + [n_lead + t for t in range(n_rest)]
    b_idx = [0] * n_lead + [min((lead + IN_TILE * (t + 1)) // IN_SHIFT_UNIT, last_unit) for t in range(n_rest)]
    shifted = [0] * n_lead + [1] * n_rest
    ntiles = len(a_idx)
    col0 = dict(q=0, ckv=w, ssm=lead, swa=lead + w, diff=lead + w + SWA_COLS, gate=lead + w + SWA_COLS + DIFF_COLS)
    bm = _tile(m, bm)
    vmem = (2 * _nbytes((bm, k), BF16) + 2 * _nbytes((k, IN_TILE + IN_SHIFT_UNIT), F32) + _nbytes((k, IN_TILE), BF16)
            + 3 * _nbytes((IN_CAST_ROWS, IN_TILE + IN_SHIFT_UNIT), F32) + 2 * _nbytes((bm, IN_TILE), F32))
    p = pl.pallas_call(
        functools.partial(_inproj_kernel, shift=MLA_ROPE), name="in_projection",
        grid_spec=pltpu.PrefetchScalarGridSpec(
            num_scalar_prefetch=3,
            grid=(ntiles, m // bm),
            in_specs=[pl.BlockSpec((bm, k), lambda j, i, a, b, s: (i, 0)),
                      pl.BlockSpec((None, k, IN_TILE), lambda j, i, a, b, s: (layer, 0, a[j])),
                      pl.BlockSpec((None, k, IN_SHIFT_UNIT), lambda j, i, a, b, s: (layer, 0, b[j]))],
            out_specs=pl.BlockSpec((bm, IN_TILE), lambda j, i, a, b, s: (i, j)),
            scratch_shapes=[pltpu.VMEM((k, IN_TILE), BF16)]),
        out_shape=jax.ShapeDtypeStruct((m, ntiles * IN_TILE), F32),
        compiler_params=_params(("arbitrary", "arbitrary"), vmem),
    )(jnp.asarray(a_idx, jnp.int32), jnp.asarray(b_idx, jnp.int32), jnp.asarray(shifted, jnp.int32), h, w_in, w_in)
    return p, col0


def _glu_kernel(x_ref, wg_ref, wu_ref, o_ref, wg_bf, wu_bf):
    @pl.when(pl.program_id(1) == 0)
    def _():
        wg_bf[...] = wg_ref[...].astype(BF16)
        wu_bf[...] = wu_ref[...].astype(BF16)
    x = x_ref[...]
    a = jnp.dot(x, wg_bf[...], preferred_element_type=F32)
    b = jnp.dot(x, wu_bf[...], preferred_element_type=F32)
    o_ref[...] = (a * jax.nn.sigmoid(a) * b).astype(o_ref.dtype)


def glu_up(x, wg, wu, lead, *, bm=1024, bn=512):
    m, k = x.shape
    n = wg.shape[-1]
    bm, bn = _tile(m, bm), _tile(n, bn)
    assert m % bm == 0 and n % bn == 0
    nl = len(lead)
    wspec = pl.BlockSpec((None,) * nl + (k, bn), lambda j, i: lead + (0, j))
    vmem = (2 * _nbytes((bm, k), x.dtype) + 4 * _nbytes((k, bn), F32) + 2 * _nbytes((k, bn), BF16)
            + 2 * _nbytes((bm, bn), BF16) + 3 * _nbytes((bm, bn), F32))
    return pl.pallas_call(
        _glu_kernel, name="glu_up",
        grid=(n // bn, m // bm),
        in_specs=[pl.BlockSpec((bm, k), lambda j, i: (i, 0)), wspec, wspec],
        out_specs=pl.BlockSpec((bm, bn), lambda j, i: (i, j)),
        out_shape=jax.ShapeDtypeStruct((m, n), BF16),
        scratch_shapes=[pltpu.VMEM((k, bn), BF16), pltpu.VMEM((k, bn), BF16)],
        compiler_params=_params(("arbitrary", "arbitrary"), vmem),
    )(x, wg, wu)


def _mm_res_kernel(a_ref, w_ref, x_ref, gt_ref, o_ref, wbf_ref):
    @pl.when(pl.program_id(1) == 0)
    def _():
        wbf_ref[...] = w_ref[...].astype(BF16)
    upd = jnp.dot(a_ref[...], wbf_ref[...], preferred_element_type=F32)
    o_ref[...] = x_ref[...] + gt_ref[...] * upd


def matmul_residual(a, w, lead, x, gt_seg, *, bm=512, bn=512):
    m, k = a.shape
    n = w.shape[-1]
    seg = m // gt_seg.shape[0]
    bm, bn = _tile(seg, bm), _tile(n, bn)
    nl = len(lead)
    vmem = (2 * _nbytes((bm, k), a.dtype) + 2 * _nbytes((k, bn), F32) + _nbytes((k, bn), BF16)
            + 5 * _nbytes((bm, bn), F32))
    return pl.pallas_call(
        _mm_res_kernel, name="matmul_residual",
        grid=(n // bn, m // bm),
        in_specs=[pl.BlockSpec((bm, k), lambda j, i: (i, 0)),
                  pl.BlockSpec((None,) * nl + (k, bn), lambda j, i: lead + (0, j)),
                  pl.BlockSpec((bm, bn), lambda j, i: (i, j)),
                  pl.BlockSpec((None, 1, bn), lambda j, i: (i * bm // seg, 0, j))],
        out_specs=pl.BlockSpec((bm, bn), lambda j, i: (i, j)),
        out_shape=jax.ShapeDtypeStruct((m, n), F32),
        scratch_shapes=[pltpu.VMEM((k, bn), BF16)],
        compiler_params=_params(("arbitrary", "arbitrary"), vmem),
    )(a, w, x, gt_seg)


def _merge_kernel(*refs):
    nb = N_BRANCH
    o_refs, g_refs = refs[:nb], refs[nb:2 * nb]
    wb_ref, out_ref, wbf_ref = refs[2 * nb:]

    @pl.when(pl.program_id(1) == 0)
    def _():
        wbf_ref[...] = wb_ref[...].astype(BF16)
    acc = None
    for n_ in range(nb):
        up = jnp.dot(o_refs[n_][...], wbf_ref[n_], preferred_element_type=F32)
        t = jax.nn.sigmoid(g_refs[n_][...]) * up
        acc = t if acc is None else acc + t
    out_ref[...] = acc.astype(out_ref.dtype)


def merge_branches(branch_outs, p, gate_col0, w_branch, layer, *, bm=512, bn=512):
    m, bw = branch_outs[0].shape
    dm = w_branch.shape[-1]
    bm, bn = _tile(m, bm), _tile(dm, bn)
    assert gate_col0 % bn == 0
    o_spec = pl.BlockSpec((bm, bw), lambda j, i: (i, 0))
    g_specs = [pl.BlockSpec((bm, bn), functools.partial(lambda j, i, c: (i, c + j), c=(gate_col0 + n_ * dm) // bn))
               for n_ in range(N_BRANCH)]
    vmem = (N_BRANCH * (2 * _nbytes((bm, bw), BF16) + 2 * _nbytes((bm, bn), F32) + 2 * _nbytes((bw, bn), F32)
                        + _nbytes((bw, bn), BF16)) + 6 * _nbytes((bm, bn), F32))
    return pl.pallas_call(
        _merge_kernel, name="merge_branches",
        grid=(dm // bn, m // bm),
        in_specs=[o_spec] * N_BRANCH + g_specs
                 + [pl.BlockSpec((None, N_BRANCH, bw, bn), lambda j, i: (layer, 0, 0, j))],
        out_specs=pl.BlockSpec((bm, bn), lambda j, i: (i, j)),
        out_shape=jax.ShapeDtypeStruct((m, dm), BF16),
        scratch_shapes=[pltpu.VMEM((N_BRANCH, bw, bn), BF16)],
        compiler_params=_params(("arbitrary", "arbitrary"), vmem),
    )(*branch_outs, *([p] * N_BRANCH), w_branch)


def _modnorm_kernel(*refs, with_router):
    if with_router:
        x_ref, g_ref, sc_ref, sh_ref, wr_ref, h_ref, logit_ref = refs
    else:
        x_ref, g_ref, sc_ref, sh_ref, h_ref = refs
    x = x_ref[...]
    y = x * lax.rsqrt(jnp.mean(x * x, axis=-1, keepdims=True) + EPS) * g_ref[...]
    h = y * (1 + sc_ref[...]) + sh_ref[...]
    h_ref[...] = h.astype(h_ref.dtype)
    if with_router:
        logit_ref[...] = jnp.dot(h, wr_ref[...], precision=lax.Precision.HIGHEST, preferred_element_type=F32)


def modulated_norm(x, g, sc_seg, sh_seg, w_router=None, router_layer=0, *, bm=256):
    m, d = x.shape
    seg = m // sc_seg.shape[0]
    bm = _tile(seg, bm)
    mod_spec = pl.BlockSpec((None, 1, d), lambda i: (i * bm // seg, 0, 0))
    in_specs = [pl.BlockSpec((bm, d), lambda i: (i, 0)), pl.BlockSpec((1, d), lambda i: (0, 0)), mod_spec, mod_spec]
    out_specs = [pl.BlockSpec((bm, d), lambda i: (i, 0))]
    out_shape = [jax.ShapeDtypeStruct((m, d), BF16)]
    args = [x, g.reshape(1, d), sc_seg, sh_seg]
    if w_router is not None:
        ne = w_router.shape[-1]
        in_specs.append(pl.BlockSpec((None, d, ne), lambda i: (router_layer, 0, 0)))
        out_specs.append(pl.BlockSpec((bm, ne), lambda i: (i, 0)))
        out_shape.append(jax.ShapeDtypeStruct((m, ne), F32))
        args.append(w_router)
    vmem = 8 * _nbytes((bm, d), F32) + 2 * _nbytes((d, 128), F32)
    out = pl.pallas_call(
        functools.partial(_modnorm_kernel, with_router=w_router is not None), name="modulated_norm",
        grid=(m // bm,),
        in_specs=in_specs, out_specs=out_specs, out_shape=out_shape,
        compiler_params=_params(("arbitrary",), vmem),
    )(*args)
    return out if w_router is not None else out[0]


ROPE_D = 64
HEAD_PAD = 128


def _rope_tables(t, head_dim, heads):
    half, n_freq = ROPE_D // 2, ROPE_D // 4
    rows = t // GRID_W
    r, col = jnp.meshgrid(jnp.arange(rows, dtype=F32), jnp.arange(GRID_W, dtype=F32), indexing='ij')
    inv = ROPE_THETA ** (-jnp.arange(n_freq, dtype=F32) / n_freq)
    ang = jnp.concatenate([r.reshape(-1, 1) * inv, col.reshape(-1, 1) * inv], axis=-1)
    plain = head_dim - ROPE_D
    cos = jnp.concatenate([jnp.ones((t, plain), F32), jnp.cos(ang), jnp.cos(ang)], axis=-1)
    sin = jnp.concatenate([jnp.zeros((t, plain), F32), jnp.sin(ang), jnp.sin(ang)], axis=-1)
    m_plus = np.concatenate([np.zeros(plain), -np.ones(half), np.zeros(half)]).astype(np.float32)
    m_minus = np.concatenate([np.zeros(plain), np.zeros(half), np.ones(half)]).astype(np.float32)
    tile = lambda z: jnp.tile(jnp.asarray(z), (1,) * (jnp.ndim(z) - 1) + (heads,))
    return dict(cos=tile(cos), sin=tile(sin), m_plus=tile(m_plus[None, :]), m_minus=tile(m_minus[None, :]))


def _headnorm_kernel(*refs, d_head, nope, use_rope, rope_blocks, out_f32):
    it = iter(refs)
    x_ref = next(it)
    extra_ref = next(it) if nope else None
    e_ref, et_ref, g_ref = next(it), next(it), next(it)
    if use_rope:
        cos_ref, sin_ref, mp_ref, mm_ref = next(it), next(it), next(it), next(it)
    obf_ref = next(it)
    x = x_ref[...]
    if nope:
        ex = extra_ref[...]
        x = jnp.concatenate([part for h in range(x.shape[1] // nope) for part in (x[:, h * nope:(h + 1) * nope], ex)],
                            axis=1)
    ss = jnp.dot(x * x, e_ref[...], precision=lax.Precision.HIGHEST, preferred_element_type=F32)
    scale = lax.rsqrt(ss / d_head + EPS)
    y = x * jnp.dot(scale, et_ref[...], precision=lax.Precision.HIGHEST, preferred_element_type=F32) * g_ref[...]
    if out_f32:
        next(it)[...] = y
    if not use_rope:
        obf_ref[...] = y.astype(obf_ref.dtype)
        return
    i = pl.program_id(0)
    is_rope = (i >= rope_blocks[0]) & (i < rope_blocks[1])

    @pl.when(is_rope)
    def _():
        bw = y.shape[1]
        partner = (pltpu.roll(y, bw - ROPE_D // 2, 1) * mp_ref[...] + pltpu.roll(y, ROPE_D // 2, 1) * mm_ref[...])
        obf_ref[...] = (y * cos_ref[...] + partner * sin_ref[...]).astype(obf_ref.dtype)

    @pl.when(jnp.logical_not(is_rope))
    def _():
        obf_ref[...] = y.astype(obf_ref.dtype)


def head_norm(x, col0, width, gains, d_head, *, bw=None, extra=None, rope=None, rope_rows=None, out_f32=False, bm=512):
    m = x.shape[0]
    nope = 0
    if extra is not None:
        e = extra.shape[1]
        nope = d_head - e
        heads = width // nope
        bw_in, bw = width, heads * d_head
        wout = bw
    else:
        bw = bw or width
        bw_in, wout = bw, width
    assert col0 % bw_in == 0 and wout % bw == 0 and bw % d_head == 0
    nh = bw // d_head
    seg = np.arange(bw) // d_head
    e_mat = (seg[:, None] == np.arange(HEAD_PAD)[None, :]).astype(np.float32)
    g = jnp.tile(gains.astype(F32).reshape(-1), wout // gains.size).reshape(1, wout)
    use_rope = rope is not None
    bm = _tile(functools.reduce(math.gcd, rope_rows, m) if use_rope else m, bm)
    in_specs = [pl.BlockSpec((bm, bw_in), lambda i, j: (i, col0 // bw_in + j))]
    args = [x]
    if extra is not None:
        in_specs.append(pl.BlockSpec((bm, extra.shape[1]), lambda i, j: (i, 0)))
        args.append(extra)
    in_specs += [pl.BlockSpec((bw, HEAD_PAD), lambda i, j: (0, 0)), pl.BlockSpec((HEAD_PAD, bw), lambda i, j: (0, 0)),
                 pl.BlockSpec((1, bw), lambda i, j: (0, j))]
    args += [jnp.asarray(e_mat), jnp.asarray(e_mat.T), g]
    rope_blocks = (0, 0)
    if use_rope:
        r0, r1, period = rope_rows
        assert r0 % bm == 0 and r1 % bm == 0 and period % bm == 0 and nh * d_head == bw
        rope_blocks = (r0 // bm, r1 // bm)
        nper, nrb = period // bm, (r1 - r0) // bm
        tspec = pl.BlockSpec((bm, bw), lambda i, j: (jnp.clip(i - r0 // bm, 0, nrb - 1) % nper, j))
        mspec = pl.BlockSpec((1, bw), lambda i, j: (0, j))
        in_specs += [tspec, tspec, mspec, mspec]
        args += [rope['cos'], rope['sin'], rope['m_plus'], rope['m_minus']]
    out_specs = [pl.BlockSpec((bm, bw), lambda i, j: (i, j))]
    out_shape = [jax.ShapeDtypeStruct((m, wout), BF16)]
    if out_f32:
        out_specs.append(pl.BlockSpec((bm, bw), lambda i, j: (i, j)))
        out_shape.append(jax.ShapeDtypeStruct((m, wout), F32))
    vmem = 14 * _nbytes((bm, bw), F32) + 4 * _nbytes((bw, HEAD_PAD), F32)
    out = pl.pallas_call(
        functools.partial(_headnorm_kernel, d_head=d_head, nope=nope, use_rope=use_rope, rope_blocks=rope_blocks,
                          out_f32=out_f32), name="head_norm",
        grid=(m // bm, wout // bw),
        in_specs=in_specs, out_specs=out_specs, out_shape=out_shape,
        compiler_params=_params(("arbitrary", "arbitrary"), vmem),
    )(*args)
    return out if out_f32 else out[0]


S5_CHUNK = 64
S5_HALF = 256


def _s5_tables(a_re, a_im, log_dt, b_re, b_im, c_re, c_im):
    ndir, g, n = a_re.shape
    ch = b_re.shape[-1]
    gh = S5_HALF // ch
    nh = g // gh
    dt = jnp.exp(log_dt)[..., None]
    er, ei = a_re * dt, a_im * dt
    mag = jnp.exp(er)
    lb_re, lb_im = mag * jnp.cos(ei), mag * jnp.sin(ei)
    den = a_re * a_re + a_im * a_im
    xr, xi = lb_re - 1.0, lb_im
    co_re, co_im = (xr * a_re + xi * a_im) / den, (xi * a_re - xr * a_im) / den
    bb_re = co_re[..., None] * b_re - co_im[..., None] * b_im
    bb_im = co_re[..., None] * b_im + co_im[..., None] * b_re
    eye = jnp.eye(gh, dtype=F32)

    def blockdiag_in(z):
        z = z.reshape(ndir, nh, gh, n, ch)
        return jnp.einsum('dhgnj,gk->dhgjkn', z, eye).reshape(ndir, nh, gh * ch, gh * n)

    def blockdiag_out(z):
        z = z.reshape(ndir, nh, gh, ch, n)
        return jnp.einsum('dhgin,gk->dhkngi', z, eye).reshape(ndir, nh, gh * n, gh * ch)

    bmat = jnp.concatenate([blockdiag_in(bb_re), blockdiag_in(bb_im)], axis=-1).astype(BF16)
    cmat = jnp.concatenate([blockdiag_out(c_re), -blockdiag_out(c_im)], axis=-2).astype(BF16)

    def halves(z):
        return z.reshape(z.shape[:-2] + (nh, gh * n))

    steps = jnp.arange(S5_CHUNK, dtype=F32)
    expo = jnp.stack([steps, S5_CHUNK - 1 - steps])[:, :, None, None]

    def powers(sign):
        pr, pi = sign * expo * er[:, None], sign * expo * ei[:, None]
        pm = jnp.exp(pr)
        tab = jnp.concatenate([halves(pm * jnp.cos(pi)), halves(pm * jnp.sin(pi))], axis=-1)
        return jnp.swapaxes(tab, 1, 2)

    lam = jnp.concatenate([halves(lb_re), halves(lb_im)], axis=-1)[:, :, None, :]
    ti = np.arange(S5_CHUNK)
    tri = jnp.asarray(np.stack([ti[:, None] >= ti[None, :], ti[:, None] <= ti[None, :]]), BF16)
    return dict(bmat=bmat, cmat=cmat, p=powers(1.0), pinv=powers(-1.0), lam=lam, tri=tri)


def _s5_kernel(u_ref, bm_ref, cm_ref, p_ref, pinv_ref, lam_ref, tri_ref, h0_ref, y_ref, hT_ref, hs_ref, st_ref,
               *, nb, nchunks):
    d, c = pl.program_id(0), pl.program_id(3)
    L = S5_CHUNK
    half = p_ref.shape[-1] // 2

    @pl.when(c == 0)
    def _():
        st_ref[...] = h0_ref[...]

    u = u_ref[...].reshape(nb * L, u_ref.shape[-1]).astype(BF16)
    bu = jnp.dot(u, bm_ref[...], preferred_element_type=F32)
    p_re, p_im = p_ref[:, :half], p_ref[:, half:]
    q_re, q_im = pinv_ref[:, :half], pinv_ref[:, half:]
    lam_re, lam_im = lam_ref[:, :half], lam_ref[:, half:]
    tri = tri_ref[...]
    for i in range(nb):
        b_re, b_im = bu[i * L:(i + 1) * L, :half], bu[i * L:(i + 1) * L, half:]
        z = jnp.concatenate([b_re * q_re - b_im * q_im, b_re * q_im + b_im * q_re], axis=1)
        z_hi = z.astype(BF16)
        z_lo = (z - z_hi.astype(F32)).astype(BF16)
        w = jnp.dot(tri, z_hi, preferred_element_type=F32) + jnp.dot(tri, z_lo, preferred_element_type=F32)
        s_re, s_im = st_ref[i:i + 1, :half], st_ref[i:i + 1, half:]
        w_re = w[:, :half] + (lam_re * s_re - lam_im * s_im)
        w_im = w[:, half:] + (lam_re * s_im + lam_im * s_re)
        h = jnp.concatenate([p_re * w_re - p_im * w_im, p_re * w_im + p_im * w_re], axis=1)
        hs_ref[i * L:(i + 1) * L, :] = h.astype(BF16)
        st_ref[i:i + 1, :] = jnp.where(d == 0, h[L - 1:L, :], h[0:1, :])
    y = jnp.dot(hs_ref[...], cm_ref[...], preferred_element_type=F32)
    y_ref[...] = y.reshape(y_ref.shape)

    @pl.when(c == nchunks - 1)
    def _():
        hT_ref[...] = st_ref[...]


def s5_scan(p3, col0, seq0, n_seq, tabs, h0):
    _, t, _ = p3.shape
    ndir, nh, hc, ns2 = tabs['bmat'].shape
    L = S5_CHUNK
    nb = _tile(n_seq, 8)
    assert t % L == 0 and seq0 % nb == 0 and col0 % hc == 0
    nchunks = t // L
    sg0, cb0 = seq0 // nb, col0 // hc

    def tchunk(d, c):
        return c + d * (nchunks - 1 - 2 * c)

    tab = lambda r, cc: pl.BlockSpec((None, None, r, cc), lambda d, hf, sg, c: (d, hf, 0, 0))
    vmem = (4 * _nbytes((nb * L, hc), F32) + 4 * _nbytes((hc, ns2), BF16) + 4 * _nbytes((L, ns2), F32)
            + 3 * _nbytes((nb * L, ns2), F32) + 8 * _nbytes((L, ns2), F32))
    return pl.pallas_call(
        functools.partial(_s5_kernel, nb=nb, nchunks=nchunks), name="s5_scan",
        grid=(ndir, nh, n_seq // nb, nchunks),
        in_specs=[pl.BlockSpec((nb, L, hc), lambda d, hf, sg, c: (sg0 + sg, tchunk(d, c), cb0 + hf)),
                  tab(hc, ns2), tab(ns2, hc), tab(L, ns2), tab(L, ns2), tab(1, ns2),
                  pl.BlockSpec((None, L, L), lambda d, hf, sg, c: (d, 0, 0)),
                  pl.BlockSpec((None, None, nb, ns2), lambda d, hf, sg, c: (d, hf, sg, 0))],
        out_specs=[pl.BlockSpec((None, nb, L, hc), lambda d, hf, sg, c: (d, sg, tchunk(d, c), hf)),
                   pl.BlockSpec((None, None, nb, ns2), lambda d, hf, sg, c: (d, hf, sg, 0))],
        out_shape=[jax.ShapeDtypeStruct((ndir, n_seq, t, nh * hc), F32),
                   jax.ShapeDtypeStruct((ndir, nh, n_seq, ns2), F32)],
        scratch_shapes=[pltpu.VMEM((nb * L, ns2), BF16), pltpu.VMEM((nb, ns2), F32)],
        compiler_params=_params(("arbitrary",) * 4, vmem),
    )(p3, tabs['bmat'], tabs['cmat'], tabs['p'], tabs['pinv'], tabs['lam'], tabs['tri'], h0)


def _ssm_post_kernel(y_ref, u_ref, d_ref, w_ref, o_ref, wbf_ref):
    @pl.when(pl.program_id(0) == 0)
    def _():
        wbf_ref[...] = w_ref[...].astype(BF16)
    y = y_ref[0] + y_ref[1] + d_ref[...] * u_ref[...]
    g = jax.nn.gelu(y)
    z = jnp.dot(g.astype(BF16), wbf_ref[...], preferred_element_type=F32)
    o_ref[...] = (g * jax.nn.sigmoid(z)).astype(o_ref.dtype)


def ssm_post(y2, p, col0, d_vec, w_glu, layer, *, bm=1024):
    _, m, w = y2.shape
    bm = _tile(m, bm)
    assert col0 % w == 0
    vmem = 6 * _nbytes((bm, w), F32) * 2 + 3 * _nbytes((w, w), F32)
    return pl.pallas_call(
        _ssm_post_kernel, name="ssm_post",
        grid=(m // bm,),
        in_specs=[pl.BlockSpec((2, bm, w), lambda i: (0, i, 0)),
                  pl.BlockSpec((bm, w), lambda i: (i, col0 // w)),
                  pl.BlockSpec((1, w), lambda i: (0, 0)),
                  pl.BlockSpec((None, w, w), lambda i: (layer, 0, 0))],
        out_specs=pl.BlockSpec((bm, w), lambda i: (i, 0)),
        out_shape=jax.ShapeDtypeStruct((m, w), BF16),
        scratch_shapes=[pltpu.VMEM((w, w), BF16)],
        compiler_params=_params(("arbitrary",), vmem),
    )(y2, p, d_vec.reshape(1, w), w_glu)


def _is_new_expert(be_ref, i):
    return (i == 0) | (be_ref[i] != be_ref[jnp.maximum(i - 1, 0)])


def _moe_glu_kernel(be_ref, nb_ref, x_ref, wg_ref, wu_ref, o_ref, wg_bf, wu_bf):
    i = pl.program_id(1)

    @pl.when(_is_new_expert(be_ref, i))
    def _():
        wg_bf[...] = wg_ref[...].astype(BF16)
        wu_bf[...] = wu_ref[...].astype(BF16)

    @pl.when(i < nb_ref[0])
    def _():
        x = x_ref[...]
        a = jnp.dot(x, wg_bf[...], preferred_element_type=F32)
        b = jnp.dot(x, wu_bf[...], preferred_element_type=F32)
        o_ref[...] = (a * jax.nn.sigmoid(a) * b).astype(o_ref.dtype)

    @pl.when(i >= nb_ref[0])
    def _():
        o_ref[...] = jnp.zeros_like(o_ref)


def moe_glu_up(x, wg, wu, layer, block_expert, nblocks, *, bm, bn=512):
    r, k = x.shape
    n = wg.shape[-1]
    assert r % bm == 0 and n % bn == 0
    wspec = pl.BlockSpec((None, None, k, bn), lambda j, i, be, nb: (layer, be[i], 0, j))
    vmem = (2 * _nbytes((bm, k), x.dtype) + 4 * _nbytes((k, bn), F32) + 2 * _nbytes((k, bn), BF16)
            + 2 * _nbytes((bm, bn), BF16) + 3 * _nbytes((bm, bn), F32))
    return pl.pallas_call(
        _moe_glu_kernel, name="moe_glu_up",
        grid_spec=pltpu.PrefetchScalarGridSpec(
            num_scalar_prefetch=2,
            grid=(n // bn, r // bm),
            in_specs=[pl.BlockSpec((bm, k), lambda j, i, be, nb: (i, 0)), wspec, wspec],
            out_specs=pl.BlockSpec((bm, bn), lambda j, i, be, nb: (i, j)),
            scratch_shapes=[pltpu.VMEM((k, bn), BF16), pltpu.VMEM((k, bn), BF16)]),
        out_shape=jax.ShapeDtypeStruct((r, n), BF16),
        compiler_params=_params(("arbitrary", "arbitrary"), vmem),
    )(block_expert, nblocks, x, wg, wu)


def _moe_down_kernel(be_ref, nb_ref, x_ref, w_ref, o_ref, w_bf):
    i = pl.program_id(1)

    @pl.when(_is_new_expert(be_ref, i))
    def _():
        w_bf[...] = w_ref[...].astype(BF16)

    @pl.when(i < nb_ref[0])
    def _():
        o_ref[...] = jnp.dot(x_ref[...], w_bf[...], preferred_element_type=F32)

    @pl.when(i >= nb_ref[0])
    def _():
        o_ref[...] = jnp.zeros_like(o_ref)


def moe_down(x, w, layer, block_expert, nblocks, *, bm, bn=512):
    r, k = x.shape
    n = w.shape[-1]
    assert r % bm == 0 and n % bn == 0
    vmem = (2 * _nbytes((bm, k), x.dtype) + 2 * _nbytes((k, bn), F32) + _nbytes((k, bn), BF16)
            + 2 * _nbytes((bm, bn), F32))
    return pl.pallas_call(
        _moe_down_kernel, name="moe_down",
        grid_spec=pltpu.PrefetchScalarGridSpec(
            num_scalar_prefetch=2,
            grid=(n // bn, r // bm),
            in_specs=[pl.BlockSpec((bm, k), lambda j, i, be, nb: (i, 0)),
                      pl.BlockSpec((None, None, k, bn), lambda j, i, be, nb: (layer, be[i], 0, j))],
            out_specs=pl.BlockSpec((bm, bn), lambda j, i, be, nb: (i, j)),
            scratch_shapes=[pltpu.VMEM((k, bn), BF16)]),
        out_shape=jax.ShapeDtypeStruct((r, n), F32),
        compiler_params=_params(("arbitrary", "arbitrary"), vmem),
    )(block_expert, nblocks, x, w)


def _qk(q, k):
    return lax.dot_general(q, k, (((1,), (1,)), ((), ())), preferred_element_type=F32)


def _row_sinks(sink_ref, first_head, rep, rows_per_head):
    r = lax.broadcasted_iota(jnp.int32, (rep * rows_per_head, 1), 0)
    sink = jnp.full((rep * rows_per_head, 1), sink_ref[first_head], F32)
    for g in range(1, rep):
        sink = jnp.where(r >= g * rows_per_head, sink_ref[first_head + g], sink)
    return sink


def _stack_heads(q_ref, heads, d):
    parts = [q_ref[:, h * d:(h + 1) * d] for h in heads]
    return parts[0] if len(parts) == 1 else jnp.concatenate(parts, axis=0)


def _attn_kernel(*refs, heads, kv_heads, dq, dv, scale, use_sink, use_ctx):
    refs = list(refs)
    sink_ref = refs.pop(0) if use_sink else None
    q_ref, k_ref, v_ref = refs[:3]
    kc_ref, vc_ref = (refs[3], refs[4]) if use_ctx else (None, None)
    o_ref = refs[-1]
    rep = heads // kv_heads
    bq = q_ref.shape[0]
    pieces = [None] * heads
    for j in range(kv_heads):
        q = _stack_heads(q_ref, range(j * rep, (j + 1) * rep), dq)
        s = _qk(q, k_ref[:, j * dq:(j + 1) * dq]) * scale
        m = jnp.max(s, axis=-1, keepdims=True)
        if use_ctx:
            s_c = _qk(q, kc_ref[:, j * dq:(j + 1) * dq].astype(BF16)) * scale
            m = jnp.maximum(m, jnp.max(s_c, axis=-1, keepdims=True))
        if use_sink:
            sink = _row_sinks(sink_ref, j * rep, rep, bq)
            m = jnp.maximum(m, sink)
        p = jnp.exp(s - m)
        l = jnp.sum(p, axis=-1, keepdims=True)
        o = jnp.dot(p.astype(BF16), v_ref[:, j * dv:(j + 1) * dv].astype(BF16), preferred_element_type=F32)
        if use_ctx:
            p_c = jnp.exp(s_c - m)
            l = l + jnp.sum(p_c, axis=-1, keepdims=True)
            o = o + jnp.dot(p_c.astype(BF16), vc_ref[:, j * dv:(j + 1) * dv].astype(BF16), preferred_element_type=F32)
        if use_sink:
            l = l + jnp.exp(sink - m)
        o = o * (1.0 / l)
        for g in range(rep):
            pieces[j * rep + g] = o[g * bq:(g + 1) * bq]
    o_ref[...] = jnp.concatenate(pieces, axis=1).astype(o_ref.dtype)


def attention(q, k, v, *, heads, kv_heads, scale, sink=None, k_ctx=None, v_ctx=None, bq=512):
    b, t, qw = q.shape
    s = k.shape[1]
    dq, dv = qw // heads, v.shape[2] // kv_heads
    rep = heads // kv_heads
    bq = _tile(t, bq)
    use_sink, use_ctx = sink is not None, k_ctx is not None
    in_specs = [pl.BlockSpec((None, bq, qw), lambda bi, qi: (bi, qi, 0)),
                pl.BlockSpec((None, s, k.shape[2]), lambda bi, qi: (bi, 0, 0)),
                pl.BlockSpec((None, s, v.shape[2]), lambda bi, qi: (bi, 0, 0))]
    args = [q, k, v]
    c = 0
    if use_ctx:
        c = k_ctx.shape[1]
        in_specs += [pl.BlockSpec((None, c, k_ctx.shape[2]), lambda bi, qi: (bi, 0, 0)),
                     pl.BlockSpec((None, c, v_ctx.shape[2]), lambda bi, qi: (bi, 0, 0))]
        args += [k_ctx, v_ctx]
    if use_sink:
        in_specs = [pl.BlockSpec(memory_space=pltpu.SMEM)] + in_specs
        args = [sink.astype(F32)] + args
    vmem = (5 * _nbytes((rep * bq, s + c), F32) + 2 * _nbytes((bq, qw), BF16) + 2 * _nbytes((s, k.shape[2] + v.shape[2]), F32)
            + 3 * _nbytes((c, k.shape[2] + v.shape[2]), F32) + 6 * _nbytes((bq, heads * dv), F32))
    return pl.pallas_call(
        functools.partial(_attn_kernel, heads=heads, kv_heads=kv_heads, dq=dq, dv=dv, scale=scale,
                          use_sink=use_sink, use_ctx=use_ctx), name="attention",
        grid=(b, t // bq),
        in_specs=in_specs,
        out_specs=pl.BlockSpec((None, bq, heads * dv), lambda bi, qi: (bi, qi, 0)),
        out_shape=jax.ShapeDtypeStruct((b, t, heads * dv), BF16),
        compiler_params=_params(("arbitrary",) * 2, vmem),
    )(*args)


def _diff_attn_kernel(*refs, heads, scale, post_scale, use_ctx):
    lam_ref, q_ref, k_ref, v_ref = refs[:4]
    kc_ref, vc_ref = (refs[4], refs[5]) if use_ctx else (None, None)
    g_ref, o_ref = refs[-2:]
    d, dv = DIFF_HD, DIFF_VD

    def probs(c0):
        q = q_ref[:, c0:c0 + d]
        s = _qk(q, k_ref[:, c0:c0 + d]) * scale
        m = jnp.max(s, axis=-1, keepdims=True)
        if use_ctx:
            s_c = _qk(q, kc_ref[:, c0:c0 + d].astype(BF16)) * scale
            m = jnp.maximum(m, jnp.max(s_c, axis=-1, keepdims=True))
        p = jnp.exp(s - m)
        l = jnp.sum(p, axis=-1, keepdims=True)
        if not use_ctx:
            return p * (1.0 / l), None
        p_c = jnp.exp(s_c - m)
        inv = 1.0 / (l + jnp.sum(p_c, axis=-1, keepdims=True))
        return p * inv, p_c * inv

    pieces = []
    for h in range(heads):
        p0, pc0 = probs(2 * h * d)
        p1, pc1 = probs((2 * h + 1) * d)
        lam = lam_ref[h]
        o = jnp.dot((p0 - lam * p1).astype(BF16), v_ref[:, h * dv:(h + 1) * dv].astype(BF16),
                    preferred_element_type=F32)
        if use_ctx:
            o = o + jnp.dot((pc0 - lam * pc1).astype(BF16), vc_ref[:, h * dv:(h + 1) * dv].astype(BF16),
                            preferred_element_type=F32)
        o = o * lax.rsqrt(jnp.mean(o * o, axis=-1, keepdims=True) + EPS) * g_ref[...]
        pieces.append(o * post_scale)
    o_ref[...] = jnp.concatenate(pieces, axis=1).astype(o_ref.dtype)


def diff_attention(q, k, v, lam, g_out, lambda_init, *, k_ctx=None, v_ctx=None, bq=512):
    b, t, qw = q.shape
    s, vw = k.shape[1], v.shape[2]
    heads = vw // DIFF_VD
    bq = _tile(t, bq)
    use_ctx = k_ctx is not None
    in_specs = [pl.BlockSpec(memory_space=pltpu.SMEM),
                pl.BlockSpec((None, bq, qw), lambda bi, qi: (bi, qi, 0)),
                pl.BlockSpec((None, s, qw), lambda bi, qi: (bi, 0, 0)),
                pl.BlockSpec((None, s, vw), lambda bi, qi: (bi, 0, 0))]
    args = [lam.astype(F32), q, k, v]
    c = 0
    if use_ctx:
        c = k_ctx.shape[1]
        in_specs += [pl.BlockSpec((None, c, qw), lambda bi, qi: (bi, 0, 0)),
                     pl.BlockSpec((None, c, vw), lambda bi, qi: (bi, 0, 0))]
        args += [k_ctx, v_ctx]
    in_specs.append(pl.BlockSpec((1, DIFF_VD), lambda bi, qi: (0, 0)))
    args.append(g_out.astype(F32).reshape(1, DIFF_VD))
    vmem = (8 * _nbytes((bq, s + c), F32) + 2 * _nbytes((bq, qw), BF16) + 2 * _nbytes((s, qw + vw), F32)
            + 3 * _nbytes((c, qw + vw), F32) + 6 * _nbytes((bq, vw), F32))
    return pl.pallas_call(
        functools.partial(_diff_attn_kernel, heads=heads, scale=DIFF_HD ** -0.5, post_scale=1.0 - lambda_init,
                          use_ctx=use_ctx), name="diff_attention",
        grid=(b, t // bq),
        in_specs=in_specs,
        out_specs=pl.BlockSpec((None, bq, vw), lambda bi, qi: (bi, qi, 0)),
        out_shape=jax.ShapeDtypeStruct((b, t, vw), BF16),
        compiler_params=_params(("arbitrary",) * 2, vmem),
    )(*args)


def _swa_lat_kernel(sink_ref, q_ref, kc_ref, vc_ref, k0_ref, k1_ref, k2_ref, v0_ref, v1_ref, v2_ref, o_ref,
                    *, scale, nblocks):
    n = pl.program_id(1)
    rows = SWA_REP * BLOCK
    r = lax.broadcasted_iota(jnp.int32, (rows, 3 * BLOCK), 0) & (BLOCK - 1)
    c = lax.broadcasted_iota(jnp.int32, (rows, 3 * BLOCK), 1)
    kpos = (n - 1) * BLOCK + c
    mask = (c - r >= BLOCK - WINDOW) & (c - r <= BLOCK + WINDOW) & (kpos >= 0) & (kpos < nblocks * BLOCK)
    pieces = [None] * SWA_HEADS
    for j in range(SWA_KV_HEADS):
        cs = slice(j * SWA_HD, (j + 1) * SWA_HD)
        q = _stack_heads(q_ref, range(j * SWA_REP, (j + 1) * SWA_REP), SWA_HD)
        kw = jnp.concatenate([k0_ref[:, cs], k1_ref[:, cs], k2_ref[:, cs]], axis=0)
        vw = jnp.concatenate([v0_ref[:, cs], v1_ref[:, cs], v2_ref[:, cs]], axis=0).astype(BF16)
        s_ctx = _qk(q, kc_ref[:, cs].astype(BF16)) * scale
        s_loc = jnp.where(mask, _qk(q, kw) * scale, NEG_INF)
        sink = _row_sinks(sink_ref, j * SWA_REP, SWA_REP, BLOCK)
        m = jnp.maximum(jnp.maximum(jnp.max(s_ctx, axis=-1, keepdims=True), jnp.max(s_loc, axis=-1, keepdims=True)), sink)
        p_ctx = jnp.exp(s_ctx - m)
        p_loc = jnp.exp(s_loc - m)
        l = jnp.sum(p_ctx, axis=-1, keepdims=True) + jnp.sum(p_loc, axis=-1, keepdims=True) + jnp.exp(sink - m)
        o = (jnp.dot(p_ctx.astype(BF16), vc_ref[:, cs].astype(BF16), preferred_element_type=F32)
             + jnp.dot(p_loc.astype(BF16), vw, preferred_element_type=F32)) * (1.0 / l)
        for g in range(SWA_REP):
            pieces[j * SWA_REP + g] = o[g * BLOCK:(g + 1) * BLOCK]
    o_ref[...] = jnp.concatenate(pieces, axis=1).astype(o_ref.dtype)


def swa_latent_attention(q, k, v, k_ctx, v_ctx, sink):
    b, t, qw = q.shape
    kw = k.shape[2]
    c = k_ctx.shape[1]
    nb = t // BLOCK
    pad = ((0, 0), (BLOCK, BLOCK), (0, 0))
    kp, vp = jnp.pad(k, pad), jnp.pad(v, pad)
    band = [pl.BlockSpec((None, BLOCK, kw), functools.partial(lambda bi, ni, o: (bi, ni + o, 0), o=o)) for o in range(3)]
    ctx = pl.BlockSpec((None, c, kw), lambda bi, ni: (bi, 0, 0))
    qspec = pl.BlockSpec((None, BLOCK, qw), lambda bi, ni: (bi, ni, 0))
    vmem = 8 * _nbytes((SWA_REP * BLOCK, c + 3 * BLOCK), F32) + 6 * _nbytes((c, kw), F32) + (4 << 20)
    return pl.pallas_call(
        functools.partial(_swa_lat_kernel, scale=SWA_HD ** -0.5, nblocks=nb), name="swa_latent_attention",
        grid=(b, nb),
        in_specs=[pl.BlockSpec(memory_space=pltpu.SMEM), qspec, ctx, ctx] + band + band,
        out_specs=qspec,
        out_shape=jax.ShapeDtypeStruct(q.shape, BF16),
        compiler_params=_params(("arbitrary",) * 2, vmem),
    )(sink.astype(F32), q, k_ctx, v_ctx, kp, kp, kp, vp, vp, vp)


def _state_to_halves(st):
    b, nd, _, g, n = st.shape
    gh = S5_HALF // SSM_CH
    z = st.reshape(b, nd, 2, g // gh, gh * n)
    return jnp.transpose(z, (1, 3, 0, 2, 4)).reshape(nd, g // gh, b, 2 * gh * n)


def _halves_to_state(h, groups):
    nd, nh, b, _ = h.shape
    z = h.reshape(nd, nh, b, 2, -1)
    return jnp.transpose(z, (2, 0, 3, 1, 4)).reshape(b, nd, 2, groups, SSM_N)


def _moe_ffn(logits, h_bf, P, j, bm):
    m = h_bf.shape[0]
    probs = jax.nn.softmax(logits, axis=-1)
    top_p, top_i = lax.top_k(probs, TOP_K)
    top_p = top_p / jnp.sum(top_p, axis=-1, keepdims=True)
    flat_e = top_i.reshape(-1).astype(jnp.int32)
    npairs = flat_e.shape[0]
    order = jnp.argsort(flat_e, stable=True).astype(jnp.int32)
    counts = jnp.sum(flat_e[:, None] == jnp.arange(N_EXPERTS, dtype=jnp.int32)[None, :], axis=0).astype(jnp.int32)
    blocks_per = (counts + bm - 1) // bm
    block_end = jnp.cumsum(blocks_per)
    pad_start = (block_end - blocks_per) * bm
    start = jnp.cumsum(counts) - counts
    sorted_e = flat_e[order]
    dest = pad_start[sorted_e] + jnp.arange(npairs, dtype=jnp.int32) - start[sorted_e]
    nrows = npairs + N_EXPERTS * bm
    pos = lax.sort_key_val(order, dest)[1].reshape(m, TOP_K)
    nblk = nrows // bm
    nused = block_end[-1:]
    block_expert = jnp.minimum(jnp.sum(jnp.arange(nblk, dtype=jnp.int32)[:, None] >= block_end[None, :], axis=1),
                               N_EXPERTS - 1).astype(jnp.int32)
    row = jnp.arange(nrows, dtype=jnp.int32)
    row_e = block_expert[row // bm]
    off = row - pad_start[row_e]
    rank = jnp.clip(start[row_e] + off, 0, npairs - 1)
    row_token = jnp.where(off < counts[row_e], order[rank] // TOP_K, 0)
    block_expert = jnp.where(jnp.arange(nblk) < nused[0], block_expert, block_expert[jnp.maximum(nused[0] - 1, 0)])
    x_sorted = jnp.take(h_bf, row_token, axis=0)
    act = moe_glu_up(x_sorted, P['w_moe_gate'], P['w_moe_up'], j, block_expert, nused, bm=bm, bn=1024)
    y_sorted = moe_down(act, P['w_moe_down'], j, block_expert, nused, bm=bm)
    out = jnp.zeros((m, y_sorted.shape[1]), F32)
    for kk in range(TOP_K):
        out = out + top_p[:, kk:kk + 1] * jnp.take(y_sorted, pos[:, kk], axis=0)
    return out


SWA_COLS = (SWA_HEADS + 2 * SWA_KV_HEADS) * SWA_HD
DIFF_COLS = 2 * DIFF_HEADS * 2 * DIFF_HD + DIFF_HEADS * DIFF_VD


def kernel(x_prompt, x_sample, cache_mla_ckv, cache_mla_krope, state_ssm, cache_swa_k, cache_swa_v, cache_diff_k, cache_diff_v, c, c_ctx, w_ada, b_ada, g_norm1, g_norm2, w_in, g_mla_qlat, g_mla_kvlat, w_mla_uq, w_mla_ukv, g_mla_q, g_mla_k, ssm_a_re, ssm_a_im, ssm_log_dt, ssm_b_re, ssm_b_im, ssm_c_re, ssm_c_im, ssm_d, w_ssm_glu, g_swa_q, g_swa_k, swa_sink, g_diff_q, g_diff_k, diff_lambda, g_diff_out, w_branch, w_out, w_ffn_gate, w_ffn_up, w_ffn_down, w_router, w_moe_gate, w_moe_up, w_moe_down):
    P = dict(ssm_a_re=ssm_a_re, ssm_a_im=ssm_a_im, ssm_log_dt=ssm_log_dt, ssm_b_re=ssm_b_re, ssm_b_im=ssm_b_im,
             ssm_c_re=ssm_c_re, ssm_c_im=ssm_c_im, ssm_d=ssm_d, w_router=w_router, w_moe_gate=w_moe_gate,
             w_moe_up=w_moe_up, w_moe_down=w_moe_down)
    bp, tp, d = x_prompt.shape
    bs, ts, _ = x_sample.shape
    depth = w_in.shape[0]
    past = cache_mla_ckv.shape[2]
    mp, ms = bp * tp, bs * ts
    m = mp + ms
    w = d // 4
    groups = w // SSM_CH
    x = jnp.concatenate([x_prompt.reshape(mp, d), x_sample.reshape(ms, d)], axis=0)

    seg = math.gcd(mp, ts)
    nseg = m // seg
    seg_mod = np.concatenate([np.zeros(mp // seg, np.int32), 1 + np.repeat(np.arange(bs, dtype=np.int32), ts // seg)])
    cvec = jnp.concatenate([c_ctx[None, :], c, jnp.zeros((8 - 1 - bs, d), F32)], axis=0)
    cvec = cvec * jax.nn.sigmoid(cvec)

    def per_seg(v):
        return v[seg_mod][:, None, :]

    def gated_add(xx, gt_seg, upd):
        return (xx.reshape(nseg, seg, d) + gt_seg * upd.reshape(nseg, seg, d)).reshape(m, d)

    rope_mla = _rope_tables(ts, MLA_QK, MLA_HEADS)
    rope_swa_q = rope_diff = _rope_tables(ts, SWA_HD, SWA_HEADS)
    rope_swa_k = _rope_tables(ts, SWA_HD, SWA_KV_HEADS)
    assert DIFF_HD == SWA_HD and 2 * DIFF_HEADS == SWA_HEADS

    outs = [[] for _ in range(7)]
    for l in range(depth):
        lambda_init = 0.8 - 0.6 * math.exp(-0.3 * l)
        mods = matmul(cvec, w_ada, (l,), bm=8, bn=1024) + b_ada[l][None, :]
        sh1, sc1, gt1, sh2, sc2, gt2 = (per_seg(v) for v in jnp.split(mods, 6, axis=-1))

        h = modulated_norm(x, g_norm1[l], sc1, sh1)
        p, col0 = in_projection(h, w_in, l)
        k_rope = matmul(h, w_in, (l,), bn=IN_SHIFT_UNIT, cols=(2 * w // IN_SHIFT_UNIT, 1))[:, :MLA_ROPE]

        def tm(z, bb, tt):
            return z.reshape(bb, tt, z.shape[-1])

        lat_rows = (mp, m, ts)

        ckv_bf, c_kv = head_norm(p, col0['ckv'], w, g_mla_kvlat[l], w, out_f32=True)
        q_lat = head_norm(p, col0['q'], w, g_mla_qlat[l], w)
        q_m = head_norm(matmul(q_lat, w_mla_uq, (l,)), 0, MLA_HEADS * MLA_QK, g_mla_q[l], MLA_QK,
                        rope=rope_mla, rope_rows=lat_rows)
        ckv_all = jnp.concatenate([ckv_bf, cache_mla_ckv[:, l].reshape(bs * past, w).astype(BF16)], axis=0)
        krope_all = jnp.concatenate([k_rope, cache_mla_krope[:, l].reshape(bs * past, MLA_ROPE)], axis=0)
        w_ukv = jnp.swapaxes(w_mla_ukv[l].reshape(w, MLA_HEADS, 2, MLA_NOPE), 1, 2).reshape(w, -1)
        kv = matmul(ckv_all, w_ukv, bm=512)
        k_m = head_norm(kv, 0, MLA_HEADS * MLA_NOPE, g_mla_k[l], MLA_QK, extra=krope_all,
                        rope=rope_mla, rope_rows=lat_rows)
        v_m = kv[:, MLA_HEADS * MLA_NOPE:]
        mla = dict(heads=MLA_HEADS, kv_heads=MLA_HEADS, scale=MLA_QK ** -0.5)
        o_mla_p = attention(tm(q_m[:mp], bp, tp), tm(k_m[:mp], bp, tp), tm(v_m[:mp], bp, tp), **mla)
        o_mla_s = attention(tm(q_m[mp:], bs, ts), tm(k_m[mp:m], bs, ts), tm(v_m[mp:m], bs, ts),
                            k_ctx=tm(k_m[m:], bs, past), v_ctx=tm(v_m[m:], bs, past), **mla)
        o_mla = jnp.concatenate([o_mla_p.reshape(mp, w), o_mla_s.reshape(ms, w)], axis=0)

        tabs = _s5_tables(ssm_a_re[l], ssm_a_im[l], ssm_log_dt[l], ssm_b_re[l], ssm_b_im[l], ssm_c_re[l], ssm_c_im[l])
        ncol = p.shape[1]
        h0_p = jnp.zeros((2, w // S5_HALF, bp, 2 * (S5_HALF // SSM_CH) * SSM_N), F32)
        y_p, hT_p = s5_scan(p.reshape(m // tp, tp, ncol), col0['ssm'], 0, bp, tabs, h0_p)
        y_s, _ = s5_scan(p.reshape(m // ts, ts, ncol), col0['ssm'], mp // ts, bs, tabs, _state_to_halves(state_ssm[:, l]))
        y2 = jnp.concatenate([y_p.reshape(2, mp, w), y_s.reshape(2, ms, w)], axis=1)
        o_ssm = ssm_post(y2, p, col0['ssm'], ssm_d[l], w_ssm_glu, l)
        ssm_state = _halves_to_state(hT_p, groups)

        nq = SWA_HEADS * SWA_HD
        nk = SWA_KV_HEADS * SWA_HD
        c_swa = col0['swa']
        q_s = head_norm(p, c_swa, nq, g_swa_q[l], SWA_HD, bw=256, rope=rope_swa_q, rope_rows=lat_rows)
        k_s, k_s32 = head_norm(p, c_swa + nq, nk, g_swa_k[l], SWA_HD, rope=rope_swa_k, rope_rows=lat_rows, out_f32=True)
        v_s = p[:, c_swa + nq + nk:c_swa + nq + 2 * nk]
        o_swa_p = attention(tm(q_s[:mp], bp, tp), tm(k_s[:mp], bp, tp), tm(v_s[:mp], bp, tp), heads=SWA_HEADS,
                            kv_heads=SWA_KV_HEADS, scale=SWA_HD ** -0.5, sink=swa_sink[l], bq=256)
        o_swa_s = swa_latent_attention(tm(q_s[mp:], bs, ts), tm(k_s[mp:], bs, ts), tm(v_s[mp:], bs, ts),
                                       cache_swa_k[:, l].reshape(bs, past, nk), cache_swa_v[:, l].reshape(bs, past, nk),
                                       swa_sink[l])
        o_swa = jnp.concatenate([o_swa_p.reshape(mp, w), o_swa_s.reshape(ms, w)], axis=0)

        ndq = DIFF_HEADS * 2 * DIFF_HD
        c_diff = col0['diff']
        q_d = head_norm(p, c_diff, ndq, g_diff_q[l], DIFF_HD, bw=256, rope=rope_diff, rope_rows=lat_rows)
        k_d, k_d32 = head_norm(p, c_diff + ndq, ndq, g_diff_k[l], DIFF_HD, bw=256, rope=rope_diff, rope_rows=lat_rows,
                               out_f32=True)
        v_d = p[:, c_diff + 2 * ndq:c_diff + 2 * ndq + DIFF_HEADS * DIFF_VD]
        lp = diff_lambda[l]
        lam = jnp.exp(jnp.sum(lp[0] * lp[1], -1)) - jnp.exp(jnp.sum(lp[2] * lp[3], -1)) + lambda_init
        o_diff_p = diff_attention(tm(q_d[:mp], bp, tp), tm(k_d[:mp], bp, tp), tm(v_d[:mp], bp, tp),
                                  lam, g_diff_out[l], lambda_init, bq=256)
        o_diff_s = diff_attention(tm(q_d[mp:], bs, ts), tm(k_d[mp:], bs, ts), tm(v_d[mp:], bs, ts), lam, g_diff_out[l],
                                  lambda_init, k_ctx=cache_diff_k[:, l].reshape(bs, past, ndq),
                                  v_ctx=cache_diff_v[:, l].reshape(bs, past, DIFF_HEADS * DIFF_VD), bq=256)
        o_diff = jnp.concatenate([o_diff_p.reshape(mp, w), o_diff_s.reshape(ms, w)], axis=0)

        merged = merge_branches((o_mla, o_ssm, o_swa, o_diff), p, col0['gate'], w_branch, l, bm=1024, bn=256)
        x = matmul_residual(merged, w_out, (l,), x, gt1)

        j = l // 2
        if l % 2 == 0:
            h2 = modulated_norm(x, g_norm2[l], sc2, sh2)
            act = glu_up(h2, w_ffn_gate, w_ffn_up, (j,))
            x = matmul_residual(act, w_ffn_down, (j,), x, gt2)
        else:
            h2, logits = modulated_norm(x, g_norm2[l], sc2, sh2, w_router, j)
            x = gated_add(x, gt2, _moe_ffn(logits, h2, P, j, bm=256))

        outs[0].append(c_kv[:mp].reshape(bp, tp, w))
        outs[1].append(k_rope[:mp].reshape(bp, tp, MLA_ROPE))
        outs[2].append(ssm_state)
        outs[3].append(k_s32[:mp].reshape(bp, tp, SWA_KV_HEADS, SWA_HD))
        outs[4].append(v_s[:mp].reshape(bp, tp, SWA_KV_HEADS, SWA_HD))
        outs[5].append(k_d32[:mp].reshape(bp, tp, DIFF_HEADS, 2, DIFF_HD))
        outs[6].append(v_d[:mp].reshape(bp, tp, DIFF_HEADS, DIFF_VD))

    return (x[:mp].reshape(bp, tp, d), x[mp:].reshape(bs, ts, d)) + tuple(jnp.stack(o, axis=1) for o in outs)
```

```python
import functools
import math

import numpy as np
import jax
import jax.numpy as jnp
from jax import lax
from jax.experimental import pallas as pl
from jax.experimental.pallas import tpu as pltpu

F32 = jnp.float32
BF16 = jnp.bfloat16
EPS = 1e-6
NEG_INF = -1e30
ROPE_THETA = 10000.0
GRID_W = 64
BLOCK = 128
N_BRANCH = 4
MLA_HEADS = 4
MLA_NOPE = 128
MLA_ROPE = 64
MLA_QK = MLA_NOPE + MLA_ROPE
MLA_V = 128
SSM_CH = 16
SSM_N = 64
SWA_HEADS = 8
SWA_KV_HEADS = 2
SWA_REP = SWA_HEADS // SWA_KV_HEADS
SWA_HD = 64
WINDOW = 128
DIFF_HEADS = 4
DIFF_HD = 64
DIFF_VD = 2 * DIFF_HD
N_EXPERTS = 8
TOP_K = 2

V7X_VMEM_REQUEST_CAP = 56 * 1024 * 1024


def _params(semantics, vmem_bytes):
    limit = int(min(max(vmem_bytes * 5 // 4 + (2 << 20), 16 << 20), V7X_VMEM_REQUEST_CAP))
    return pltpu.CompilerParams(dimension_semantics=semantics, vmem_limit_bytes=limit)


def _tile(n, want):
    t = min(n, want)
    while n % t:
        t -= 1
    return t


def _nbytes(shape, dtype):
    return int(np.prod(shape)) * jnp.dtype(dtype).itemsize


def _mm_kernel(x_ref, w_ref, o_ref, *scratch):
    if scratch:
        wbf_ref, = scratch

        @pl.when(pl.program_id(1) == 0)
        def _():
            wbf_ref[...] = w_ref[...].astype(BF16)
        w = wbf_ref[...]
    else:
        w = w_ref[...]
    o_ref[...] = jnp.dot(x_ref[...].astype(BF16), w, preferred_element_type=F32).astype(o_ref.dtype)


def matmul(x, w, lead=(), *, bm=1024, bn=512, out_dtype=F32, cols=None):
    m, k = x.shape
    assert w.shape[-2] == k
    cb0 = 0
    if cols is None:
        n = w.shape[-1]
        bn = _tile(n, bn)
    else:
        cb0, n = cols[0], cols[1] * bn
    bm = _tile(m, bm)
    assert m % bm == 0 and n % bn == 0, (m, n, bm, bn)
    lead = tuple(lead)
    nl = len(lead)
    cast = w.dtype != BF16
    scratch = [pltpu.VMEM((k, bn), BF16)] if cast else []
    vmem = (2 * _nbytes((bm, k), x.dtype) + 2 * _nbytes((k, bn), w.dtype)
            + 2 * _nbytes((bm, bn), out_dtype) + (_nbytes((k, bn), BF16) if cast else 0))
    return pl.pallas_call(
        _mm_kernel, name="matmul",
        grid=(n // bn, m // bm),
        in_specs=[pl.BlockSpec((bm, k), lambda j, i: (i, 0)),
                  pl.BlockSpec((None,) * nl + (k, bn), lambda j, i: lead + (0, cb0 + j))],
        out_specs=pl.BlockSpec((bm, bn), lambda j, i: (i, j)),
        out_shape=jax.ShapeDtypeStruct((m, n), out_dtype),
        scratch_shapes=scratch,
        compiler_params=_params(("arbitrary", "arbitrary"), vmem),
    )(x, w)


IN_TILE = 1024
IN_SHIFT_UNIT = 128
IN_CAST_ROWS = 512


def _inproj_kernel(a_idx, b_idx, shift_ref, x_ref, wa_ref, wb_ref, o_ref, wbf_ref, *, shift):
    j = pl.program_id(0)
    first = pl.program_id(1) == 0

    @pl.when(first & (shift_ref[j] == 0))
    def _():
        wbf_ref[...] = wa_ref[...].astype(BF16)

    @pl.when(first & (shift_ref[j] != 0))
    def _():
        for r in range(0, wa_ref.shape[0], IN_CAST_ROWS):
            rows = slice(r, r + IN_CAST_ROWS)
            wcat = jnp.concatenate([wa_ref[rows, :], wb_ref[rows, :]], axis=1)
            wbf_ref[rows, :] = wcat[:, shift:shift + IN_TILE].astype(BF16)

    o_ref[...] = jnp.dot(x_ref[...], wbf_ref[...], preferred_element_type=F32)


def in_projection(h, w_in, layer, *, bm=1024):
    m, k = h.shape
    ntot = w_in.shape[-1]
    w = k // 4
    lead = 2 * w
    rest = ntot - lead - MLA_ROPE
    assert lead % IN_TILE == 0 and MLA_ROPE < IN_SHIFT_UNIT and IN_TILE % IN_SHIFT_UNIT == 0 and k % IN_CAST_ROWS == 0
    n_lead, n_rest = lead // IN_TILE, -(-rest // IN_TILE)
    last_unit = (ntot - 1) // IN_SHIFT_UNIT
    a_idx = list(range(n_lead)) + [n_lead + t for t in range(n_rest)]
    b_idx = [0] * n_lead + [min((lead + IN_TILE * (t + 1)) // IN_SHIFT_UNIT, last_unit) for t in range(n_rest)]
    shifted = [0] * n_lead + [1] * n_rest
    ntiles = len(a_idx)
    col0 = dict(q=0, ckv=w, ssm=lead, swa=lead + w, diff=lead + w + SWA_COLS, gate=lead + w + SWA_COLS + DIFF_COLS)
    bm = _tile(m, bm)
    vmem = (2 * _nbytes((bm, k), BF16) + 2 * _nbytes((k, IN_TILE + IN_SHIFT_UNIT), F32) + _nbytes((k, IN_TILE), BF16)
            + 3 * _nbytes((IN_CAST_ROWS, IN_TILE + IN_SHIFT_UNIT), F32) + 2 * _nbytes((bm, IN_TILE), F32))
    p = pl.pallas_call(
        functools.partial(_inproj_kernel, shift=MLA_ROPE), name="in_projection",
        grid_spec=pltpu.PrefetchScalarGridSpec(
            num_scalar_prefetch=3,
            grid=(ntiles, m // bm),
            in_specs=[pl.BlockSpec((bm, k), lambda j, i, a, b, s: (i, 0)),
                      pl.BlockSpec((None, k, IN_TILE), lambda j, i, a, b, s: (layer, 0, a[j])),
                      pl.BlockSpec((None, k, IN_SHIFT_UNIT), lambda j, i, a, b, s: (layer, 0, b[j]))],
            out_specs=pl.BlockSpec((bm, IN_TILE), lambda j, i, a, b, s: (i, j)),
            scratch_shapes=[pltpu.VMEM((k, IN_TILE), BF16)]),
        out_shape=jax.ShapeDtypeStruct((m, ntiles * IN_TILE), F32),
        compiler_params=_params(("arbitrary", "arbitrary"), vmem),
    )(jnp.asarray(a_idx, jnp.int32), jnp.asarray(b_idx, jnp.int32), jnp.asarray(shifted, jnp.int32), h, w_in, w_in)
    return p, col0


def _glu_kernel(x_ref, wg_ref, wu_ref, o_ref, wg_bf, wu_bf):
    @pl.when(pl.program_id(1) == 0)
    def _():
        wg_bf[...] = wg_ref[...].astype(BF16)
        wu_bf[...] = wu_ref[...].astype(BF16)
    x = x_ref[...]
    a = jnp.dot(x, wg_bf[...], preferred_element_type=F32)
    b = jnp.dot(x, wu_bf[...], preferred_element_type=F32)
    o_ref[...] = (a * jax.nn.sigmoid(a) * b).astype(o_ref.dtype)


def glu_up(x, wg, wu, lead, *, bm=1024, bn=512):
    m, k = x.shape
    n = wg.shape[-1]
    bm, bn = _tile(m, bm), _tile(n, bn)
    assert m % bm == 0 and n % bn == 0
    nl = len(lead)
    wspec = pl.BlockSpec((None,) * nl + (k, bn), lambda j, i: lead + (0, j))
    vmem = (2 * _nbytes((bm, k), x.dtype) + 4 * _nbytes((k, bn), F32) + 2 * _nbytes((k, bn), BF16)
            + 2 * _nbytes((bm, bn), BF16) + 3 * _nbytes((bm, bn), F32))
    return pl.pallas_call(
        _glu_kernel, name="glu_up",
        grid=(n // bn, m // bm),
        in_specs=[pl.BlockSpec((bm, k), lambda j, i: (i, 0)), wspec, wspec],
        out_specs=pl.BlockSpec((bm, bn), lambda j, i: (i, j)),
        out_shape=jax.ShapeDtypeStruct((m, n), BF16),
        scratch_shapes=[pltpu.VMEM((k, bn), BF16), pltpu.VMEM((k, bn), BF16)],
        compiler_params=_params(("arbitrary", "arbitrary"), vmem),
    )(x, wg, wu)


def _mm_res_kernel(a_ref, w_ref, x_ref, gt_ref, o_ref, wbf_ref):
    @pl.when(pl.program_id(1) == 0)
    def _():
        wbf_ref[...] = w_ref[...].astype(BF16)
    upd = jnp.dot(a_ref[...], wbf_ref[...], preferred_element_type=F32)
    o_ref[...] = x_ref[...] + gt_ref[...] * upd


def matmul_residual(a, w, lead, x, gt_seg, *, bm=512, bn=512):
    m, k = a.shape
    n = w.shape[-1]
    seg = m // gt_seg.shape[0]
    bm, bn = _tile(seg, bm), _tile(n, bn)
    nl = len(lead)
    vmem = (2 * _nbytes((bm, k), a.dtype) + 2 * _nbytes((k, bn), F32) + _nbytes((k, bn), BF16)
            + 5 * _nbytes((bm, bn), F32))
    return pl.pallas_call(
        _mm_res_kernel, name="matmul_residual",
        grid=(n // bn, m // bm),
        in_specs=[pl.BlockSpec((bm, k), lambda j, i: (i, 0)),
                  pl.BlockSpec((None,) * nl + (k, bn), lambda j, i: lead + (0, j)),
                  pl.BlockSpec((bm, bn), lambda j, i: (i, j)),
                  pl.BlockSpec((None, 1, bn), lambda j, i: (i * bm // seg, 0, j))],
        out_specs=pl.BlockSpec((bm, bn), lambda j, i: (i, j)),
        out_shape=jax.ShapeDtypeStruct((m, n), F32),
        scratch_shapes=[pltpu.VMEM((k, bn), BF16)],
        compiler_params=_params(("arbitrary", "arbitrary"), vmem),
    )(a, w, x, gt_seg)


def _merge_kernel(*refs):
    nb = N_BRANCH
    o_refs, g_refs = refs[:nb], refs[nb:2 * nb]
    wb_ref, out_ref, wbf_ref = refs[2 * nb:]

    @pl.when(pl.program_id(1) == 0)
    def _():
        wbf_ref[...] = wb_ref[...].astype(BF16)
    acc = None
    for n_ in range(nb):
        up = jnp.dot(o_refs[n_][...], wbf_ref[n_], preferred_element_type=F32)
        t = jax.nn.sigmoid(g_refs[n_][...]) * up
        acc = t if acc is None else acc + t
    out_ref[...] = acc.astype(out_ref.dtype)


def merge_branches(branch_outs, p, gate_col0, w_branch, layer, *, bm=512, bn=512):
    m, bw = branch_outs[0].shape
    dm = w_branch.shape[-1]
    bm, bn = _tile(m, bm), _tile(dm, bn)
    assert gate_col0 % bn == 0
    o_spec = pl.BlockSpec((bm, bw), lambda j, i: (i, 0))
    g_specs = [pl.BlockSpec((bm, bn), functools.partial(lambda j, i, c: (i, c + j), c=(gate_col0 + n_ * dm) // bn))
               for n_ in range(N_BRANCH)]
    vmem = (N_BRANCH * (2 * _nbytes((bm, bw), BF16) + 2 * _nbytes((bm, bn), F32) + 2 * _nbytes((bw, bn), F32)
                        + _nbytes((bw, bn), BF16)) + 6 * _nbytes((bm, bn), F32))
    return pl.pallas_call(
        _merge_kernel, name="merge_branches",
        grid=(dm // bn, m // bm),
        in_specs=[o_spec] * N_BRANCH + g_specs
                 + [pl.BlockSpec((None, N_BRANCH, bw, bn), lambda j, i: (layer, 0, 0, j))],
        out_specs=pl.BlockSpec((bm, bn), lambda j, i: (i, j)),
        out_shape=jax.ShapeDtypeStruct((m, dm), BF16),
        scratch_shapes=[pltpu.VMEM((N_BRANCH, bw, bn), BF16)],
        compiler_params=_params(("arbitrary", "arbitrary"), vmem),
    )(*branch_outs, *([p] * N_BRANCH), w_branch)


def _modnorm_kernel(*refs, with_router):
    if with_router:
        x_ref, g_ref, sc_ref, sh_ref, wr_ref, h_ref, logit_ref = refs
    else:
        x_ref, g_ref, sc_ref, sh_ref, h_ref = refs
    x = x_ref[...]
    y = x * lax.rsqrt(jnp.mean(x * x, axis=-1, keepdims=True) + EPS) * g_ref[...]
    h = y * (1 + sc_ref[...]) + sh_ref[...]
    h_ref[...] = h.astype(h_ref.dtype)
    if with_router:
        logit_ref[...] = jnp.dot(h, wr_ref[...], precision=lax.Precision.HIGHEST, preferred_element_type=F32)


def modulated_norm(x, g, sc_seg, sh_seg, w_router=None, router_layer=0, *, bm=256):
    m, d = x.shape
    seg = m // sc_seg.shape[0]
    bm = _tile(seg, bm)
    mod_spec = pl.BlockSpec((None, 1, d), lambda i: (i * bm // seg, 0, 0))
    in_specs = [pl.BlockSpec((bm, d), lambda i: (i, 0)), pl.BlockSpec((1, d), lambda i: (0, 0)), mod_spec, mod_spec]
    out_specs = [pl.BlockSpec((bm, d), lambda i: (i, 0))]
    out_shape = [jax.ShapeDtypeStruct((m, d), BF16)]
    args = [x, g.reshape(1, d), sc_seg, sh_seg]
    if w_router is not None:
        ne = w_router.shape[-1]
        in_specs.append(pl.BlockSpec((None, d, ne), lambda i: (router_layer, 0, 0)))
        out_specs.append(pl.BlockSpec((bm, ne), lambda i: (i, 0)))
        out_shape.append(jax.ShapeDtypeStruct((m, ne), F32))
        args.append(w_router)
    vmem = 8 * _nbytes((bm, d), F32) + 2 * _nbytes((d, 128), F32)
    out = pl.pallas_call(
        functools.partial(_modnorm_kernel, with_router=w_router is not None), name="modulated_norm",
        grid=(m // bm,),
        in_specs=in_specs, out_specs=out_specs, out_shape=out_shape,
        compiler_params=_params(("arbitrary",), vmem),
    )(*args)
    return out if w_router is not None else out[0]


ROPE_D = 64
HEAD_PAD = 128


def _rope_tables(t, head_dim, heads):
    half, n_freq = ROPE_D // 2, ROPE_D // 4
    rows = t // GRID_W
    r, col = jnp.meshgrid(jnp.arange(rows, dtype=F32), jnp.arange(GRID_W, dtype=F32), indexing='ij')
    inv = ROPE_THETA ** (-jnp.arange(n_freq, dtype=F32) / n_freq)
    ang = jnp.concatenate([r.reshape(-1, 1) * inv, col.reshape(-1, 1) * inv], axis=-1)
    plain = head_dim - ROPE_D
    cos = jnp.concatenate([jnp.ones((t, plain), F32), jnp.cos(ang), jnp.cos(ang)], axis=-1)
    sin = jnp.concatenate([jnp.zeros((t, plain), F32), jnp.sin(ang), jnp.sin(ang)], axis=-1)
    m_plus = np.concatenate([np.zeros(plain), -np.ones(half), np.zeros(half)]).astype(np.float32)
    m_minus = np.concatenate([np.zeros(plain), np.zeros(half), np.ones(half)]).astype(np.float32)
    tile = lambda z: jnp.tile(jnp.asarray(z), (1,) * (jnp.ndim(z) - 1) + (heads,))
    return dict(cos=tile(cos), sin=tile(sin), m_plus=tile(m_plus[None, :]), m_minus=tile(m_minus[None, :]))


def _dot_hi_lo(a, b_bf16):
    hi = a.astype(BF16)
    lo = (a - hi.astype(F32)).astype(BF16)
    return (jnp.dot(hi, b_bf16, preferred_element_type=F32) + jnp.dot(lo, b_bf16, preferred_element_type=F32))


def _headnorm_kernel(*refs, d_head, nope, use_rope, rope_blocks, out_f32):
    it = iter(refs)
    x_ref = next(it)
    extra_ref = next(it) if nope else None
    e_ref, et_ref, g_ref = next(it), next(it), next(it)
    if use_rope:
        cos_ref, sin_ref, mp_ref, mm_ref = next(it), next(it), next(it), next(it)
    obf_ref = next(it)
    x = x_ref[...]
    if nope:
        ex = extra_ref[...]
        x = jnp.concatenate([part for h in range(x.shape[1] // nope) for part in (x[:, h * nope:(h + 1) * nope], ex)],
                            axis=1)
    ss = _dot_hi_lo(x * x, e_ref[...])
    scale = lax.rsqrt(ss / d_head + EPS)
    y = x * _dot_hi_lo(scale, et_ref[...]) * g_ref[...]
    if out_f32:
        next(it)[...] = y
    if not use_rope:
        obf_ref[...] = y.astype(obf_ref.dtype)
        return
    i = pl.program_id(0)
    is_rope = (i >= rope_blocks[0]) & (i < rope_blocks[1])

    @pl.when(is_rope)
    def _():
        bw = y.shape[1]
        partner = (pltpu.roll(y, bw - ROPE_D // 2, 1) * mp_ref[...] + pltpu.roll(y, ROPE_D // 2, 1) * mm_ref[...])
        obf_ref[...] = (y * cos_ref[...] + partner * sin_ref[...]).astype(obf_ref.dtype)

    @pl.when(jnp.logical_not(is_rope))
    def _():
        obf_ref[...] = y.astype(obf_ref.dtype)


def head_norm(x, col0, width, gains, d_head, *, bw=None, extra=None, rope=None, rope_rows=None, out_f32=False, bm=512):
    m = x.shape[0]
    nope = 0
    if extra is not None:
        e = extra.shape[1]
        nope = d_head - e
        heads = width // nope
        bw_in, bw = width, heads * d_head
        wout = bw
    else:
        bw = bw or width
        bw_in, wout = bw, width
    assert col0 % bw_in == 0 and wout % bw == 0 and bw % d_head == 0
    nh = bw // d_head
    seg = np.arange(bw) // d_head
    e_mat = (seg[:, None] == np.arange(HEAD_PAD)[None, :]).astype(np.float32)
    g = jnp.tile(gains.astype(F32).reshape(-1), wout // gains.size).reshape(1, wout)
    use_rope = rope is not None
    bm = _tile(functools.reduce(math.gcd, rope_rows, m) if use_rope else m, bm)
    in_specs = [pl.BlockSpec((bm, bw_in), lambda i, j: (i, col0 // bw_in + j))]
    args = [x]
    if extra is not None:
        in_specs.append(pl.BlockSpec((bm, extra.shape[1]), lambda i, j: (i, 0)))
        args.append(extra)
    in_specs += [pl.BlockSpec((bw, HEAD_PAD), lambda i, j: (0, 0)), pl.BlockSpec((HEAD_PAD, bw), lambda i, j: (0, 0)),
                 pl.BlockSpec((1, bw), lambda i, j: (0, j))]
    args += [jnp.asarray(e_mat, BF16), jnp.asarray(e_mat.T, BF16), g]
    rope_blocks = (0, 0)
    if use_rope:
        r0, r1, period = rope_rows
        assert r0 % bm == 0 and r1 % bm == 0 and period % bm == 0 and nh * d_head == bw
        rope_blocks = (r0 // bm, r1 // bm)
        nper, nrb = period // bm, (r1 - r0) // bm
        tspec = pl.BlockSpec((bm, bw), lambda i, j: (jnp.clip(i - r0 // bm, 0, nrb - 1) % nper, j))
        mspec = pl.BlockSpec((1, bw), lambda i, j: (0, j))
        in_specs += [tspec, tspec, mspec, mspec]
        args += [rope['cos'], rope['sin'], rope['m_plus'], rope['m_minus']]
    out_specs = [pl.BlockSpec((bm, bw), lambda i, j: (i, j))]
    out_shape = [jax.ShapeDtypeStruct((m, wout), BF16)]
    if out_f32:
        out_specs.append(pl.BlockSpec((bm, bw), lambda i, j: (i, j)))
        out_shape.append(jax.ShapeDtypeStruct((m, wout), F32))
    vmem = 14 * _nbytes((bm, bw), F32) + 4 * _nbytes((bw, HEAD_PAD), F32)
    out = pl.pallas_call(
        functools.partial(_headnorm_kernel, d_head=d_head, nope=nope, use_rope=use_rope, rope_blocks=rope_blocks,
                          out_f32=out_f32), name="head_norm",
        grid=(m // bm, wout // bw),
        in_specs=in_specs, out_specs=out_specs, out_shape=out_shape,
        compiler_params=_params(("arbitrary", "arbitrary"), vmem),
    )(*args)
    return out if out_f32 else out[0]


S5_CHUNK = 64
S5_HALF = 256


def _s5_tables(a_re, a_im, log_dt, b_re, b_im, c_re, c_im):
    ndir, g, n = a_re.shape
    ch = b_re.shape[-1]
    gh = S5_HALF // ch
    nh = g // gh
    dt = jnp.exp(log_dt)[..., None]
    er, ei = a_re * dt, a_im * dt
    mag = jnp.exp(er)
    lb_re, lb_im = mag * jnp.cos(ei), mag * jnp.sin(ei)
    den = a_re * a_re + a_im * a_im
    xr, xi = lb_re - 1.0, lb_im
    co_re, co_im = (xr * a_re + xi * a_im) / den, (xi * a_re - xr * a_im) / den
    bb_re = co_re[..., None] * b_re - co_im[..., None] * b_im
    bb_im = co_re[..., None] * b_im + co_im[..., None] * b_re
    eye = jnp.eye(gh, dtype=F32)

    def blockdiag_in(z):
        z = z.reshape(ndir, nh, gh, n, ch)
        return jnp.einsum('dhgnj,gk->dhgjkn', z, eye).reshape(ndir, nh, gh * ch, gh * n)

    def blockdiag_out(z):
        z = z.reshape(ndir, nh, gh, ch, n)
        return jnp.einsum('dhgin,gk->dhkngi', z, eye).reshape(ndir, nh, gh * n, gh * ch)

    bmat = jnp.concatenate([blockdiag_in(bb_re), blockdiag_in(bb_im)], axis=-1).astype(BF16)
    cmat = jnp.concatenate([blockdiag_out(c_re), -blockdiag_out(c_im)], axis=-2).astype(BF16)

    def halves(z):
        return z.reshape(z.shape[:-2] + (nh, gh * n))

    steps = jnp.arange(S5_CHUNK, dtype=F32)
    expo = jnp.stack([steps, S5_CHUNK - 1 - steps])[:, :, None, None]

    def powers(sign):
        pr, pi = sign * expo * er[:, None], sign * expo * ei[:, None]
        pm = jnp.exp(pr)
        tab = jnp.concatenate([halves(pm * jnp.cos(pi)), halves(pm * jnp.sin(pi))], axis=-1)
        return jnp.swapaxes(tab, 1, 2)

    lam = jnp.concatenate([halves(lb_re), halves(lb_im)], axis=-1)[:, :, None, :]
    ti = np.arange(S5_CHUNK)
    tri = jnp.asarray(np.stack([ti[:, None] >= ti[None, :], ti[:, None] <= ti[None, :]]), BF16)
    return dict(bmat=bmat, cmat=cmat, p=powers(1.0), pinv=powers(-1.0), lam=lam, tri=tri)


def _s5_kernel(u_ref, bm_ref, cm_ref, p_ref, pinv_ref, lam_ref, tri_ref, h0_ref, y_ref, hT_ref, hs_ref, st_ref,
               *, nb, nchunks):
    d, c = pl.program_id(0), pl.program_id(3)
    L = S5_CHUNK
    half = p_ref.shape[-1] // 2

    @pl.when(c == 0)
    def _():
        st_ref[...] = h0_ref[...]

    u = u_ref[...].reshape(nb * L, u_ref.shape[-1]).astype(BF16)
    bu = jnp.dot(u, bm_ref[...], preferred_element_type=F32)
    p_re, p_im = p_ref[:, :half], p_ref[:, half:]
    q_re, q_im = pinv_ref[:, :half], pinv_ref[:, half:]
    lam_re, lam_im = lam_ref[:, :half], lam_ref[:, half:]
    tri = tri_ref[...]
    for i in range(nb):
        b_re, b_im = bu[i * L:(i + 1) * L, :half], bu[i * L:(i + 1) * L, half:]
        z = jnp.concatenate([b_re * q_re - b_im * q_im, b_re * q_im + b_im * q_re], axis=1)
        w = jnp.dot(tri, z.astype(BF16), preferred_element_type=F32)
        s_re, s_im = st_ref[i:i + 1, :half], st_ref[i:i + 1, half:]
        w_re = w[:, :half] + (lam_re * s_re - lam_im * s_im)
        w_im = w[:, half:] + (lam_re * s_im + lam_im * s_re)
        h = jnp.concatenate([p_re * w_re - p_im * w_im, p_re * w_im + p_im * w_re], axis=1)
        hs_ref[i * L:(i + 1) * L, :] = h.astype(BF16)
        st_ref[i:i + 1, :] = jnp.where(d == 0, h[L - 1:L, :], h[0:1, :])
    y = jnp.dot(hs_ref[...], cm_ref[...], preferred_element_type=F32)
    y_ref[...] = y.reshape(y_ref.shape)

    @pl.when(c == nchunks - 1)
    def _():
        hT_ref[...] = st_ref[...]


def s5_scan(p3, col0, seq0, n_seq, tabs, h0):
    _, t, _ = p3.shape
    ndir, nh, hc, ns2 = tabs['bmat'].shape
    L = S5_CHUNK
    nb = _tile(n_seq, 8)
    assert t % L == 0 and seq0 % nb == 0 and col0 % hc == 0
    nchunks = t // L
    sg0, cb0 = seq0 // nb, col0 // hc

    def tchunk(d, c):
        return c + d * (nchunks - 1 - 2 * c)

    tab = lambda r, cc: pl.BlockSpec((None, None, r, cc), lambda d, hf, sg, c: (d, hf, 0, 0))
    vmem = (4 * _nbytes((nb * L, hc), F32) + 4 * _nbytes((hc, ns2), BF16) + 4 * _nbytes((L, ns2), F32)
            + 3 * _nbytes((nb * L, ns2), F32) + 8 * _nbytes((L, ns2), F32))
    return pl.pallas_call(
        functools.partial(_s5_kernel, nb=nb, nchunks=nchunks), name="s5_scan",
        grid=(ndir, nh, n_seq // nb, nchunks),
        in_specs=[pl.BlockSpec((nb, L, hc), lambda d, hf, sg, c: (sg0 + sg, tchunk(d, c), cb0 + hf)),
                  tab(hc, ns2), tab(ns2, hc), tab(L, ns2), tab(L, ns2), tab(1, ns2),
                  pl.BlockSpec((None, L, L), lambda d, hf, sg, c: (d, 0, 0)),
                  pl.BlockSpec((None, None, nb, ns2), lambda d, hf, sg, c: (d, hf, sg, 0))],
        out_specs=[pl.BlockSpec((None, nb, L, hc), lambda d, hf, sg, c: (d, sg, tchunk(d, c), hf)),
                   pl.BlockSpec((None, None, nb, ns2), lambda d, hf, sg, c: (d, hf, sg, 0))],
        out_shape=[jax.ShapeDtypeStruct((ndir, n_seq, t, nh * hc), F32),
                   jax.ShapeDtypeStruct((ndir, nh, n_seq, ns2), F32)],
        scratch_shapes=[pltpu.VMEM((nb * L, ns2), BF16), pltpu.VMEM((nb, ns2), F32)],
        compiler_params=_params(("arbitrary",) * 4, vmem),
    )(p3, tabs['bmat'], tabs['cmat'], tabs['p'], tabs['pinv'], tabs['lam'], tabs['tri'], h0)


def _ssm_post_kernel(y_ref, u_ref, d_ref, w_ref, o_ref, wbf_ref):
    @pl.when(pl.program_id(0) == 0)
    def _():
        wbf_ref[...] = w_ref[...].astype(BF16)
    y = y_ref[0] + y_ref[1] + d_ref[...] * u_ref[...]
    g = jax.nn.gelu(y)
    z = jnp.dot(g.astype(BF16), wbf_ref[...], preferred_element_type=F32)
    o_ref[...] = (g * jax.nn.sigmoid(z)).astype(o_ref.dtype)


def ssm_post(y2, p, col0, d_vec, w_glu, layer, *, bm=1024):
    _, m, w = y2.shape
    bm = _tile(m, bm)
    assert col0 % w == 0
    vmem = 6 * _nbytes((bm, w), F32) * 2 + 3 * _nbytes((w, w), F32)
    return pl.pallas_call(
        _ssm_post_kernel, name="ssm_post",
        grid=(m // bm,),
        in_specs=[pl.BlockSpec((2, bm, w), lambda i: (0, i, 0)),
                  pl.BlockSpec((bm, w), lambda i: (i, col0 // w)),
                  pl.BlockSpec((1, w), lambda i: (0, 0)),
                  pl.BlockSpec((None, w, w), lambda i: (layer, 0, 0))],
        out_specs=pl.BlockSpec((bm, w), lambda i: (i, 0)),
        out_shape=jax.ShapeDtypeStruct((m, w), BF16),
        scratch_shapes=[pltpu.VMEM((w, w), BF16)],
        compiler_params=_params(("arbitrary",), vmem),
    )(y2, p, d_vec.reshape(1, w), w_glu)


def _is_new_expert(be_ref, i):
    return (i == 0) | (be_ref[i] != be_ref[jnp.maximum(i - 1, 0)])


def _moe_glu_kernel(be_ref, nb_ref, x_ref, wg_ref, wu_ref, o_ref, wg_bf, wu_bf):
    i = pl.program_id(1)

    @pl.when(_is_new_expert(be_ref, i))
    def _():
        wg_bf[...] = wg_ref[...].astype(BF16)
        wu_bf[...] = wu_ref[...].astype(BF16)

    @pl.when(i < nb_ref[0])
    def _():
        x = x_ref[...]
        a = jnp.dot(x, wg_bf[...], preferred_element_type=F32)
        b = jnp.dot(x, wu_bf[...], preferred_element_type=F32)
        o_ref[...] = (a * jax.nn.sigmoid(a) * b).astype(o_ref.dtype)

    @pl.when(i >= nb_ref[0])
    def _():
        o_ref[...] = jnp.zeros_like(o_ref)


def moe_glu_up(x, wg, wu, layer, block_expert, nblocks, *, bm, bn=512):
    r, k = x.shape
    n = wg.shape[-1]
    assert r % bm == 0 and n % bn == 0
    wspec = pl.BlockSpec((None, None, k, bn), lambda j, i, be, nb: (layer, be[i], 0, j))
    vmem = (2 * _nbytes((bm, k), x.dtype) + 4 * _nbytes((k, bn), F32) + 2 * _nbytes((k, bn), BF16)
            + 2 * _nbytes((bm, bn), BF16) + 3 * _nbytes((bm, bn), F32))
    return pl.pallas_call(
        _moe_glu_kernel, name="moe_glu_up",
        grid_spec=pltpu.PrefetchScalarGridSpec(
            num_scalar_prefetch=2,
            grid=(n // bn, r // bm),
            in_specs=[pl.BlockSpec((bm, k), lambda j, i, be, nb: (i, 0)), wspec, wspec],
            out_specs=pl.BlockSpec((bm, bn), lambda j, i, be, nb: (i, j)),
            scratch_shapes=[pltpu.VMEM((k, bn), BF16), pltpu.VMEM((k, bn), BF16)]),
        out_shape=jax.ShapeDtypeStruct((r, n), BF16),
        compiler_params=_params(("arbitrary", "arbitrary"), vmem),
    )(block_expert, nblocks, x, wg, wu)


def _moe_down_kernel(be_ref, nb_ref, x_ref, w_ref, o_ref, w_bf):
    i = pl.program_id(1)

    @pl.when(_is_new_expert(be_ref, i))
    def _():
        w_bf[...] = w_ref[...].astype(BF16)

    @pl.when(i < nb_ref[0])
    def _():
        o_ref[...] = jnp.dot(x_ref[...], w_bf[...], preferred_element_type=F32)

    @pl.when(i >= nb_ref[0])
    def _():
        o_ref[...] = jnp.zeros_like(o_ref)


def moe_down(x, w, layer, block_expert, nblocks, *, bm, bn=512):
    r, k = x.shape
    n = w.shape[-1]
    assert r % bm == 0 and n % bn == 0
    vmem = (2 * _nbytes((bm, k), x.dtype) + 2 * _nbytes((k, bn), F32) + _nbytes((k, bn), BF16)
            + 2 * _nbytes((bm, bn), F32))
    return pl.pallas_call(
        _moe_down_kernel, name="moe_down",
        grid_spec=pltpu.PrefetchScalarGridSpec(
            num_scalar_prefetch=2,
            grid=(n // bn, r // bm),
            in_specs=[pl.BlockSpec((bm, k), lambda j, i, be, nb: (i, 0)),
                      pl.BlockSpec((None, None, k, bn), lambda j, i, be, nb: (layer, be[i], 0, j))],
            out_specs=pl.BlockSpec((bm, bn), lambda j, i, be, nb: (i, j)),
            scratch_shapes=[pltpu.VMEM((k, bn), BF16)]),
        out_shape=jax.ShapeDtypeStruct((r, n), F32),
        compiler_params=_params(("arbitrary", "arbitrary"), vmem),
    )(block_expert, nblocks, x, w)


def _qk(q, k):
    return lax.dot_general(q, k, (((1,), (1,)), ((), ())), preferred_element_type=F32)


def _row_sinks(sink_ref, first_head, rep, rows_per_head):
    r = lax.broadcasted_iota(jnp.int32, (rep * rows_per_head, 1), 0)
    sink = jnp.full((rep * rows_per_head, 1), sink_ref[first_head], F32)
    for g in range(1, rep):
        sink = jnp.where(r >= g * rows_per_head, sink_ref[first_head + g], sink)
    return sink


def _stack_heads(q_ref, heads, d):
    parts = [q_ref[:, h * d:(h + 1) * d] for h in heads]
    return parts[0] if len(parts) == 1 else jnp.concatenate(parts, axis=0)


def _attn_kernel(*refs, heads, kv_heads, dq, dv, scale, use_sink, use_ctx):
    refs = list(refs)
    sink_ref = refs.pop(0) if use_sink else None
    q_ref, k_ref, v_ref = refs[:3]
    kc_ref, vc_ref = (refs[3], refs[4]) if use_ctx else (None, None)
    o_ref = refs[-1]
    rep = heads // kv_heads
    bq = q_ref.shape[0]
    pieces = [None] * heads
    for j in range(kv_heads):
        q = _stack_heads(q_ref, range(j * rep, (j + 1) * rep), dq)
        s = _qk(q, k_ref[:, j * dq:(j + 1) * dq]) * scale
        m = jnp.max(s, axis=-1, keepdims=True)
        if use_ctx:
            s_c = _qk(q, kc_ref[:, j * dq:(j + 1) * dq].astype(BF16)) * scale
            m = jnp.maximum(m, jnp.max(s_c, axis=-1, keepdims=True))
        if use_sink:
            sink = _row_sinks(sink_ref, j * rep, rep, bq)
            m = jnp.maximum(m, sink)
        p = jnp.exp(s - m)
        l = jnp.sum(p, axis=-1, keepdims=True)
        o = jnp.dot(p.astype(BF16), v_ref[:, j * dv:(j + 1) * dv].astype(BF16), preferred_element_type=F32)
        if use_ctx:
            p_c = jnp.exp(s_c - m)
            l = l + jnp.sum(p_c, axis=-1, keepdims=True)
            o = o + jnp.dot(p_c.astype(BF16), vc_ref[:, j * dv:(j + 1) * dv].astype(BF16), preferred_element_type=F32)
        if use_sink:
            l = l + jnp.exp(sink - m)
        o = o * (1.0 / l)
        for g in range(rep):
            pieces[j * rep + g] = o[g * bq:(g + 1) * bq]
    o_ref[...] = jnp.concatenate(pieces, axis=1).astype(o_ref.dtype)


def attention(q, k, v, *, heads, kv_heads, scale, sink=None, k_ctx=None, v_ctx=None, bq=512):
    b, t, qw = q.shape
    s = k.shape[1]
    dq, dv = qw // heads, v.shape[2] // kv_heads
    rep = heads // kv_heads
    bq = _tile(t, bq)
    use_sink, use_ctx = sink is not None, k_ctx is not None
    in_specs = [pl.BlockSpec((None, bq, qw), lambda bi, qi: (bi, qi, 0)),
                pl.BlockSpec((None, s, k.shape[2]), lambda bi, qi: (bi, 0, 0)),
                pl.BlockSpec((None, s, v.shape[2]), lambda bi, qi: (bi, 0, 0))]
    args = [q, k, v]
    c = 0
    if use_ctx:
        c = k_ctx.shape[1]
        in_specs += [pl.BlockSpec((None, c, k_ctx.shape[2]), lambda bi, qi: (bi, 0, 0)),
                     pl.BlockSpec((None, c, v_ctx.shape[2]), lambda bi, qi: (bi, 0, 0))]
        args += [k_ctx, v_ctx]
    if use_sink:
        in_specs = [pl.BlockSpec(memory_space=pltpu.SMEM)] + in_specs
        args = [sink.astype(F32)] + args
    vmem = (5 * _nbytes((rep * bq, s + c), F32) + 2 * _nbytes((bq, qw), BF16) + 2 * _nbytes((s, k.shape[2] + v.shape[2]), F32)
            + 3 * _nbytes((c, k.shape[2] + v.shape[2]), F32) + 6 * _nbytes((bq, heads * dv), F32))
    return pl.pallas_call(
        functools.partial(_attn_kernel, heads=heads, kv_heads=kv_heads, dq=dq, dv=dv, scale=scale,
                          use_sink=use_sink, use_ctx=use_ctx), name="attention",
        grid=(b, t // bq),
        in_specs=in_specs,
        out_specs=pl.BlockSpec((None, bq, heads * dv), lambda bi, qi: (bi, qi, 0)),
        out_shape=jax.ShapeDtypeStruct((b, t, heads * dv), BF16),
        compiler_params=_params(("arbitrary",) * 2, vmem),
    )(*args)


def _diff_attn_kernel(*refs, heads, scale, post_scale, use_ctx):
    lam_ref, q_ref, k_ref, v_ref = refs[:4]
    kc_ref, vc_ref = (refs[4], refs[5]) if use_ctx else (None, None)
    g_ref, o_ref = refs[-2:]
    d, dv = DIFF_HD, DIFF_VD

    def probs(c0):
        q = q_ref[:, c0:c0 + d]
        s = _qk(q, k_ref[:, c0:c0 + d]) * scale
        m = jnp.max(s, axis=-1, keepdims=True)
        if use_ctx:
            s_c = _qk(q, kc_ref[:, c0:c0 + d].astype(BF16)) * scale
            m = jnp.maximum(m, jnp.max(s_c, axis=-1, keepdims=True))
        p = jnp.exp(s - m)
        l = jnp.sum(p, axis=-1, keepdims=True)
        if not use_ctx:
            return p * (1.0 / l), None
        p_c = jnp.exp(s_c - m)
        inv = 1.0 / (l + jnp.sum(p_c, axis=-1, keepdims=True))
        return p * inv, p_c * inv

    pieces = []
    for h in range(heads):
        p0, pc0 = probs(2 * h * d)
        p1, pc1 = probs((2 * h + 1) * d)
        lam = lam_ref[h]
        o = jnp.dot((p0 - lam * p1).astype(BF16), v_ref[:, h * dv:(h + 1) * dv].astype(BF16),
                    preferred_element_type=F32)
        if use_ctx:
            o = o + jnp.dot((pc0 - lam * pc1).astype(BF16), vc_ref[:, h * dv:(h + 1) * dv].astype(BF16),
                            preferred_element_type=F32)
        o = o * lax.rsqrt(jnp.mean(o * o, axis=-1, keepdims=True) + EPS) * g_ref[...]
        pieces.append(o * post_scale)
    o_ref[...] = jnp.concatenate(pieces, axis=1).astype(o_ref.dtype)


def diff_attention(q, k, v, lam, g_out, lambda_init, *, k_ctx=None, v_ctx=None, bq=512):
    b, t, qw = q.shape
    s, vw = k.shape[1], v.shape[2]
    heads = vw // DIFF_VD
    bq = _tile(t, bq)
    use_ctx = k_ctx is not None
    in_specs = [pl.BlockSpec(memory_space=pltpu.SMEM),
                pl.BlockSpec((None, bq, qw), lambda bi, qi: (bi, qi, 0)),
                pl.BlockSpec((None, s, qw), lambda bi, qi: (bi, 0, 0)),
                pl.BlockSpec((None, s, vw), lambda bi, qi: (bi, 0, 0))]
    args = [lam.astype(F32), q, k, v]
    c = 0
    if use_ctx:
        c = k_ctx.shape[1]
        in_specs += [pl.BlockSpec((None, c, qw), lambda bi, qi: (bi, 0, 0)),
                     pl.BlockSpec((None, c, vw), lambda bi, qi: (bi, 0, 0))]
        args += [k_ctx, v_ctx]
    in_specs.append(pl.BlockSpec((1, DIFF_VD), lambda bi, qi: (0, 0)))
    args.append(g_out.astype(F32).reshape(1, DIFF_VD))
    vmem = (8 * _nbytes((bq, s + c), F32) + 2 * _nbytes((bq, qw), BF16) + 2 * _nbytes((s, qw + vw), F32)
            + 3 * _nbytes((c, qw + vw), F32) + 6 * _nbytes((bq, vw), F32))
    return pl.pallas_call(
        functools.partial(_diff_attn_kernel, heads=heads, scale=DIFF_HD ** -0.5, post_scale=1.0 - lambda_init,
                          use_ctx=use_ctx), name="diff_attention",
        grid=(b, t // bq),
        in_specs=in_specs,
        out_specs=pl.BlockSpec((None, bq, vw), lambda bi, qi: (bi, qi, 0)),
        out_shape=jax.ShapeDtypeStruct((b, t, vw), BF16),
        compiler_params=_params(("arbitrary",) * 2, vmem),
    )(*args)


def _swa_lat_kernel(sink_ref, q_ref, kc_ref, vc_ref, k0_ref, k1_ref, k2_ref, v0_ref, v1_ref, v2_ref, o_ref,
                    *, scale, nblocks):
    n = pl.program_id(1)
    rows = SWA_REP * BLOCK
    r = lax.broadcasted_iota(jnp.int32, (rows, 3 * BLOCK), 0) & (BLOCK - 1)
    c = lax.broadcasted_iota(jnp.int32, (rows, 3 * BLOCK), 1)
    kpos = (n - 1) * BLOCK + c
    mask = (c - r >= BLOCK - WINDOW) & (c - r <= BLOCK + WINDOW) & (kpos >= 0) & (kpos < nblocks * BLOCK)
    pieces = [None] * SWA_HEADS
    for j in range(SWA_KV_HEADS):
        cs = slice(j * SWA_HD, (j + 1) * SWA_HD)
        q = _stack_heads(q_ref, range(j * SWA_REP, (j + 1) * SWA_REP), SWA_HD)
        kw = jnp.concatenate([k0_ref[:, cs], k1_ref[:, cs], k2_ref[:, cs]], axis=0)
        vw = jnp.concatenate([v0_ref[:, cs], v1_ref[:, cs], v2_ref[:, cs]], axis=0).astype(BF16)
        s_ctx = _qk(q, kc_ref[:, cs].astype(BF16)) * scale
        s_loc = jnp.where(mask, _qk(q, kw) * scale, NEG_INF)
        sink = _row_sinks(sink_ref, j * SWA_REP, SWA_REP, BLOCK)
        m = jnp.maximum(jnp.maximum(jnp.max(s_ctx, axis=-1, keepdims=True), jnp.max(s_loc, axis=-1, keepdims=True)), sink)
        p_ctx = jnp.exp(s_ctx - m)
        p_loc = jnp.exp(s_loc - m)
        l = jnp.sum(p_ctx, axis=-1, keepdims=True) + jnp.sum(p_loc, axis=-1, keepdims=True) + jnp.exp(sink - m)
        o = (jnp.dot(p_ctx.astype(BF16), vc_ref[:, cs].astype(BF16), preferred_element_type=F32)
             + jnp.dot(p_loc.astype(BF16), vw, preferred_element_type=F32)) * (1.0 / l)
        for g in range(SWA_REP):
            pieces[j * SWA_REP + g] = o[g * BLOCK:(g + 1) * BLOCK]
    o_ref[...] = jnp.concatenate(pieces, axis=1).astype(o_ref.dtype)


def swa_latent_attention(q, k, v, k_ctx, v_ctx, sink):
    b, t, qw = q.shape
    kw = k.shape[2]
    c = k_ctx.shape[1]
    nb = t // BLOCK
    pad = ((0, 0), (BLOCK, BLOCK), (0, 0))
    kp, vp = jnp.pad(k, pad), jnp.pad(v, pad)
    band = [pl.BlockSpec((None, BLOCK, kw), functools.partial(lambda bi, ni, o: (bi, ni + o, 0), o=o)) for o in range(3)]
    ctx = pl.BlockSpec((None, c, kw), lambda bi, ni: (bi, 0, 0))
    qspec = pl.BlockSpec((None, BLOCK, qw), lambda bi, ni: (bi, ni, 0))
    vmem = 8 * _nbytes((SWA_REP * BLOCK, c + 3 * BLOCK), F32) + 6 * _nbytes((c, kw), F32) + (4 << 20)
    return pl.pallas_call(
        functools.partial(_swa_lat_kernel, scale=SWA_HD ** -0.5, nblocks=nb), name="swa_latent_attention",
        grid=(b, nb),
        in_specs=[pl.BlockSpec(memory_space=pltpu.SMEM), qspec, ctx, ctx] + band + band,
        out_specs=qspec,
        out_shape=jax.ShapeDtypeStruct(q.shape, BF16),
        compiler_params=_params(("arbitrary",) * 2, vmem),
    )(sink.astype(F32), q, k_ctx, v_ctx, kp, kp, kp, vp, vp, vp)


def _state_to_halves(st):
    b, nd, _, g, n = st.shape
    gh = S5_HALF // SSM_CH
    z = st.reshape(b, nd, 2, g // gh, gh * n)
    return jnp.transpose(z, (1, 3, 0, 2, 4)).reshape(nd, g // gh, b, 2 * gh * n)


def _halves_to_state(h, groups):
    nd, nh, b, _ = h.shape
    z = h.reshape(nd, nh, b, 2, -1)
    return jnp.transpose(z, (2, 0, 3, 1, 4)).reshape(b, nd, 2, groups, SSM_N)


def _moe_ffn(logits, h_bf, P, j, bm):
    m = h_bf.shape[0]
    probs = jax.nn.softmax(logits, axis=-1)
    top_p, top_i = lax.top_k(probs, TOP_K)
    top_p = top_p / jnp.sum(top_p, axis=-1, keepdims=True)
    flat_e = top_i.reshape(-1).astype(jnp.int32)
    npairs = flat_e.shape[0]
    order = jnp.argsort(flat_e, stable=True).astype(jnp.int32)
    counts = jnp.sum(flat_e[:, None] == jnp.arange(N_EXPERTS, dtype=jnp.int32)[None, :], axis=0).astype(jnp.int32)
    blocks_per = (counts + bm - 1) // bm
    block_end = jnp.cumsum(blocks_per)
    pad_start = (block_end - blocks_per) * bm
    start = jnp.cumsum(counts) - counts
    sorted_e = flat_e[order]
    dest = pad_start[sorted_e] + jnp.arange(npairs, dtype=jnp.int32) - start[sorted_e]
    nrows = npairs + N_EXPERTS * bm
    pos = lax.sort_key_val(order, dest)[1].reshape(m, TOP_K)
    nblk = nrows // bm
    nused = block_end[-1:]
    block_expert = jnp.minimum(jnp.sum(jnp.arange(nblk, dtype=jnp.int32)[:, None] >= block_end[None, :], axis=1),
                               N_EXPERTS - 1).astype(jnp.int32)
    row = jnp.arange(nrows, dtype=jnp.int32)
    row_e = block_expert[row // bm]
    off = row - pad_start[row_e]
    rank = jnp.clip(start[row_e] + off, 0, npairs - 1)
    row_token = jnp.where(off < counts[row_e], order[rank] // TOP_K, 0)
    block_expert = jnp.where(jnp.arange(nblk) < nused[0], block_expert, block_expert[jnp.maximum(nused[0] - 1, 0)])
    x_sorted = jnp.take(h_bf, row_token, axis=0)
    act = moe_glu_up(x_sorted, P['w_moe_gate'], P['w_moe_up'], j, block_expert, nused, bm=bm, bn=1024)
    y_sorted = moe_down(act, P['w_moe_down'], j, block_expert, nused, bm=bm)
    out = jnp.zeros((m, y_sorted.shape[1]), F32)
    for kk in range(TOP_K):
        out = out + top_p[:, kk:kk + 1] * jnp.take(y_sorted, pos[:, kk], axis=0)
    return out


SWA_COLS = (SWA_HEADS + 2 * SWA_KV_HEADS) * SWA_HD
DIFF_COLS = 2 * DIFF_HEADS * 2 * DIFF_HD + DIFF_HEADS * DIFF_VD


def kernel(x_prompt, x_sample, cache_mla_ckv, cache_mla_krope, state_ssm, cache_swa_k, cache_swa_v, cache_diff_k, cache_diff_v, c, c_ctx, w_ada, b_ada, g_norm1, g_norm2, w_in, g_mla_qlat, g_mla_kvlat, w_mla_uq, w_mla_ukv, g_mla_q, g_mla_k, ssm_a_re, ssm_a_im, ssm_log_dt, ssm_b_re, ssm_b_im, ssm_c_re, ssm_c_im, ssm_d, w_ssm_glu, g_swa_q, g_swa_k, swa_sink, g_diff_q, g_diff_k, diff_lambda, g_diff_out, w_branch, w_out, w_ffn_gate, w_ffn_up, w_ffn_down, w_router, w_moe_gate, w_moe_up, w_moe_down):
    P = dict(ssm_a_re=ssm_a_re, ssm_a_im=ssm_a_im, ssm_log_dt=ssm_log_dt, ssm_b_re=ssm_b_re, ssm_b_im=ssm_b_im,
             ssm_c_re=ssm_c_re, ssm_c_im=ssm_c_im, ssm_d=ssm_d, w_router=w_router, w_moe_gate=w_moe_gate,
             w_moe_up=w_moe_up, w_moe_down=w_moe_down)
    bp, tp, d = x_prompt.shape
    bs, ts, _ = x_sample.shape
    depth = w_in.shape[0]
    past = cache_mla_ckv.shape[2]
    mp, ms = bp * tp, bs * ts
    m = mp + ms
    w = d // 4
    groups = w // SSM_CH
    x = jnp.concatenate([x_prompt.reshape(mp, d), x_sample.reshape(ms, d)], axis=0)

    seg = math.gcd(mp, ts)
    nseg = m // seg
    seg_mod = np.concatenate([np.zeros(mp // seg, np.int32), 1 + np.repeat(np.arange(bs, dtype=np.int32), ts // seg)])
    cvec = jnp.concatenate([c_ctx[None, :], c, jnp.zeros((8 - 1 - bs, d), F32)], axis=0)
    cvec = cvec * jax.nn.sigmoid(cvec)

    def per_seg(v):
        return v[seg_mod][:, None, :]

    def gated_add(xx, gt_seg, upd):
        return (xx.reshape(nseg, seg, d) + gt_seg * upd.reshape(nseg, seg, d)).reshape(m, d)

    rope_mla = _rope_tables(ts, MLA_QK, MLA_HEADS)
    rope_swa_q = rope_diff = _rope_tables(ts, SWA_HD, SWA_HEADS)
    rope_swa_k = _rope_tables(ts, SWA_HD, SWA_KV_HEADS)
    assert DIFF_HD == SWA_HD and 2 * DIFF_HEADS == SWA_HEADS

    outs = [[] for _ in range(7)]
    for l in range(depth):
        lambda_init = 0.8 - 0.6 * math.exp(-0.3 * l)
        mods = matmul(cvec, w_ada, (l,), bm=8, bn=1024) + b_ada[l][None, :]
        sh1, sc1, gt1, sh2, sc2, gt2 = (per_seg(v) for v in jnp.split(mods, 6, axis=-1))

        h = modulated_norm(x, g_norm1[l], sc1, sh1)
        p, col0 = in_projection(h, w_in, l)
        k_rope = matmul(h, w_in, (l,), bn=IN_SHIFT_UNIT, cols=(2 * w // IN_SHIFT_UNIT, 1))[:, :MLA_ROPE]

        def tm(z, bb, tt):
            return z.reshape(bb, tt, z.shape[-1])

        lat_rows = (mp, m, ts)

        ckv_bf, c_kv = head_norm(p, col0['ckv'], w, g_mla_kvlat[l], w, out_f32=True)
        q_lat = head_norm(p, col0['q'], w, g_mla_qlat[l], w)
        q_m = head_norm(matmul(q_lat, w_mla_uq, (l,)), 0, MLA_HEADS * MLA_QK, g_mla_q[l], MLA_QK,
                        rope=rope_mla, rope_rows=lat_rows)
        ckv_all = jnp.concatenate([ckv_bf, cache_mla_ckv[:, l].reshape(bs * past, w).astype(BF16)], axis=0)
        krope_all = jnp.concatenate([k_rope, cache_mla_krope[:, l].reshape(bs * past, MLA_ROPE)], axis=0)
        w_ukv = jnp.swapaxes(w_mla_ukv[l].reshape(w, MLA_HEADS, 2, MLA_NOPE), 1, 2).reshape(w, -1)
        kv = matmul(ckv_all, w_ukv, bm=512)
        k_m = head_norm(kv, 0, MLA_HEADS * MLA_NOPE, g_mla_k[l], MLA_QK, extra=krope_all,
                        rope=rope_mla, rope_rows=lat_rows)
        v_m = kv[:, MLA_HEADS * MLA_NOPE:]
        mla = dict(heads=MLA_HEADS, kv_heads=MLA_HEADS, scale=MLA_QK ** -0.5)
        o_mla_p = attention(tm(q_m[:mp], bp, tp), tm(k_m[:mp], bp, tp), tm(v_m[:mp], bp, tp), **mla)
        o_mla_s = attention(tm(q_m[mp:], bs, ts), tm(k_m[mp:m], bs, ts), tm(v_m[mp:m], bs, ts),
                            k_ctx=tm(k_m[m:], bs, past), v_ctx=tm(v_m[m:], bs, past), **mla)
        o_mla = jnp.concatenate([o_mla_p.reshape(mp, w), o_mla_s.reshape(ms, w)], axis=0)

        tabs = _s5_tables(ssm_a_re[l], ssm_a_im[l], ssm_log_dt[l], ssm_b_re[l], ssm_b_im[l], ssm_c_re[l], ssm_c_im[l])
        ncol = p.shape[1]
        h0_p = jnp.zeros((2, w // S5_HALF, bp, 2 * (S5_HALF // SSM_CH) * SSM_N), F32)
        y_p, hT_p = s5_scan(p.reshape(m // tp, tp, ncol), col0['ssm'], 0, bp, tabs, h0_p)
        y_s, _ = s5_scan(p.reshape(m // ts, ts, ncol), col0['ssm'], mp // ts, bs, tabs, _state_to_halves(state_ssm[:, l]))
        y2 = jnp.concatenate([y_p.reshape(2, mp, w), y_s.reshape(2, ms, w)], axis=1)
        o_ssm = ssm_post(y2, p, col0['ssm'], ssm_d[l], w_ssm_glu, l)
        ssm_state = _halves_to_state(hT_p, groups)

        nq = SWA_HEADS * SWA_HD
        nk = SWA_KV_HEADS * SWA_HD
        c_swa = col0['swa']
        q_s = head_norm(p, c_swa, nq, g_swa_q[l], SWA_HD, bw=256, rope=rope_swa_q, rope_rows=lat_rows)
        k_s, k_s32 = head_norm(p, c_swa + nq, nk, g_swa_k[l], SWA_HD, rope=rope_swa_k, rope_rows=lat_rows, out_f32=True)
        v_s = p[:, c_swa + nq + nk:c_swa + nq + 2 * nk]
        o_swa_p = attention(tm(q_s[:mp], bp, tp), tm(k_s[:mp], bp, tp), tm(v_s[:mp], bp, tp), heads=SWA_HEADS,
                            kv_heads=SWA_KV_HEADS, scale=SWA_HD ** -0.5, sink=swa_sink[l], bq=256)
        o_swa_s = swa_latent_attention(tm(q_s[mp:], bs, ts), tm(k_s[mp:], bs, ts), tm(v_s[mp:], bs, ts),
                                       cache_swa_k[:, l].reshape(bs, past, nk), cache_swa_v[:, l].reshape(bs, past, nk),
                                       swa_sink[l])
        o_swa = jnp.concatenate([o_swa_p.reshape(mp, w), o_swa_s.reshape(ms, w)], axis=0)

        ndq = DIFF_HEADS * 2 * DIFF_HD
        c_diff = col0['diff']
        q_d = head_norm(p, c_diff, ndq, g_diff_q[l], DIFF_HD, bw=256, rope=rope_diff, rope_rows=lat_rows)
        k_d, k_d32 = head_norm(p, c_diff + ndq, ndq, g_diff_k[l], DIFF_HD, bw=256, rope=rope_diff, rope_rows=lat_rows,
                               out_f32=True)
        v_d = p[:, c_diff + 2 * ndq:c_diff + 2 * ndq + DIFF_HEADS * DIFF_VD]
        lp = diff_lambda[l]
        lam = jnp.exp(jnp.sum(lp[0] * lp[1], -1)) - jnp.exp(jnp.sum(lp[2] * lp[3], -1)) + lambda_init
        o_diff_p = diff_attention(tm(q_d[:mp], bp, tp), tm(k_d[:mp], bp, tp), tm(v_d[:mp], bp, tp),
                                  lam, g_diff_out[l], lambda_init, bq=256)
        o_diff_s = diff_attention(tm(q_d[mp:], bs, ts), tm(k_d[mp:], bs, ts), tm(v_d[mp:], bs, ts), lam, g_diff_out[l],
                                  lambda_init, k_ctx=cache_diff_k[:, l].reshape(bs, past, ndq),
                                  v_ctx=cache_diff_v[:, l].reshape(bs, past, DIFF_HEADS * DIFF_VD), bq=256)
        o_diff = jnp.concatenate([o_diff_p.reshape(mp, w), o_diff_s.reshape(ms, w)], axis=0)

        merged = merge_branches((o_mla, o_ssm, o_swa, o_diff), p, col0['gate'], w_branch, l, bm=1024, bn=256)
        x = matmul_residual(merged, w_out, (l,), x, gt1, bm=1024)

        j = l // 2
        if l % 2 == 0:
            h2 = modulated_norm(x, g_norm2[l], sc2, sh2)
            act = glu_up(h2, w_ffn_gate, w_ffn_up, (j,))
            x = matmul_residual(act, w_ffn_down, (j,), x, gt2)
        else:
            h2, logits = modulated_norm(x, g_norm2[l], sc2, sh2, w_router, j)
            x = gated_add(x, gt2, _moe_ffn(logits, h2, P, j, bm=256))

        outs[0].append(c_kv[:mp].reshape(bp, tp, w))
        outs[1].append(k_rope[:mp].reshape(bp, tp, MLA_ROPE))
        outs[2].append(ssm_state)
        outs[3].append(k_s32[:mp].reshape(bp, tp, SWA_KV_HEADS, SWA_HD))
        outs[4].append(v_s[:mp].reshape(bp, tp, SWA_KV_HEADS, SWA_HD))
        outs[5].append(k_d32[:mp].reshape(bp, tp, DIFF_HEADS, 2, DIFF_HD))
        outs[6].append(v_d[:mp].reshape(bp, tp, DIFF_HEADS, DIFF_VD))

    return (x[:mp].reshape(bp, tp, d), x[mp:].reshape(bs, ts, d)) + tuple(jnp.stack(o, axis=1) for o in outs)
```

```python
import functools
import math

import numpy as np
import jax
import jax.numpy as jnp
from jax import lax
from jax.experimental import pallas as pl
from jax.experimental.pallas import tpu as pltpu

F32 = jnp.float32
BF16 = jnp.bfloat16
EPS = 1e-6
NEG_INF = -1e30
ROPE_THETA = 10000.0
GRID_W = 64
BLOCK = 128
N_BRANCH = 4
MLA_HEADS = 4
MLA_NOPE = 128
MLA_ROPE = 64
MLA_QK = MLA_NOPE + MLA_ROPE
MLA_V = 128
SSM_CH = 16
SSM_N = 64
SWA_HEADS = 8
SWA_KV_HEADS = 2
SWA_REP = SWA_HEADS // SWA_KV_HEADS
SWA_HD = 64
WINDOW = 128
DIFF_HEADS = 4
DIFF_HD = 64
DIFF_VD = 2 * DIFF_HD
N_EXPERTS = 8
TOP_K = 2

V7X_VMEM_REQUEST_CAP = 56 * 1024 * 1024


def _params(semantics, vmem_bytes):
    limit = int(min(max(vmem_bytes * 5 // 4 + (2 << 20), 16 << 20), V7X_VMEM_REQUEST_CAP))
    return pltpu.CompilerParams(dimension_semantics=semantics, vmem_limit_bytes=limit)


def _tile(n, want):
    t = min(n, want)
    while n % t:
        t -= 1
    return t


def _nbytes(shape, dtype):
    return int(np.prod(shape)) * jnp.dtype(dtype).itemsize


def _mm_kernel(x_ref, w_ref, o_ref, *scratch):
    if scratch:
        wbf_ref, = scratch

        @pl.when(pl.program_id(1) == 0)
        def _():
            wbf_ref[...] = w_ref[...].astype(BF16)
        w = wbf_ref[...]
    else:
        w = w_ref[...]
    o_ref[...] = jnp.dot(x_ref[...].astype(BF16), w, preferred_element_type=F32).astype(o_ref.dtype)


def matmul(x, w, lead=(), *, bm=1024, bn=512, out_dtype=F32, cols=None):
    m, k = x.shape
    assert w.shape[-2] == k
    cb0 = 0
    if cols is None:
        n = w.shape[-1]
        bn = _tile(n, bn)
    else:
        cb0, n = cols[0], cols[1] * bn
    bm = _tile(m, bm)
    assert m % bm == 0 and n % bn == 0, (m, n, bm, bn)
    lead = tuple(lead)
    nl = len(lead)
    cast = w.dtype != BF16
    scratch = [pltpu.VMEM((k, bn), BF16)] if cast else []
    vmem = (2 * _nbytes((bm, k), x.dtype) + 2 * _nbytes((k, bn), w.dtype)
            + 2 * _nbytes((bm, bn), out_dtype) + (_nbytes((k, bn), BF16) if cast else 0))
    return pl.pallas_call(
        _mm_kernel, name="matmul",
        grid=(n // bn, m // bm),
        in_specs=[pl.BlockSpec((bm, k), lambda j, i: (i, 0)),
                  pl.BlockSpec((None,) * nl + (k, bn), lambda j, i: lead + (0, cb0 + j))],
        out_specs=pl.BlockSpec((bm, bn), lambda j, i: (i, j)),
        out_shape=jax.ShapeDtypeStruct((m, n), out_dtype),
        scratch_shapes=scratch,
        compiler_params=_params(("arbitrary", "arbitrary"), vmem),
    )(x, w)


IN_TILE = 1024
IN_SHIFT_UNIT = 128
IN_CAST_ROWS = 512


def _inproj_kernel(a_idx, b_idx, shift_ref, valid_ref, x_ref, wa_ref, wb_ref, o_ref, wbf_ref, *, shift):
    j = pl.program_id(0)
    first = pl.program_id(1) == 0

    @pl.when(first & (shift_ref[j] == 0))
    def _():
        wbf_ref[...] = wa_ref[...].astype(BF16)

    @pl.when(first & (shift_ref[j] != 0))
    def _():
        keep = lax.broadcasted_iota(jnp.int32, (IN_CAST_ROWS, IN_TILE), 1) < valid_ref[j]
        for r in range(0, wa_ref.shape[0], IN_CAST_ROWS):
            rows = slice(r, r + IN_CAST_ROWS)
            wcat = jnp.concatenate([wa_ref[rows, :], wb_ref[rows, :]], axis=1)
            wbf_ref[rows, :] = jnp.where(keep, wcat[:, shift:shift + IN_TILE], 0.0).astype(BF16)

    o_ref[...] = jnp.dot(x_ref[...], wbf_ref[...], preferred_element_type=F32)


def in_projection(h, w_in, layer, *, bm=1024):
    m, k = h.shape
    ntot = w_in.shape[-1]
    w = k // 4
    lead = 2 * w
    rest = ntot - lead - MLA_ROPE
    assert lead % IN_TILE == 0 and MLA_ROPE < IN_SHIFT_UNIT and IN_TILE % IN_SHIFT_UNIT == 0 and k % IN_CAST_ROWS == 0
    n_lead, n_rest = lead // IN_TILE, -(-rest // IN_TILE)
    last_unit = (ntot - 1) // IN_SHIFT_UNIT
    a_idx = list(range(n_lead)) + [n_lead + t for t in range(n_rest)]
    b_idx = [0] * n_lead + [min((lead + IN_TILE * (t + 1)) // IN_SHIFT_UNIT, last_unit) for t in range(n_rest)]
    shifted = [0] * n_lead + [1] * n_rest
    valid = [IN_TILE] * n_lead + [min(IN_TILE, rest - IN_TILE * t) for t in range(n_rest)]
    ntiles = len(a_idx)
    col0 = dict(q=0, ckv=w, ssm=lead, swa=lead + w, diff=lead + w + SWA_COLS, gate=lead + w + SWA_COLS + DIFF_COLS)
    bm = _tile(m, bm)
    vmem = (2 * _nbytes((bm, k), BF16) + 2 * _nbytes((k, IN_TILE + IN_SHIFT_UNIT), F32) + _nbytes((k, IN_TILE), BF16)
            + 3 * _nbytes((IN_CAST_ROWS, IN_TILE + IN_SHIFT_UNIT), F32) + 2 * _nbytes((bm, IN_TILE), F32))
    p = pl.pallas_call(
        functools.partial(_inproj_kernel, shift=MLA_ROPE), name="in_projection",
        grid_spec=pltpu.PrefetchScalarGridSpec(
            num_scalar_prefetch=4,
            grid=(ntiles, m // bm),
            in_specs=[pl.BlockSpec((bm, k), lambda j, i, a, b, s, v: (i, 0)),
                      pl.BlockSpec((None, k, IN_TILE), lambda j, i, a, b, s, v: (layer, 0, a[j])),
                      pl.BlockSpec((None, k, IN_SHIFT_UNIT), lambda j, i, a, b, s, v: (layer, 0, b[j]))],
            out_specs=pl.BlockSpec((bm, IN_TILE), lambda j, i, a, b, s, v: (i, j)),
            scratch_shapes=[pltpu.VMEM((k, IN_TILE), BF16)]),
        out_shape=jax.ShapeDtypeStruct((m, ntiles * IN_TILE), F32),
        compiler_params=_params(("arbitrary", "arbitrary"), vmem),
    )(jnp.asarray(a_idx, jnp.int32), jnp.asarray(b_idx, jnp.int32), jnp.asarray(shifted, jnp.int32),
      jnp.asarray(valid, jnp.int32), h, w_in, w_in)
    return p, col0


def _glu_kernel(x_ref, wg_ref, wu_ref, o_ref, wg_bf, wu_bf):
    @pl.when(pl.program_id(1) == 0)
    def _():
        wg_bf[...] = wg_ref[...].astype(BF16)
        wu_bf[...] = wu_ref[...].astype(BF16)
    x = x_ref[...]
    a = jnp.dot(x, wg_bf[...], preferred_element_type=F32)
    b = jnp.dot(x, wu_bf[...], preferred_element_type=F32)
    o_ref[...] = (a * jax.nn.sigmoid(a) * b).astype(o_ref.dtype)


def glu_up(x, wg, wu, lead, *, bm=1024, bn=512):
    m, k = x.shape
    n = wg.shape[-1]
    bm, bn = _tile(m, bm), _tile(n, bn)
    assert m % bm == 0 and n % bn == 0
    nl = len(lead)
    wspec = pl.BlockSpec((None,) * nl + (k, bn), lambda j, i: lead + (0, j))
    vmem = (2 * _nbytes((bm, k), x.dtype) + 4 * _nbytes((k, bn), F32) + 2 * _nbytes((k, bn), BF16)
            + 2 * _nbytes((bm, bn), BF16) + 3 * _nbytes((bm, bn), F32))
    return pl.pallas_call(
        _glu_kernel, name="glu_up",
        grid=(n // bn, m // bm),
        in_specs=[pl.BlockSpec((bm, k), lambda j, i: (i, 0)), wspec, wspec],
        out_specs=pl.BlockSpec((bm, bn), lambda j, i: (i, j)),
        out_shape=jax.ShapeDtypeStruct((m, n), BF16),
        scratch_shapes=[pltpu.VMEM((k, bn), BF16), pltpu.VMEM((k, bn), BF16)],
        compiler_params=_params(("arbitrary", "arbitrary"), vmem),
    )(x, wg, wu)


def _mm_res_kernel(a_ref, w_ref, x_ref, gt_ref, o_ref, wbf_ref):
    @pl.when(pl.program_id(1) == 0)
    def _():
        wbf_ref[...] = w_ref[...].astype(BF16)
    upd = jnp.dot(a_ref[...], wbf_ref[...], preferred_element_type=F32)
    o_ref[...] = x_ref[...] + gt_ref[...] * upd


def matmul_residual(a, w, lead, x, gt_seg, *, bm=512, bn=512):
    m, k = a.shape
    n = w.shape[-1]
    seg = m // gt_seg.shape[0]
    bm, bn = _tile(seg, bm), _tile(n, bn)
    nl = len(lead)
    vmem = (2 * _nbytes((bm, k), a.dtype) + 2 * _nbytes((k, bn), F32) + _nbytes((k, bn), BF16)
            + 5 * _nbytes((bm, bn), F32))
    return pl.pallas_call(
        _mm_res_kernel, name="matmul_residual",
        grid=(n // bn, m // bm),
        in_specs=[pl.BlockSpec((bm, k), lambda j, i: (i, 0)),
                  pl.BlockSpec((None,) * nl + (k, bn), lambda j, i: lead + (0, j)),
                  pl.BlockSpec((bm, bn), lambda j, i: (i, j)),
                  pl.BlockSpec((None, 1, bn), lambda j, i: (i * bm // seg, 0, j))],
        out_specs=pl.BlockSpec((bm, bn), lambda j, i: (i, j)),
        out_shape=jax.ShapeDtypeStruct((m, n), F32),
        scratch_shapes=[pltpu.VMEM((k, bn), BF16)],
        compiler_params=_params(("arbitrary", "arbitrary"), vmem),
    )(a, w, x, gt_seg)


def _merge_kernel(*refs):
    nb = N_BRANCH
    o_refs, g_refs = refs[:nb], refs[nb:2 * nb]
    wb_ref, out_ref, wbf_ref = refs[2 * nb:]

    @pl.when(pl.program_id(1) == 0)
    def _():
        wbf_ref[...] = wb_ref[...].astype(BF16)
    acc = None
    for n_ in range(nb):
        up = jnp.dot(o_refs[n_][...], wbf_ref[n_], preferred_element_type=F32)
        t = jax.nn.sigmoid(g_refs[n_][...]) * up
        acc = t if acc is None else acc + t
    out_ref[...] = acc.astype(out_ref.dtype)


def merge_branches(branch_outs, p, gate_col0, w_branch, layer, *, bm=512, bn=512):
    m, bw = branch_outs[0].shape
    dm = w_branch.shape[-1]
    bm, bn = _tile(m, bm), _tile(dm, bn)
    assert gate_col0 % bn == 0
    o_spec = pl.BlockSpec((bm, bw), lambda j, i: (i, 0))
    g_specs = [pl.BlockSpec((bm, bn), functools.partial(lambda j, i, c: (i, c + j), c=(gate_col0 + n_ * dm) // bn))
               for n_ in range(N_BRANCH)]
    vmem = (N_BRANCH * (2 * _nbytes((bm, bw), BF16) + 2 * _nbytes((bm, bn), F32) + 2 * _nbytes((bw, bn), F32)
                        + _nbytes((bw, bn), BF16)) + 6 * _nbytes((bm, bn), F32))
    return pl.pallas_call(
        _merge_kernel, name="merge_branches",
        grid=(dm // bn, m // bm),
        in_specs=[o_spec] * N_BRANCH + g_specs
                 + [pl.BlockSpec((None, N_BRANCH, bw, bn), lambda j, i: (layer, 0, 0, j))],
        out_specs=pl.BlockSpec((bm, bn), lambda j, i: (i, j)),
        out_shape=jax.ShapeDtypeStruct((m, dm), BF16),
        scratch_shapes=[pltpu.VMEM((N_BRANCH, bw, bn), BF16)],
        compiler_params=_params(("arbitrary", "arbitrary"), vmem),
    )(*branch_outs, *([p] * N_BRANCH), w_branch)


def _modnorm_kernel(*refs, with_router):
    if with_router:
        x_ref, g_ref, sc_ref, sh_ref, wr_ref, h_ref, logit_ref = refs
    else:
        x_ref, g_ref, sc_ref, sh_ref, h_ref = refs
    x = x_ref[...]
    y = x * lax.rsqrt(jnp.mean(x * x, axis=-1, keepdims=True) + EPS) * g_ref[...]
    h = y * (1 + sc_ref[...]) + sh_ref[...]
    h_ref[...] = h.astype(h_ref.dtype)
    if with_router:
        logit_ref[...] = jnp.dot(h, wr_ref[...], precision=lax.Precision.HIGHEST, preferred_element_type=F32)


def modulated_norm(x, g, sc_seg, sh_seg, w_router=None, router_layer=0, *, bm=512):
    m, d = x.shape
    seg = m // sc_seg.shape[0]
    bm = _tile(seg, bm)
    mod_spec = pl.BlockSpec((None, 1, d), lambda i: (i * bm // seg, 0, 0))
    in_specs = [pl.BlockSpec((bm, d), lambda i: (i, 0)), pl.BlockSpec((1, d), lambda i: (0, 0)), mod_spec, mod_spec]
    out_specs = [pl.BlockSpec((bm, d), lambda i: (i, 0))]
    out_shape = [jax.ShapeDtypeStruct((m, d), BF16)]
    args = [x, g.reshape(1, d), sc_seg, sh_seg]
    if w_router is not None:
        ne = w_router.shape[-1]
        in_specs.append(pl.BlockSpec((None, d, ne), lambda i: (router_layer, 0, 0)))
        out_specs.append(pl.BlockSpec((bm, ne), lambda i: (i, 0)))
        out_shape.append(jax.ShapeDtypeStruct((m, ne), F32))
        args.append(w_router)
    vmem = 8 * _nbytes((bm, d), F32) + 2 * _nbytes((d, 128), F32)
    out = pl.pallas_call(
        functools.partial(_modnorm_kernel, with_router=w_router is not None), name="modulated_norm",
        grid=(m // bm,),
        in_specs=in_specs, out_specs=out_specs, out_shape=out_shape,
        compiler_params=_params(("arbitrary",), vmem),
    )(*args)
    return out if w_router is not None else out[0]


ROPE_D = 64
HEAD_PAD = 128


def _rope_tables(t, head_dim, heads):
    half, n_freq = ROPE_D // 2, ROPE_D // 4
    rows = t // GRID_W
    r, col = jnp.meshgrid(jnp.arange(rows, dtype=F32), jnp.arange(GRID_W, dtype=F32), indexing='ij')
    inv = ROPE_THETA ** (-jnp.arange(n_freq, dtype=F32) / n_freq)
    ang = jnp.concatenate([r.reshape(-1, 1) * inv, col.reshape(-1, 1) * inv], axis=-1)
    plain = head_dim - ROPE_D
    cos = jnp.concatenate([jnp.ones((t, plain), F32), jnp.cos(ang), jnp.cos(ang)], axis=-1)
    sin = jnp.concatenate([jnp.zeros((t, plain), F32), jnp.sin(ang), jnp.sin(ang)], axis=-1)
    m_plus = np.concatenate([np.zeros(plain), -np.ones(half), np.zeros(half)]).astype(np.float32)
    m_minus = np.concatenate([np.zeros(plain), np.zeros(half), np.ones(half)]).astype(np.float32)
    tile = lambda z: jnp.tile(jnp.asarray(z), (1,) * (jnp.ndim(z) - 1) + (heads,))
    return dict(cos=tile(cos), sin=tile(sin), m_plus=tile(m_plus[None, :]), m_minus=tile(m_minus[None, :]))


def _dot_hi_lo(a, b_bf16):
    hi = a.astype(BF16)
    lo = (a - hi.astype(F32)).astype(BF16)
    return (jnp.dot(hi, b_bf16, preferred_element_type=F32) + jnp.dot(lo, b_bf16, preferred_element_type=F32))


def _headnorm_kernel(*refs, d_head, nope, use_rope, rope_blocks, out_f32):
    it = iter(refs)
    x_ref = next(it)
    extra_ref = next(it) if nope else None
    e_ref, et_ref, g_ref = next(it), next(it), next(it)
    if use_rope:
        cos_ref, sin_ref, mp_ref, mm_ref = next(it), next(it), next(it), next(it)
    obf_ref = next(it)
    x = x_ref[...]
    if nope:
        ex = extra_ref[...]
        x = jnp.concatenate([part for h in range(x.shape[1] // nope) for part in (x[:, h * nope:(h + 1) * nope], ex)],
                            axis=1)
    ss = _dot_hi_lo(x * x, e_ref[...])
    scale = lax.rsqrt(ss / d_head + EPS)
    y = x * _dot_hi_lo(scale, et_ref[...]) * g_ref[...]
    if out_f32:
        next(it)[...] = y
    if not use_rope:
        obf_ref[...] = y.astype(obf_ref.dtype)
        return
    i = pl.program_id(0)
    is_rope = (i >= rope_blocks[0]) & (i < rope_blocks[1])

    @pl.when(is_rope)
    def _():
        bw = y.shape[1]
        partner = (pltpu.roll(y, bw - ROPE_D // 2, 1) * mp_ref[...] + pltpu.roll(y, ROPE_D // 2, 1) * mm_ref[...])
        obf_ref[...] = (y * cos_ref[...] + partner * sin_ref[...]).astype(obf_ref.dtype)

    @pl.when(jnp.logical_not(is_rope))
    def _():
        obf_ref[...] = y.astype(obf_ref.dtype)


def head_norm(x, col0, width, gains, d_head, *, bw=None, extra=None, rope=None, rope_rows=None, out_f32=False, bm=512):
    m = x.shape[0]
    nope = 0
    if extra is not None:
        e = extra.shape[1]
        nope = d_head - e
        heads = width // nope
        bw_in, bw = width, heads * d_head
        wout = bw
    else:
        bw = bw or width
        bw_in, wout = bw, width
    assert col0 % bw_in == 0 and wout % bw == 0 and bw % d_head == 0
    nh = bw // d_head
    seg = np.arange(bw) // d_head
    e_mat = (seg[:, None] == np.arange(HEAD_PAD)[None, :]).astype(np.float32)
    g = jnp.tile(gains.astype(F32).reshape(-1), wout // gains.size).reshape(1, wout)
    use_rope = rope is not None
    bm = _tile(functools.reduce(math.gcd, rope_rows, m) if use_rope else m, bm if bw > 512 else 2 * bm)
    in_specs = [pl.BlockSpec((bm, bw_in), lambda i, j: (i, col0 // bw_in + j))]
    args = [x]
    if extra is not None:
        in_specs.append(pl.BlockSpec((bm, extra.shape[1]), lambda i, j: (i, 0)))
        args.append(extra)
    in_specs += [pl.BlockSpec((bw, HEAD_PAD), lambda i, j: (0, 0)), pl.BlockSpec((HEAD_PAD, bw), lambda i, j: (0, 0)),
                 pl.BlockSpec((1, bw), lambda i, j: (0, j))]
    args += [jnp.asarray(e_mat, BF16), jnp.asarray(e_mat.T, BF16), g]
    rope_blocks = (0, 0)
    if use_rope:
        r0, r1, period = rope_rows
        assert r0 % bm == 0 and r1 % bm == 0 and period % bm == 0 and nh * d_head == bw
        rope_blocks = (r0 // bm, r1 // bm)
        nper, nrb = period // bm, (r1 - r0) // bm
        tspec = pl.BlockSpec((bm, bw), lambda i, j: (jnp.clip(i - r0 // bm, 0, nrb - 1) % nper, j))
        mspec = pl.BlockSpec((1, bw), lambda i, j: (0, j))
        in_specs += [tspec, tspec, mspec, mspec]
        args += [rope['cos'], rope['sin'], rope['m_plus'], rope['m_minus']]
    out_specs = [pl.BlockSpec((bm, bw), lambda i, j: (i, j))]
    out_shape = [jax.ShapeDtypeStruct((m, wout), BF16)]
    if out_f32:
        out_specs.append(pl.BlockSpec((bm, bw), lambda i, j: (i, j)))
        out_shape.append(jax.ShapeDtypeStruct((m, wout), F32))
    vmem = 14 * _nbytes((bm, bw), F32) + 4 * _nbytes((bw, HEAD_PAD), F32)
    out = pl.pallas_call(
        functools.partial(_headnorm_kernel, d_head=d_head, nope=nope, use_rope=use_rope, rope_blocks=rope_blocks,
                          out_f32=out_f32), name="head_norm",
        grid=(m // bm, wout // bw),
        in_specs=in_specs, out_specs=out_specs, out_shape=out_shape,
        compiler_params=_params(("arbitrary", "arbitrary"), vmem),
    )(*args)
    return out if out_f32 else out[0]


S5_CHUNK = 64
S5_HALF = 256


def _s5_tables(a_re, a_im, log_dt, b_re, b_im, c_re, c_im):
    ndir, g, n = a_re.shape
    ch = b_re.shape[-1]
    gh = S5_HALF // ch
    nh = g // gh
    dt = jnp.exp(log_dt)[..., None]
    er, ei = a_re * dt, a_im * dt
    mag = jnp.exp(er)
    lb_re, lb_im = mag * jnp.cos(ei), mag * jnp.sin(ei)
    den = a_re * a_re + a_im * a_im
    xr, xi = lb_re - 1.0, lb_im
    co_re, co_im = (xr * a_re + xi * a_im) / den, (xi * a_re - xr * a_im) / den
    bb_re = co_re[..., None] * b_re - co_im[..., None] * b_im
    bb_im = co_re[..., None] * b_im + co_im[..., None] * b_re
    eye = jnp.eye(gh, dtype=F32)

    def blockdiag_in(z):
        z = z.reshape(ndir, nh, gh, n, ch)
        return jnp.einsum('dhgnj,gk->dhgjkn', z, eye).reshape(ndir, nh, gh * ch, gh * n)

    def blockdiag_out(z):
        z = z.reshape(ndir, nh, gh, ch, n)
        return jnp.einsum('dhgin,gk->dhkngi', z, eye).reshape(ndir, nh, gh * n, gh * ch)

    bmat = jnp.concatenate([blockdiag_in(bb_re), blockdiag_in(bb_im)], axis=-1).astype(BF16)
    cmat = jnp.concatenate([blockdiag_out(c_re), -blockdiag_out(c_im)], axis=-2).astype(BF16)

    def halves(z):
        return z.reshape(z.shape[:-2] + (nh, gh * n))

    steps = jnp.arange(S5_CHUNK, dtype=F32)
    expo = jnp.stack([steps, S5_CHUNK - 1 - steps])[:, :, None, None]

    def powers(sign):
        pr, pi = sign * expo * er[:, None], sign * expo * ei[:, None]
        pm = jnp.exp(pr)
        tab = jnp.concatenate([halves(pm * jnp.cos(pi)), halves(pm * jnp.sin(pi))], axis=-1)
        return jnp.swapaxes(tab, 1, 2)

    lam = jnp.concatenate([halves(lb_re), halves(lb_im)], axis=-1)[:, :, None, :]
    ti = np.arange(S5_CHUNK)
    tri = jnp.asarray(np.stack([ti[:, None] >= ti[None, :], ti[:, None] <= ti[None, :]]), BF16)
    return dict(bmat=bmat, cmat=cmat, p=powers(1.0), pinv=powers(-1.0), lam=lam, tri=tri)


def _s5_kernel(u_ref, bm_ref, cm_ref, p_ref, pinv_ref, lam_ref, tri_ref, h0_ref, y_ref, hT_ref, hs_ref, st_ref,
               *, nb, nchunks):
    d, c = pl.program_id(0), pl.program_id(3)
    L = S5_CHUNK
    half = p_ref.shape[-1] // 2

    @pl.when(c == 0)
    def _():
        st_ref[...] = h0_ref[...]

    u = u_ref[...].reshape(nb * L, u_ref.shape[-1]).astype(BF16)
    bu = jnp.dot(u, bm_ref[...], preferred_element_type=F32)
    p_re, p_im = p_ref[:, :half], p_ref[:, half:]
    q_re, q_im = pinv_ref[:, :half], pinv_ref[:, half:]
    lam_re, lam_im = lam_ref[:, :half], lam_ref[:, half:]
    tri = tri_ref[...]
    for i in range(nb):
        b_re, b_im = bu[i * L:(i + 1) * L, :half], bu[i * L:(i + 1) * L, half:]
        z = jnp.concatenate([b_re * q_re - b_im * q_im, b_re * q_im + b_im * q_re], axis=1)
        w = jnp.dot(tri, z.astype(BF16), preferred_element_type=F32)
        s_re, s_im = st_ref[i:i + 1, :half], st_ref[i:i + 1, half:]
        w_re = w[:, :half] + (lam_re * s_re - lam_im * s_im)
        w_im = w[:, half:] + (lam_re * s_im + lam_im * s_re)
        h = jnp.concatenate([p_re * w_re - p_im * w_im, p_re * w_im + p_im * w_re], axis=1)
        hs_ref[i * L:(i + 1) * L, :] = h.astype(BF16)
        st_ref[i:i + 1, :] = jnp.where(d == 0, h[L - 1:L, :], h[0:1, :])
    y = jnp.dot(hs_ref[...], cm_ref[...], preferred_element_type=F32)
    y_ref[...] = y.reshape(y_ref.shape)

    @pl.when(c == nchunks - 1)
    def _():
        hT_ref[...] = st_ref[...]


def s5_scan(p3, col0, seq0, n_seq, tabs, h0):
    _, t, _ = p3.shape
    ndir, nh, hc, ns2 = tabs['bmat'].shape
    L = S5_CHUNK
    nb = _tile(n_seq, 8)
    assert t % L == 0 and seq0 % nb == 0 and col0 % hc == 0
    nchunks = t // L
    sg0, cb0 = seq0 // nb, col0 // hc

    def tchunk(d, c):
        return c + d * (nchunks - 1 - 2 * c)

    tab = lambda r, cc: pl.BlockSpec((None, None, r, cc), lambda d, hf, sg, c: (d, hf, 0, 0))
    vmem = (4 * _nbytes((nb * L, hc), F32) + 4 * _nbytes((hc, ns2), BF16) + 4 * _nbytes((L, ns2), F32)
            + 3 * _nbytes((nb * L, ns2), F32) + 8 * _nbytes((L, ns2), F32))
    return pl.pallas_call(
        functools.partial(_s5_kernel, nb=nb, nchunks=nchunks), name="s5_scan",
        grid=(ndir, nh, n_seq // nb, nchunks),
        in_specs=[pl.BlockSpec((nb, L, hc), lambda d, hf, sg, c: (sg0 + sg, tchunk(d, c), cb0 + hf)),
                  tab(hc, ns2), tab(ns2, hc), tab(L, ns2), tab(L, ns2), tab(1, ns2),
                  pl.BlockSpec((None, L, L), lambda d, hf, sg, c: (d, 0, 0)),
                  pl.BlockSpec((None, None, nb, ns2), lambda d, hf, sg, c: (d, hf, sg, 0))],
        out_specs=[pl.BlockSpec((None, nb, L, hc), lambda d, hf, sg, c: (d, sg, tchunk(d, c), hf)),
                   pl.BlockSpec((None, None, nb, ns2), lambda d, hf, sg, c: (d, hf, sg, 0))],
        out_shape=[jax.ShapeDtypeStruct((ndir, n_seq, t, nh * hc), F32),
                   jax.ShapeDtypeStruct((ndir, nh, n_seq, ns2), F32)],
        scratch_shapes=[pltpu.VMEM((nb * L, ns2), BF16), pltpu.VMEM((nb, ns2), F32)],
        compiler_params=_params(("arbitrary",) * 4, vmem),
    )(p3, tabs['bmat'], tabs['cmat'], tabs['p'], tabs['pinv'], tabs['lam'], tabs['tri'], h0)


def _ssm_post_kernel(y_ref, u_ref, d_ref, w_ref, o_ref, wbf_ref):
    @pl.when(pl.program_id(0) == 0)
    def _():
        wbf_ref[...] = w_ref[...].astype(BF16)
    y = y_ref[0] + y_ref[1] + d_ref[...] * u_ref[...]
    g = jax.nn.gelu(y)
    z = jnp.dot(g.astype(BF16), wbf_ref[...], preferred_element_type=F32)
    o_ref[...] = (g * jax.nn.sigmoid(z)).astype(o_ref.dtype)


def ssm_post(y2, p, col0, d_vec, w_glu, layer, *, bm=1024):
    _, m, w = y2.shape
    bm = _tile(m, bm)
    assert col0 % w == 0
    vmem = 6 * _nbytes((bm, w), F32) * 2 + 3 * _nbytes((w, w), F32)
    return pl.pallas_call(
        _ssm_post_kernel, name="ssm_post",
        grid=(m // bm,),
        in_specs=[pl.BlockSpec((2, bm, w), lambda i: (0, i, 0)),
                  pl.BlockSpec((bm, w), lambda i: (i, col0 // w)),
                  pl.BlockSpec((1, w), lambda i: (0, 0)),
                  pl.BlockSpec((None, w, w), lambda i: (layer, 0, 0))],
        out_specs=pl.BlockSpec((bm, w), lambda i: (i, 0)),
        out_shape=jax.ShapeDtypeStruct((m, w), BF16),
        scratch_shapes=[pltpu.VMEM((w, w), BF16)],
        compiler_params=_params(("arbitrary",), vmem),
    )(y2, p, d_vec.reshape(1, w), w_glu)


def _is_new_expert(be_ref, i):
    return (i == 0) | (be_ref[i] != be_ref[jnp.maximum(i - 1, 0)])


def _moe_glu_kernel(be_ref, nb_ref, x_ref, wg_ref, wu_ref, o_ref, wg_bf, wu_bf):
    i = pl.program_id(1)

    @pl.when(_is_new_expert(be_ref, i))
    def _():
        wg_bf[...] = wg_ref[...].astype(BF16)
        wu_bf[...] = wu_ref[...].astype(BF16)

    @pl.when(i < nb_ref[0])
    def _():
        x = x_ref[...]
        a = jnp.dot(x, wg_bf[...], preferred_element_type=F32)
        b = jnp.dot(x, wu_bf[...], preferred_element_type=F32)
        o_ref[...] = (a * jax.nn.sigmoid(a) * b).astype(o_ref.dtype)

    @pl.when(i >= nb_ref[0])
    def _():
        o_ref[...] = jnp.zeros_like(o_ref)


def moe_glu_up(x, wg, wu, layer, block_expert, nblocks, *, bm, bn=512):
    r, k = x.shape
    n = wg.shape[-1]
    assert r % bm == 0 and n % bn == 0
    wspec = pl.BlockSpec((None, None, k, bn), lambda j, i, be, nb: (layer, be[i], 0, j))
    vmem = (2 * _nbytes((bm, k), x.dtype) + 4 * _nbytes((k, bn), F32) + 2 * _nbytes((k, bn), BF16)
            + 2 * _nbytes((bm, bn), BF16) + 3 * _nbytes((bm, bn), F32))
    return pl.pallas_call(
        _moe_glu_kernel, name="moe_glu_up",
        grid_spec=pltpu.PrefetchScalarGridSpec(
            num_scalar_prefetch=2,
            grid=(n // bn, r // bm),
            in_specs=[pl.BlockSpec((bm, k), lambda j, i, be, nb: (i, 0)), wspec, wspec],
            out_specs=pl.BlockSpec((bm, bn), lambda j, i, be, nb: (i, j)),
            scratch_shapes=[pltpu.VMEM((k, bn), BF16), pltpu.VMEM((k, bn), BF16)]),
        out_shape=jax.ShapeDtypeStruct((r, n), BF16),
        compiler_params=_params(("arbitrary", "arbitrary"), vmem),
    )(block_expert, nblocks, x, wg, wu)


def _moe_down_kernel(be_ref, nb_ref, x_ref, w_ref, o_ref, w_bf):
    i = pl.program_id(1)

    @pl.when(_is_new_expert(be_ref, i))
    def _():
        w_bf[...] = w_ref[...].astype(BF16)

    @pl.when(i < nb_ref[0])
    def _():
        o_ref[...] = jnp.dot(x_ref[...], w_bf[...], preferred_element_type=F32)

    @pl.when(i >= nb_ref[0])
    def _():
        o_ref[...] = jnp.zeros_like(o_ref)


def moe_down(x, w, layer, block_expert, nblocks, *, bm, bn=512):
    r, k = x.shape
    n = w.shape[-1]
    assert r % bm == 0 and n % bn == 0
    vmem = (2 * _nbytes((bm, k), x.dtype) + 2 * _nbytes((k, bn), F32) + _nbytes((k, bn), BF16)
            + 2 * _nbytes((bm, bn), F32))
    return pl.pallas_call(
        _moe_down_kernel, name="moe_down",
        grid_spec=pltpu.PrefetchScalarGridSpec(
            num_scalar_prefetch=2,
            grid=(n // bn, r // bm),
            in_specs=[pl.BlockSpec((bm, k), lambda j, i, be, nb: (i, 0)),
                      pl.BlockSpec((None, None, k, bn), lambda j, i, be, nb: (layer, be[i], 0, j))],
            out_specs=pl.BlockSpec((bm, bn), lambda j, i, be, nb: (i, j)),
            scratch_shapes=[pltpu.VMEM((k, bn), BF16)]),
        out_shape=jax.ShapeDtypeStruct((r, n), F32),
        compiler_params=_params(("arbitrary", "arbitrary"), vmem),
    )(block_expert, nblocks, x, w)


def _qk(q, k, scale=1.0):
    if math.frexp(scale)[0] == 0.5:
        q, scale = q * jnp.asarray(scale, q.dtype), 1.0
    s = lax.dot_general(q, k, (((1,), (1,)), ((), ())), preferred_element_type=F32)
    return s if scale == 1.0 else s * scale


def _row_sinks(sink_ref, first_head, rep, rows_per_head):
    r = lax.broadcasted_iota(jnp.int32, (rep * rows_per_head, 1), 0)
    sink = jnp.full((rep * rows_per_head, 1), sink_ref[first_head], F32)
    for g in range(1, rep):
        sink = jnp.where(r >= g * rows_per_head, sink_ref[first_head + g], sink)
    return sink


def _stack_heads(q_ref, heads, d):
    parts = [q_ref[:, h * d:(h + 1) * d] for h in heads]
    return parts[0] if len(parts) == 1 else jnp.concatenate(parts, axis=0)


def _attn_kernel(*refs, heads, kv_heads, dq, dv, scale, use_sink, use_ctx):
    refs = list(refs)
    sink_ref = refs.pop(0) if use_sink else None
    q_ref, k_ref, v_ref = refs[:3]
    kc_ref, vc_ref = (refs[3], refs[4]) if use_ctx else (None, None)
    o_ref = refs[-1]
    rep = heads // kv_heads
    bq = q_ref.shape[0]
    pieces = [None] * heads
    for j in range(kv_heads):
        q = _stack_heads(q_ref, range(j * rep, (j + 1) * rep), dq)
        s = _qk(q, k_ref[:, j * dq:(j + 1) * dq], scale)
        m = jnp.max(s, axis=-1, keepdims=True)
        if use_ctx:
            s_c = _qk(q, kc_ref[:, j * dq:(j + 1) * dq].astype(BF16), scale)
            m = jnp.maximum(m, jnp.max(s_c, axis=-1, keepdims=True))
        if use_sink:
            sink = _row_sinks(sink_ref, j * rep, rep, bq)
            m = jnp.maximum(m, sink)
        p = jnp.exp(s - m)
        l = jnp.sum(p, axis=-1, keepdims=True)
        o = jnp.dot(p.astype(BF16), v_ref[:, j * dv:(j + 1) * dv].astype(BF16), preferred_element_type=F32)
        if use_ctx:
            p_c = jnp.exp(s_c - m)
            l = l + jnp.sum(p_c, axis=-1, keepdims=True)
            o = o + jnp.dot(p_c.astype(BF16), vc_ref[:, j * dv:(j + 1) * dv].astype(BF16), preferred_element_type=F32)
        if use_sink:
            l = l + jnp.exp(sink - m)
        o = o * (1.0 / l)
        for g in range(rep):
            pieces[j * rep + g] = o[g * bq:(g + 1) * bq]
    o_ref[...] = jnp.concatenate(pieces, axis=1).astype(o_ref.dtype)


def attention(q, k, v, *, heads, kv_heads, scale, sink=None, k_ctx=None, v_ctx=None, bq=512):
    b, t, qw = q.shape
    s = k.shape[1]
    dq, dv = qw // heads, v.shape[2] // kv_heads
    rep = heads // kv_heads
    bq = _tile(t, bq)
    use_sink, use_ctx = sink is not None, k_ctx is not None
    in_specs = [pl.BlockSpec((None, bq, qw), lambda bi, qi: (bi, qi, 0)),
                pl.BlockSpec((None, s, k.shape[2]), lambda bi, qi: (bi, 0, 0)),
                pl.BlockSpec((None, s, v.shape[2]), lambda bi, qi: (bi, 0, 0))]
    args = [q, k, v]
    c = 0
    if use_ctx:
        c = k_ctx.shape[1]
        in_specs += [pl.BlockSpec((None, c, k_ctx.shape[2]), lambda bi, qi: (bi, 0, 0)),
                     pl.BlockSpec((None, c, v_ctx.shape[2]), lambda bi, qi: (bi, 0, 0))]
        args += [k_ctx, v_ctx]
    if use_sink:
        in_specs = [pl.BlockSpec(memory_space=pltpu.SMEM)] + in_specs
        args = [sink.astype(F32)] + args
    vmem = (5 * _nbytes((rep * bq, s + c), F32) + 2 * _nbytes((bq, qw), BF16) + 2 * _nbytes((s, k.shape[2] + v.shape[2]), F32)
            + 3 * _nbytes((c, k.shape[2] + v.shape[2]), F32) + 6 * _nbytes((bq, heads * dv), F32))
    return pl.pallas_call(
        functools.partial(_attn_kernel, heads=heads, kv_heads=kv_heads, dq=dq, dv=dv, scale=scale,
                          use_sink=use_sink, use_ctx=use_ctx), name="attention",
        grid=(b, t // bq),
        in_specs=in_specs,
        out_specs=pl.BlockSpec((None, bq, heads * dv), lambda bi, qi: (bi, qi, 0)),
        out_shape=jax.ShapeDtypeStruct((b, t, heads * dv), BF16),
        compiler_params=_params(("arbitrary",) * 2, vmem),
    )(*args)


def _diff_attn_kernel(*refs, heads, scale, post_scale, use_ctx):
    lam_ref, q_ref, k_ref, v_ref = refs[:4]
    kc_ref, vc_ref = (refs[4], refs[5]) if use_ctx else (None, None)
    g_ref, o_ref = refs[-2:]
    d, dv = DIFF_HD, DIFF_VD

    def attend(c0, v, vc):
        q = q_ref[:, c0:c0 + d]
        s = _qk(q, k_ref[:, c0:c0 + d], scale)
        m = jnp.max(s, axis=-1, keepdims=True)
        if use_ctx:
            s_c = _qk(q, kc_ref[:, c0:c0 + d].astype(BF16), scale)
            m = jnp.maximum(m, jnp.max(s_c, axis=-1, keepdims=True))
        p = jnp.exp(s - m)
        l = jnp.sum(p, axis=-1, keepdims=True)
        o = jnp.dot(p.astype(BF16), v, preferred_element_type=F32)
        if use_ctx:
            p_c = jnp.exp(s_c - m)
            l = l + jnp.sum(p_c, axis=-1, keepdims=True)
            o = o + jnp.dot(p_c.astype(BF16), vc, preferred_element_type=F32)
        return o * (1.0 / l)

    pieces = []
    for h in range(heads):
        v = v_ref[:, h * dv:(h + 1) * dv].astype(BF16)
        vc = vc_ref[:, h * dv:(h + 1) * dv].astype(BF16) if use_ctx else None
        o = attend(2 * h * d, v, vc) - lam_ref[h] * attend((2 * h + 1) * d, v, vc)
        o = o * lax.rsqrt(jnp.mean(o * o, axis=-1, keepdims=True) + EPS) * g_ref[...]
        pieces.append(o * post_scale)
    o_ref[...] = jnp.concatenate(pieces, axis=1).astype(o_ref.dtype)


def diff_attention(q, k, v, lam, g_out, lambda_init, *, k_ctx=None, v_ctx=None, bq=512):
    b, t, qw = q.shape
    s, vw = k.shape[1], v.shape[2]
    heads = vw // DIFF_VD
    bq = _tile(t, bq)
    use_ctx = k_ctx is not None
    in_specs = [pl.BlockSpec(memory_space=pltpu.SMEM),
                pl.BlockSpec((None, bq, qw), lambda bi, qi: (bi, qi, 0)),
                pl.BlockSpec((None, s, qw), lambda bi, qi: (bi, 0, 0)),
                pl.BlockSpec((None, s, vw), lambda bi, qi: (bi, 0, 0))]
    args = [lam.astype(F32), q, k, v]
    c = 0
    if use_ctx:
        c = k_ctx.shape[1]
        in_specs += [pl.BlockSpec((None, c, qw), lambda bi, qi: (bi, 0, 0)),
                     pl.BlockSpec((None, c, vw), lambda bi, qi: (bi, 0, 0))]
        args += [k_ctx, v_ctx]
    in_specs.append(pl.BlockSpec((1, DIFF_VD), lambda bi, qi: (0, 0)))
    args.append(g_out.astype(F32).reshape(1, DIFF_VD))
    vmem = (8 * _nbytes((bq, s + c), F32) + 2 * _nbytes((bq, qw), BF16) + 2 * _nbytes((s, qw + vw), F32)
            + 3 * _nbytes((c, qw + vw), F32) + 6 * _nbytes((bq, vw), F32))
    return pl.pallas_call(
        functools.partial(_diff_attn_kernel, heads=heads, scale=DIFF_HD ** -0.5, post_scale=1.0 - lambda_init,
                          use_ctx=use_ctx), name="diff_attention",
        grid=(b, t // bq),
        in_specs=in_specs,
        out_specs=pl.BlockSpec((None, bq, vw), lambda bi, qi: (bi, qi, 0)),
        out_shape=jax.ShapeDtypeStruct((b, t, vw), BF16),
        compiler_params=_params(("arbitrary",) * 2, vmem),
    )(*args)


def _swa_lat_kernel(sink_ref, q_ref, kc_ref, vc_ref, k0_ref, k1_ref, k2_ref, v0_ref, v1_ref, v2_ref, o_ref,
                    *, scale, nblocks):
    n = pl.program_id(1)
    rows = SWA_REP * BLOCK
    r = lax.broadcasted_iota(jnp.int32, (rows, 3 * BLOCK), 0) & (BLOCK - 1)
    c = lax.broadcasted_iota(jnp.int32, (rows, 3 * BLOCK), 1)
    kpos = (n - 1) * BLOCK + c
    mask = (c - r >= BLOCK - WINDOW) & (c - r <= BLOCK + WINDOW) & (kpos >= 0) & (kpos < nblocks * BLOCK)
    pieces = [None] * SWA_HEADS
    for j in range(SWA_KV_HEADS):
        cs = slice(j * SWA_HD, (j + 1) * SWA_HD)
        q = _stack_heads(q_ref, range(j * SWA_REP, (j + 1) * SWA_REP), SWA_HD)
        kw = jnp.concatenate([k0_ref[:, cs], k1_ref[:, cs], k2_ref[:, cs]], axis=0)
        vw = jnp.concatenate([v0_ref[:, cs], v1_ref[:, cs], v2_ref[:, cs]], axis=0).astype(BF16)
        s_ctx = _qk(q, kc_ref[:, cs].astype(BF16), scale)
        s_loc = jnp.where(mask, _qk(q, kw, scale), NEG_INF)
        sink = _row_sinks(sink_ref, j * SWA_REP, SWA_REP, BLOCK)
        m = jnp.maximum(jnp.maximum(jnp.max(s_ctx, axis=-1, keepdims=True), jnp.max(s_loc, axis=-1, keepdims=True)), sink)
        p_ctx = jnp.exp(s_ctx - m)
        p_loc = jnp.exp(s_loc - m)
        l = jnp.sum(p_ctx, axis=-1, keepdims=True) + jnp.sum(p_loc, axis=-1, keepdims=True) + jnp.exp(sink - m)
        o = (jnp.dot(p_ctx.astype(BF16), vc_ref[:, cs].astype(BF16), preferred_element_type=F32)
             + jnp.dot(p_loc.astype(BF16), vw, preferred_element_type=F32)) * (1.0 / l)
        for g in range(SWA_REP):
            pieces[j * SWA_REP + g] = o[g * BLOCK:(g + 1) * BLOCK]
    o_ref[...] = jnp.concatenate(pieces, axis=1).astype(o_ref.dtype)


def swa_latent_attention(q, k, v, k_ctx, v_ctx, sink):
    b, t, qw = q.shape
    kw = k.shape[2]
    c = k_ctx.shape[1]
    nb = t // BLOCK
    pad = ((0, 0), (BLOCK, BLOCK), (0, 0))
    kp, vp = jnp.pad(k, pad), jnp.pad(v, pad)
    band = [pl.BlockSpec((None, BLOCK, kw), functools.partial(lambda bi, ni, o: (bi, ni + o, 0), o=o)) for o in range(3)]
    ctx = pl.BlockSpec((None, c, kw), lambda bi, ni: (bi, 0, 0))
    qspec = pl.BlockSpec((None, BLOCK, qw), lambda bi, ni: (bi, ni, 0))
    vmem = 8 * _nbytes((SWA_REP * BLOCK, c + 3 * BLOCK), F32) + 6 * _nbytes((c, kw), F32) + (4 << 20)
    return pl.pallas_call(
        functools.partial(_swa_lat_kernel, scale=SWA_HD ** -0.5, nblocks=nb), name="swa_latent_attention",
        grid=(b, nb),
        in_specs=[pl.BlockSpec(memory_space=pltpu.SMEM), qspec, ctx, ctx] + band + band,
        out_specs=qspec,
        out_shape=jax.ShapeDtypeStruct(q.shape, BF16),
        compiler_params=_params(("arbitrary",) * 2, vmem),
    )(sink.astype(F32), q, k_ctx, v_ctx, kp, kp, kp, vp, vp, vp)


def _state_to_halves(st):
    b, nd, _, g, n = st.shape
    gh = S5_HALF // SSM_CH
    z = st.reshape(b, nd, 2, g // gh, gh * n)
    return jnp.transpose(z, (1, 3, 0, 2, 4)).reshape(nd, g // gh, b, 2 * gh * n)


def _halves_to_state(h, groups):
    nd, nh, b, _ = h.shape
    z = h.reshape(nd, nh, b, 2, -1)
    return jnp.transpose(z, (2, 0, 3, 1, 4)).reshape(b, nd, 2, groups, SSM_N)


def _moe_ffn(logits, h_bf, P, j, bm):
    m = h_bf.shape[0]
    probs = jax.nn.softmax(logits, axis=-1)
    top_p, top_i = lax.top_k(probs, TOP_K)
    top_p = top_p / jnp.sum(top_p, axis=-1, keepdims=True)
    flat_e = top_i.reshape(-1).astype(jnp.int32)
    npairs = flat_e.shape[0]
    order = jnp.argsort(flat_e, stable=True).astype(jnp.int32)
    counts = jnp.sum(flat_e[:, None] == jnp.arange(N_EXPERTS, dtype=jnp.int32)[None, :], axis=0).astype(jnp.int32)
    blocks_per = (counts + bm - 1) // bm
    block_end = jnp.cumsum(blocks_per)
    pad_start = (block_end - blocks_per) * bm
    start = jnp.cumsum(counts) - counts
    sorted_e = flat_e[order]
    dest = pad_start[sorted_e] + jnp.arange(npairs, dtype=jnp.int32) - start[sorted_e]
    nrows = npairs + N_EXPERTS * bm
    pos = lax.sort_key_val(order, dest)[1].reshape(m, TOP_K)
    nblk = nrows // bm
    nused = block_end[-1:]
    block_expert = jnp.minimum(jnp.sum(jnp.arange(nblk, dtype=jnp.int32)[:, None] >= block_end[None, :], axis=1),
                               N_EXPERTS - 1).astype(jnp.int32)
    row = jnp.arange(nrows, dtype=jnp.int32)
    row_e = block_expert[row // bm]
    off = row - pad_start[row_e]
    rank = jnp.clip(start[row_e] + off, 0, npairs - 1)
    row_token = jnp.where(off < counts[row_e], order[rank] // TOP_K, 0)
    block_expert = jnp.where(jnp.arange(nblk) < nused[0], block_expert, block_expert[jnp.maximum(nused[0] - 1, 0)])
    x_sorted = jnp.take(h_bf, row_token, axis=0)
    act = moe_glu_up(x_sorted, P['w_moe_gate'], P['w_moe_up'], j, block_expert, nused, bm=bm, bn=1024)
    y_sorted = moe_down(act, P['w_moe_down'], j, block_expert, nused, bm=bm)
    out = jnp.zeros((m, y_sorted.shape[1]), F32)
    for kk in range(TOP_K):
        out = out + top_p[:, kk:kk + 1] * jnp.take(y_sorted, pos[:, kk], axis=0)
    return out


SWA_COLS = (SWA_HEADS + 2 * SWA_KV_HEADS) * SWA_HD
DIFF_COLS = 2 * DIFF_HEADS * 2 * DIFF_HD + DIFF_HEADS * DIFF_VD


def kernel(x_prompt, x_sample, cache_mla_ckv, cache_mla_krope, state_ssm, cache_swa_k, cache_swa_v, cache_diff_k, cache_diff_v, c, c_ctx, w_ada, b_ada, g_norm1, g_norm2, w_in, g_mla_qlat, g_mla_kvlat, w_mla_uq, w_mla_ukv, g_mla_q, g_mla_k, ssm_a_re, ssm_a_im, ssm_log_dt, ssm_b_re, ssm_b_im, ssm_c_re, ssm_c_im, ssm_d, w_ssm_glu, g_swa_q, g_swa_k, swa_sink, g_diff_q, g_diff_k, diff_lambda, g_diff_out, w_branch, w_out, w_ffn_gate, w_ffn_up, w_ffn_down, w_router, w_moe_gate, w_moe_up, w_moe_down):
    P = dict(ssm_a_re=ssm_a_re, ssm_a_im=ssm_a_im, ssm_log_dt=ssm_log_dt, ssm_b_re=ssm_b_re, ssm_b_im=ssm_b_im,
             ssm_c_re=ssm_c_re, ssm_c_im=ssm_c_im, ssm_d=ssm_d, w_router=w_router, w_moe_gate=w_moe_gate,
             w_moe_up=w_moe_up, w_moe_down=w_moe_down)
    bp, tp, d = x_prompt.shape
    bs, ts, _ = x_sample.shape
    depth = w_in.shape[0]
    past = cache_mla_ckv.shape[2]
    mp, ms = bp * tp, bs * ts
    m = mp + ms
    w = d // 4
    groups = w // SSM_CH
    x = jnp.concatenate([x_prompt.reshape(mp, d), x_sample.reshape(ms, d)], axis=0)

    seg = math.gcd(mp, ts)
    nseg = m // seg
    seg_mod = np.concatenate([np.zeros(mp // seg, np.int32), 1 + np.repeat(np.arange(bs, dtype=np.int32), ts // seg)])
    cvec = jnp.concatenate([c_ctx[None, :], c, jnp.zeros((8 - 1 - bs, d), F32)], axis=0)
    cvec = cvec * jax.nn.sigmoid(cvec)

    def per_seg(v):
        return v[seg_mod][:, None, :]

    def gated_add(xx, gt_seg, upd):
        return (xx.reshape(nseg, seg, d) + gt_seg * upd.reshape(nseg, seg, d)).reshape(m, d)

    rope_mla = _rope_tables(ts, MLA_QK, MLA_HEADS)
    rope_swa_q = rope_diff = _rope_tables(ts, SWA_HD, SWA_HEADS)
    rope_swa_k = _rope_tables(ts, SWA_HD, SWA_KV_HEADS)
    assert DIFF_HD == SWA_HD and 2 * DIFF_HEADS == SWA_HEADS

    outs = [[] for _ in range(7)]
    for l in range(depth):
        lambda_init = 0.8 - 0.6 * math.exp(-0.3 * l)
        mods = matmul(cvec, w_ada, (l,), bm=8, bn=1024) + b_ada[l][None, :]
        sh1, sc1, gt1, sh2, sc2, gt2 = (per_seg(v) for v in jnp.split(mods, 6, axis=-1))

        h = modulated_norm(x, g_norm1[l], sc1, sh1)
        p, col0 = in_projection(h, w_in, l)
        k_rope = matmul(h, w_in, (l,), bn=IN_SHIFT_UNIT, cols=(2 * w // IN_SHIFT_UNIT, 1))[:, :MLA_ROPE]

        def tm(z, bb, tt):
            return z.reshape(bb, tt, z.shape[-1])

        lat_rows = (mp, m, ts)

        ckv_bf, c_kv = head_norm(p, col0['ckv'], w, g_mla_kvlat[l], w, out_f32=True)
        q_lat = head_norm(p, col0['q'], w, g_mla_qlat[l], w)
        q_m = head_norm(matmul(q_lat, w_mla_uq, (l,)), 0, MLA_HEADS * MLA_QK, g_mla_q[l], MLA_QK,
                        rope=rope_mla, rope_rows=lat_rows)
        ckv_all = jnp.concatenate([ckv_bf, cache_mla_ckv[:, l].reshape(bs * past, w).astype(BF16)], axis=0)
        krope_all = jnp.concatenate([k_rope, cache_mla_krope[:, l].reshape(bs * past, MLA_ROPE)], axis=0)
        w_ukv = jnp.swapaxes(w_mla_ukv[l].reshape(w, MLA_HEADS, 2, MLA_NOPE), 1, 2).reshape(w, -1)
        kv = matmul(ckv_all, w_ukv, bm=512)
        k_m = head_norm(kv, 0, MLA_HEADS * MLA_NOPE, g_mla_k[l], MLA_QK, extra=krope_all,
                        rope=rope_mla, rope_rows=lat_rows)
        v_m = kv[:, MLA_HEADS * MLA_NOPE:]
        mla = dict(heads=MLA_HEADS, kv_heads=MLA_HEADS, scale=MLA_QK ** -0.5)
        o_mla_p = attention(tm(q_m[:mp], bp, tp), tm(k_m[:mp], bp, tp), tm(v_m[:mp], bp, tp), **mla)
        o_mla_s = attention(tm(q_m[mp:], bs, ts), tm(k_m[mp:m], bs, ts), tm(v_m[mp:m], bs, ts),
                            k_ctx=tm(k_m[m:], bs, past), v_ctx=tm(v_m[m:], bs, past), **mla)
        o_mla = jnp.concatenate([o_mla_p.reshape(mp, w), o_mla_s.reshape(ms, w)], axis=0)

        tabs = _s5_tables(ssm_a_re[l], ssm_a_im[l], ssm_log_dt[l], ssm_b_re[l], ssm_b_im[l], ssm_c_re[l], ssm_c_im[l])
        ncol = p.shape[1]
        h0_p = jnp.zeros((2, w // S5_HALF, bp, 2 * (S5_HALF // SSM_CH) * SSM_N), F32)
        y_p, hT_p = s5_scan(p.reshape(m // tp, tp, ncol), col0['ssm'], 0, bp, tabs, h0_p)
        y_s, _ = s5_scan(p.reshape(m // ts, ts, ncol), col0['ssm'], mp // ts, bs, tabs, _state_to_halves(state_ssm[:, l]))
        y2 = jnp.concatenate([y_p.reshape(2, mp, w), y_s.reshape(2, ms, w)], axis=1)
        o_ssm = ssm_post(y2, p, col0['ssm'], ssm_d[l], w_ssm_glu, l)
        ssm_state = _halves_to_state(hT_p, groups)

        nq = SWA_HEADS * SWA_HD
        nk = SWA_KV_HEADS * SWA_HD
        c_swa = col0['swa']
        q_s = head_norm(p, c_swa, nq, g_swa_q[l], SWA_HD, bw=256, rope=rope_swa_q, rope_rows=lat_rows)
        k_s, k_s32 = head_norm(p, c_swa + nq, nk, g_swa_k[l], SWA_HD, rope=rope_swa_k, rope_rows=lat_rows, out_f32=True)
        v_s = p[:, c_swa + nq + nk:c_swa + nq + 2 * nk]
        o_swa_p = attention(tm(q_s[:mp], bp, tp), tm(k_s[:mp], bp, tp), tm(v_s[:mp], bp, tp), heads=SWA_HEADS,
                            kv_heads=SWA_KV_HEADS, scale=SWA_HD ** -0.5, sink=swa_sink[l], bq=256)
        o_swa_s = swa_latent_attention(tm(q_s[mp:], bs, ts), tm(k_s[mp:], bs, ts), tm(v_s[mp:], bs, ts),
                                       cache_swa_k[:, l].reshape(bs, past, nk), cache_swa_v[:, l].reshape(bs, past, nk),
                                       swa_sink[l])
        o_swa = jnp.concatenate([o_swa_p.reshape(mp, w), o_swa_s.reshape(ms, w)], axis=0)

        ndq = DIFF_HEADS * 2 * DIFF_HD
        c_diff = col0['diff']
        q_d = head_norm(p, c_diff, ndq, g_diff_q[l], DIFF_HD, bw=256, rope=rope_diff, rope_rows=lat_rows)
        k_d, k_d32 = head_norm(p, c_diff + ndq, ndq, g_diff_k[l], DIFF_HD, bw=256, rope=rope_diff, rope_rows=lat_rows,
                               out_f32=True)
        v_d = p[:, c_diff + 2 * ndq:c_diff + 2 * ndq + DIFF_HEADS * DIFF_VD]
        lp = diff_lambda[l]
        lam = jnp.exp(jnp.sum(lp[0] * lp[1], -1)) - jnp.exp(jnp.sum(lp[2] * lp[3], -1)) + lambda_init
        o_diff_p = diff_attention(tm(q_d[:mp], bp, tp), tm(k_d[:mp], bp, tp), tm(v_d[:mp], bp, tp),
                                  lam, g_diff_out[l], lambda_init, bq=256)
        o_diff_s = diff_attention(tm(q_d[mp:], bs, ts), tm(k_d[mp:], bs, ts), tm(v_d[mp:], bs, ts), lam, g_diff_out[l],
                                  lambda_init, k_ctx=cache_diff_k[:, l].reshape(bs, past, ndq),
                                  v_ctx=cache_diff_v[:, l].reshape(bs, past, DIFF_HEADS * DIFF_VD), bq=256)
        o_diff = jnp.concatenate([o_diff_p.reshape(mp, w), o_diff_s.reshape(ms, w)], axis=0)

        merged = merge_branches((o_mla, o_ssm, o_swa, o_diff), p, col0['gate'], w_branch, l, bm=1024, bn=256)
        x = matmul_residual(merged, w_out, (l,), x, gt1, bm=1024)

        j = l // 2
        if l % 2 == 0:
            h2 = modulated_norm(x, g_norm2[l], sc2, sh2)
            act = glu_up(h2, w_ffn_gate, w_ffn_up, (j,))
            x = matmul_residual(act, w_ffn_down, (j,), x, gt2)
        else:
            h2, logits = modulated_norm(x, g_norm2[l], sc2, sh2, w_router, j)
            x = gated_add(x, gt2, _moe_ffn(logits, h2, P, j, bm=256))

        outs[0].append(c_kv[:mp].reshape(bp, tp, w))
        outs[1].append(k_rope[:mp].reshape(bp, tp, MLA_ROPE))
        outs[2].append(ssm_state)
        outs[3].append(k_s32[:mp].reshape(bp, tp, SWA_KV_HEADS, SWA_HD))
        outs[4].append(v_s[:mp].reshape(bp, tp, SWA_KV_HEADS, SWA_HD))
        outs[5].append(k_d32[:mp].reshape(bp, tp, DIFF_HEADS, 2, DIFF_HD))
        outs[6].append(v_d[:mp].reshape(bp, tp, DIFF_HEADS, DIFF_VD))

    return (x[:mp].reshape(bp, tp, d), x[mp:].reshape(bs, ts, d)) + tuple(jnp.stack(o, axis=1) for o in outs)
```

```python
import functools
import math

import numpy as np
import jax
import jax.numpy as jnp
from jax import lax
from jax.experimental import pallas as pl
from jax.experimental.pallas import tpu as pltpu

F32 = jnp.float32
BF16 = jnp.bfloat16
EPS = 1e-6
NEG_INF = -1e30
ROPE_THETA = 10000.0
GRID_W = 64
BLOCK = 128
N_BRANCH = 4
MLA_HEADS = 4
MLA_NOPE = 128
MLA_ROPE = 64
MLA_QK = MLA_NOPE + MLA_ROPE
MLA_V = 128
SSM_CH = 16
SSM_N = 64
SWA_HEADS = 8
SWA_KV_HEADS = 2
SWA_REP = SWA_HEADS // SWA_KV_HEADS
SWA_HD = 64
WINDOW = 128
DIFF_HEADS = 4
DIFF_HD = 64
DIFF_VD = 2 * DIFF_HD
N_EXPERTS = 8
TOP_K = 2

V7X_VMEM_REQUEST_CAP = 56 * 1024 * 1024


def _params(semantics, vmem_bytes):
    limit = int(min(max(vmem_bytes * 5 // 4 + (2 << 20), 16 << 20), V7X_VMEM_REQUEST_CAP))
    return pltpu.CompilerParams(dimension_semantics=semantics, vmem_limit_bytes=limit)


def _tile(n, want):
    t = min(n, want)
    while n % t:
        t -= 1
    return t


def _nbytes(shape, dtype):
    return int(np.prod(shape)) * jnp.dtype(dtype).itemsize


def _mm_kernel(x_ref, w_ref, o_ref, *scratch):
    if scratch:
        wbf_ref, = scratch

        @pl.when(pl.program_id(1) == 0)
        def _():
            wbf_ref[...] = w_ref[...].astype(BF16)
        w = wbf_ref[...]
    else:
        w = w_ref[...]
    o_ref[...] = jnp.dot(x_ref[...].astype(BF16), w, preferred_element_type=F32).astype(o_ref.dtype)


def matmul(x, w, lead=(), *, bm=1024, bn=512, out_dtype=F32, cols=None):
    m, k = x.shape
    assert w.shape[-2] == k
    cb0 = 0
    if cols is None:
        n = w.shape[-1]
        bn = _tile(n, bn)
    else:
        cb0, n = cols[0], cols[1] * bn
    bm = _tile(m, bm)
    assert m % bm == 0 and n % bn == 0, (m, n, bm, bn)
    lead = tuple(lead)
    nl = len(lead)
    cast = w.dtype != BF16
    scratch = [pltpu.VMEM((k, bn), BF16)] if cast else []
    vmem = (2 * _nbytes((bm, k), x.dtype) + 2 * _nbytes((k, bn), w.dtype)
            + 2 * _nbytes((bm, bn), out_dtype) + (_nbytes((k, bn), BF16) if cast else 0))
    return pl.pallas_call(
        _mm_kernel, name="matmul",
        grid=(n // bn, m // bm),
        in_specs=[pl.BlockSpec((bm, k), lambda j, i: (i, 0)),
                  pl.BlockSpec((None,) * nl + (k, bn), lambda j, i: lead + (0, cb0 + j))],
        out_specs=pl.BlockSpec((bm, bn), lambda j, i: (i, j)),
        out_shape=jax.ShapeDtypeStruct((m, n), out_dtype),
        scratch_shapes=scratch,
        compiler_params=_params(("arbitrary", "arbitrary"), vmem),
    )(x, w)


IN_TILE = 1024
IN_SHIFT_UNIT = 128
IN_CAST_ROWS = 512


def _inproj_kernel(a_idx, b_idx, shift_ref, valid_ref, x_ref, wa_ref, wb_ref, o_ref, wbf_ref, *, shift):
    j = pl.program_id(0)
    first = pl.program_id(1) == 0

    @pl.when(first & (shift_ref[j] == 0))
    def _():
        wbf_ref[...] = wa_ref[...].astype(BF16)

    @pl.when(first & (shift_ref[j] != 0))
    def _():
        keep = lax.broadcasted_iota(jnp.int32, (IN_CAST_ROWS, IN_TILE), 1) < valid_ref[j]
        for r in range(0, wa_ref.shape[0], IN_CAST_ROWS):
            rows = slice(r, r + IN_CAST_ROWS)
            wcat = jnp.concatenate([wa_ref[rows, :], wb_ref[rows, :]], axis=1)
            wbf_ref[rows, :] = jnp.where(keep, wcat[:, shift:shift + IN_TILE], 0.0).astype(BF16)

    o_ref[...] = jnp.dot(x_ref[...], wbf_ref[...], preferred_element_type=F32)


def in_projection(h, w_in, layer, *, bm=1024):
    m, k = h.shape
    ntot = w_in.shape[-1]
    w = k // 4
    lead = 2 * w
    rest = ntot - lead - MLA_ROPE
    assert lead % IN_TILE == 0 and MLA_ROPE < IN_SHIFT_UNIT and IN_TILE % IN_SHIFT_UNIT == 0 and k % IN_CAST_ROWS == 0
    n_lead, n_rest = lead // IN_TILE, -(-rest // IN_TILE)
    last_unit = (ntot - 1) // IN_SHIFT_UNIT
    a_idx = list(range(n_lead)) + [n_lead + t for t in range(n_rest)]
    b_idx = [0] * n_lead + [min((lead + IN_TILE * (t + 1)) // IN_SHIFT_UNIT, last_unit) for t in range(n_rest)]
    shifted = [0] * n_lead + [1] * n_rest
    valid = [IN_TILE] * n_lead + [min(IN_TILE, rest - IN_TILE * t) for t in range(n_rest)]
    ntiles = len(a_idx)
    col0 = dict(q=0, ckv=w, ssm=lead, swa=lead + w, diff=lead + w + SWA_COLS, gate=lead + w + SWA_COLS + DIFF_COLS)
    bm = _tile(m, bm)
    vmem = (2 * _nbytes((bm, k), BF16) + 2 * _nbytes((k, IN_TILE + IN_SHIFT_UNIT), F32) + _nbytes((k, IN_TILE), BF16)
            + 3 * _nbytes((IN_CAST_ROWS, IN_TILE + IN_SHIFT_UNIT), F32) + 2 * _nbytes((bm, IN_TILE), F32))
    p = pl.pallas_call(
        functools.partial(_inproj_kernel, shift=MLA_ROPE), name="in_projection",
        grid_spec=pltpu.PrefetchScalarGridSpec(
            num_scalar_prefetch=4,
            grid=(ntiles, m // bm),
            in_specs=[pl.BlockSpec((bm, k), lambda j, i, a, b, s, v: (i, 0)),
                      pl.BlockSpec((None, k, IN_TILE), lambda j, i, a, b, s, v: (layer, 0, a[j])),
                      pl.BlockSpec((None, k, IN_SHIFT_UNIT), lambda j, i, a, b, s, v: (layer, 0, b[j]))],
            out_specs=pl.BlockSpec((bm, IN_TILE), lambda j, i, a, b, s, v: (i, j)),
            scratch_shapes=[pltpu.VMEM((k, IN_TILE), BF16)]),
        out_shape=jax.ShapeDtypeStruct((m, ntiles * IN_TILE), F32),
        compiler_params=_params(("arbitrary", "arbitrary"), vmem),
    )(jnp.asarray(a_idx, jnp.int32), jnp.asarray(b_idx, jnp.int32), jnp.asarray(shifted, jnp.int32),
      jnp.asarray(valid, jnp.int32), h, w_in, w_in)
    return p, col0


def _glu_kernel(x_ref, wg_ref, wu_ref, o_ref, wg_bf, wu_bf):
    @pl.when(pl.program_id(1) == 0)
    def _():
        wg_bf[...] = wg_ref[...].astype(BF16)
        wu_bf[...] = wu_ref[...].astype(BF16)
    x = x_ref[...]
    a = jnp.dot(x, wg_bf[...], preferred_element_type=F32)
    b = jnp.dot(x, wu_bf[...], preferred_element_type=F32)
    o_ref[...] = (a * jax.nn.sigmoid(a) * b).astype(o_ref.dtype)


def glu_up(x, wg, wu, lead, *, bm=1024, bn=512):
    m, k = x.shape
    n = wg.shape[-1]
    bm, bn = _tile(m, bm), _tile(n, bn)
    assert m % bm == 0 and n % bn == 0
    nl = len(lead)
    wspec = pl.BlockSpec((None,) * nl + (k, bn), lambda j, i: lead + (0, j))
    vmem = (2 * _nbytes((bm, k), x.dtype) + 4 * _nbytes((k, bn), F32) + 2 * _nbytes((k, bn), BF16)
            + 2 * _nbytes((bm, bn), BF16) + 3 * _nbytes((bm, bn), F32))
    return pl.pallas_call(
        _glu_kernel, name="glu_up",
        grid=(n // bn, m // bm),
        in_specs=[pl.BlockSpec((bm, k), lambda j, i: (i, 0)), wspec, wspec],
        out_specs=pl.BlockSpec((bm, bn), lambda j, i: (i, j)),
        out_shape=jax.ShapeDtypeStruct((m, n), BF16),
        scratch_shapes=[pltpu.VMEM((k, bn), BF16), pltpu.VMEM((k, bn), BF16)],
        compiler_params=_params(("arbitrary", "arbitrary"), vmem),
    )(x, wg, wu)


def _mm_res_kernel(a_ref, w_ref, x_ref, gt_ref, o_ref, wbf_ref):
    @pl.when(pl.program_id(1) == 0)
    def _():
        wbf_ref[...] = w_ref[...].astype(BF16)
    upd = jnp.dot(a_ref[...], wbf_ref[...], preferred_element_type=F32)
    o_ref[...] = x_ref[...] + gt_ref[...] * upd


def matmul_residual(a, w, lead, x, gt_seg, *, bm=512, bn=512):
    m, k = a.shape
    n = w.shape[-1]
    seg = m // gt_seg.shape[0]
    bm, bn = _tile(seg, bm), _tile(n, bn)
    nl = len(lead)
    vmem = (2 * _nbytes((bm, k), a.dtype) + 2 * _nbytes((k, bn), F32) + _nbytes((k, bn), BF16)
            + 5 * _nbytes((bm, bn), F32))
    return pl.pallas_call(
        _mm_res_kernel, name="matmul_residual",
        grid=(n // bn, m // bm),
        in_specs=[pl.BlockSpec((bm, k), lambda j, i: (i, 0)),
                  pl.BlockSpec((None,) * nl + (k, bn), lambda j, i: lead + (0, j)),
                  pl.BlockSpec((bm, bn), lambda j, i: (i, j)),
                  pl.BlockSpec((None, 1, bn), lambda j, i: (i * bm // seg, 0, j))],
        out_specs=pl.BlockSpec((bm, bn), lambda j, i: (i, j)),
        out_shape=jax.ShapeDtypeStruct((m, n), F32),
        scratch_shapes=[pltpu.VMEM((k, bn), BF16)],
        compiler_params=_params(("arbitrary", "arbitrary"), vmem),
    )(a, w, x, gt_seg)


def _merge_kernel(*refs, split, head_blocks):
    refs = list(refs)
    i = pl.program_id(1)
    outs = []
    for is_split in split:
        if is_split:
            head, tail = refs.pop(0), refs.pop(0)
            outs.append(jnp.where(i < head_blocks, head[...], tail[...]))
        else:
            outs.append(refs.pop(0)[...])
    g_refs = refs[:N_BRANCH]
    wb_ref, out_ref, wbf_ref = refs[N_BRANCH:]

    @pl.when(i == 0)
    def _():
        wbf_ref[...] = wb_ref[...].astype(BF16)
    acc = None
    for n_ in range(N_BRANCH):
        up = jnp.dot(outs[n_], wbf_ref[n_], preferred_element_type=F32)
        t = jax.nn.sigmoid(g_refs[n_][...]) * up
        acc = t if acc is None else acc + t
    out_ref[...] = acc.astype(out_ref.dtype)


def merge_branches(branch_outs, p, gate_col0, w_branch, layer, *, bm=512, bn=512):
    m = p.shape[0]
    split = tuple(isinstance(o, tuple) for o in branch_outs)
    heads = {o[0].shape[0] for o in branch_outs if isinstance(o, tuple)}
    assert len(heads) <= 1
    m_head = heads.pop() if heads else m
    bw = (branch_outs[0][0] if split[0] else branch_outs[0]).shape[1]
    dm = w_branch.shape[-1]
    bm, bn = _tile(math.gcd(m, m_head), bm), _tile(dm, bn)
    assert gate_col0 % bn == 0
    hb = m_head // bm
    o_specs, o_args = [], []
    for o in branch_outs:
        if isinstance(o, tuple):
            o_specs += [pl.BlockSpec((bm, bw), lambda j, i: (jnp.minimum(i, hb - 1), 0)),
                        pl.BlockSpec((bm, bw), lambda j, i: (jnp.maximum(i - hb, 0), 0))]
            o_args += list(o)
        else:
            o_specs.append(pl.BlockSpec((bm, bw), lambda j, i: (i, 0)))
            o_args.append(o)
    g_specs = [pl.BlockSpec((bm, bn), functools.partial(lambda j, i, c: (i, c + j), c=(gate_col0 + n_ * dm) // bn))
               for n_ in range(N_BRANCH)]
    vmem = (N_BRANCH * (5 * _nbytes((bm, bw), BF16) + 2 * _nbytes((bm, bn), F32) + 2 * _nbytes((bw, bn), F32)
                        + _nbytes((bw, bn), BF16)) + 6 * _nbytes((bm, bn), F32))
    return pl.pallas_call(
        functools.partial(_merge_kernel, split=split, head_blocks=hb), name="merge_branches",
        grid=(dm // bn, m // bm),
        in_specs=o_specs + g_specs + [pl.BlockSpec((None, N_BRANCH, bw, bn), lambda j, i: (layer, 0, 0, j))],
        out_specs=pl.BlockSpec((bm, bn), lambda j, i: (i, j)),
        out_shape=jax.ShapeDtypeStruct((m, dm), BF16),
        scratch_shapes=[pltpu.VMEM((N_BRANCH, bw, bn), BF16)],
        compiler_params=_params(("arbitrary", "arbitrary"), vmem),
    )(*o_args, *([p] * N_BRANCH), w_branch)


def _modnorm_kernel(*refs, with_router):
    if with_router:
        x_ref, g_ref, sc_ref, sh_ref, wr_ref, h_ref, logit_ref = refs
    else:
        x_ref, g_ref, sc_ref, sh_ref, h_ref = refs
    x = x_ref[...]
    y = x * lax.rsqrt(jnp.mean(x * x, axis=-1, keepdims=True) + EPS) * g_ref[...]
    h = y * (1 + sc_ref[...]) + sh_ref[...]
    h_ref[...] = h.astype(h_ref.dtype)
    if with_router:
        logit_ref[...] = jnp.dot(h, wr_ref[...], precision=lax.Precision.HIGHEST, preferred_element_type=F32)


def modulated_norm(x, g, sc_seg, sh_seg, w_router=None, router_layer=0, *, bm=512):
    m, d = x.shape
    seg = m // sc_seg.shape[0]
    bm = _tile(seg, bm)
    mod_spec = pl.BlockSpec((None, 1, d), lambda i: (i * bm // seg, 0, 0))
    in_specs = [pl.BlockSpec((bm, d), lambda i: (i, 0)), pl.BlockSpec((1, d), lambda i: (0, 0)), mod_spec, mod_spec]
    out_specs = [pl.BlockSpec((bm, d), lambda i: (i, 0))]
    out_shape = [jax.ShapeDtypeStruct((m, d), BF16)]
    args = [x, g.reshape(1, d), sc_seg, sh_seg]
    if w_router is not None:
        ne = w_router.shape[-1]
        in_specs.append(pl.BlockSpec((None, d, ne), lambda i: (router_layer, 0, 0)))
        out_specs.append(pl.BlockSpec((bm, ne), lambda i: (i, 0)))
        out_shape.append(jax.ShapeDtypeStruct((m, ne), F32))
        args.append(w_router)
    vmem = 8 * _nbytes((bm, d), F32) + 2 * _nbytes((d, 128), F32)
    out = pl.pallas_call(
        functools.partial(_modnorm_kernel, with_router=w_router is not None), name="modulated_norm",
        grid=(m // bm,),
        in_specs=in_specs, out_specs=out_specs, out_shape=out_shape,
        compiler_params=_params(("arbitrary",), vmem),
    )(*args)
    return out if w_router is not None else out[0]


ROPE_D = 64
HEAD_PAD = 128


def _rope_tables(t, head_dim, heads):
    half, n_freq = ROPE_D // 2, ROPE_D // 4
    rows = t // GRID_W
    r, col = jnp.meshgrid(jnp.arange(rows, dtype=F32), jnp.arange(GRID_W, dtype=F32), indexing='ij')
    inv = ROPE_THETA ** (-jnp.arange(n_freq, dtype=F32) / n_freq)
    ang = jnp.concatenate([r.reshape(-1, 1) * inv, col.reshape(-1, 1) * inv], axis=-1)
    plain = head_dim - ROPE_D
    cos = jnp.concatenate([jnp.ones((t, plain), F32), jnp.cos(ang), jnp.cos(ang)], axis=-1)
    sin = jnp.concatenate([jnp.zeros((t, plain), F32), jnp.sin(ang), jnp.sin(ang)], axis=-1)
    m_plus = np.concatenate([np.zeros(plain), -np.ones(half), np.zeros(half)]).astype(np.float32)
    m_minus = np.concatenate([np.zeros(plain), np.zeros(half), np.ones(half)]).astype(np.float32)
    tile = lambda z: jnp.tile(jnp.asarray(z), (1,) * (jnp.ndim(z) - 1) + (heads,))
    return dict(cos=tile(cos), sin=tile(sin), m_plus=tile(m_plus[None, :]), m_minus=tile(m_minus[None, :]))


def _dot_hi_lo(a, b_bf16):
    hi = a.astype(BF16)
    lo = (a - hi.astype(F32)).astype(BF16)
    return (jnp.dot(hi, b_bf16, preferred_element_type=F32) + jnp.dot(lo, b_bf16, preferred_element_type=F32))


def _headnorm_kernel(*refs, d_head, nope, use_rope, rope_blocks, out_f32):
    it = iter(refs)
    x_ref = next(it)
    extra_ref = next(it) if nope else None
    e_ref, et_ref, g_ref = next(it), next(it), next(it)
    if use_rope:
        cos_ref, sin_ref, mp_ref, mm_ref = next(it), next(it), next(it), next(it)
    obf_ref = next(it)
    x = x_ref[...]
    if nope:
        ex = extra_ref[...]
        x = jnp.concatenate([part for h in range(x.shape[1] // nope) for part in (x[:, h * nope:(h + 1) * nope], ex)],
                            axis=1)
    ss = _dot_hi_lo(x * x, e_ref[...])
    scale = lax.rsqrt(ss / d_head + EPS)
    y = x * _dot_hi_lo(scale, et_ref[...]) * g_ref[...]
    if out_f32:
        next(it)[...] = y
    if not use_rope:
        obf_ref[...] = y.astype(obf_ref.dtype)
        return
    i = pl.program_id(0)
    is_rope = (i >= rope_blocks[0]) & (i < rope_blocks[1])

    @pl.when(is_rope)
    def _():
        bw = y.shape[1]
        partner = (pltpu.roll(y, bw - ROPE_D // 2, 1) * mp_ref[...] + pltpu.roll(y, ROPE_D // 2, 1) * mm_ref[...])
        obf_ref[...] = (y * cos_ref[...] + partner * sin_ref[...]).astype(obf_ref.dtype)

    @pl.when(jnp.logical_not(is_rope))
    def _():
        obf_ref[...] = y.astype(obf_ref.dtype)


def head_norm(x, col0, width, gains, d_head, *, bw=None, extra=None, rope=None, rope_rows=None, out_f32=False, bm=512):
    m = x.shape[0]
    nope = 0
    if extra is not None:
        e = extra.shape[1]
        nope = d_head - e
        heads = width // nope
        bw_in, bw = width, heads * d_head
        wout = bw
    else:
        bw = bw or width
        bw_in, wout = bw, width
    assert col0 % bw_in == 0 and wout % bw == 0 and bw % d_head == 0
    nh = bw // d_head
    seg = np.arange(bw) // d_head
    e_mat = (seg[:, None] == np.arange(HEAD_PAD)[None, :]).astype(np.float32)
    g = jnp.tile(gains.astype(F32).reshape(-1), wout // gains.size).reshape(1, wout)
    use_rope = rope is not None
    bm = _tile(functools.reduce(math.gcd, rope_rows, m) if use_rope else m, bm if bw > 512 else 2 * bm)
    in_specs = [pl.BlockSpec((bm, bw_in), lambda i, j: (i, col0 // bw_in + j))]
    args = [x]
    if extra is not None:
        in_specs.append(pl.BlockSpec((bm, extra.shape[1]), lambda i, j: (i, 0)))
        args.append(extra)
    in_specs += [pl.BlockSpec((bw, HEAD_PAD), lambda i, j: (0, 0)), pl.BlockSpec((HEAD_PAD, bw), lambda i, j: (0, 0)),
                 pl.BlockSpec((1, bw), lambda i, j: (0, j))]
    args += [jnp.asarray(e_mat, BF16), jnp.asarray(e_mat.T, BF16), g]
    rope_blocks = (0, 0)
    if use_rope:
        r0, r1, period = rope_rows
        assert r0 % bm == 0 and r1 % bm == 0 and period % bm == 0 and nh * d_head == bw
        rope_blocks = (r0 // bm, r1 // bm)
        nper, nrb = period // bm, (r1 - r0) // bm
        tspec = pl.BlockSpec((bm, bw), lambda i, j: (jnp.clip(i - r0 // bm, 0, nrb - 1) % nper, j))
        mspec = pl.BlockSpec((1, bw), lambda i, j: (0, j))
        in_specs += [tspec, tspec, mspec, mspec]
        args += [rope['cos'], rope['sin'], rope['m_plus'], rope['m_minus']]
    out_specs = [pl.BlockSpec((bm, bw), lambda i, j: (i, j))]
    out_shape = [jax.ShapeDtypeStruct((m, wout), BF16)]
    if out_f32:
        out_specs.append(pl.BlockSpec((bm, bw), lambda i, j: (i, j)))
        out_shape.append(jax.ShapeDtypeStruct((m, wout), F32))
    vmem = 14 * _nbytes((bm, bw), F32) + 4 * _nbytes((bw, HEAD_PAD), F32)
    out = pl.pallas_call(
        functools.partial(_headnorm_kernel, d_head=d_head, nope=nope, use_rope=use_rope, rope_blocks=rope_blocks,
                          out_f32=out_f32), name="head_norm",
        grid=(m // bm, wout // bw),
        in_specs=in_specs, out_specs=out_specs, out_shape=out_shape,
        compiler_params=_params(("arbitrary", "arbitrary"), vmem),
    )(*args)
    return out if out_f32 else out[0]


S5_CHUNK = 64
S5_HALF = 256


def _s5_tables(a_re, a_im, log_dt, b_re, b_im, c_re, c_im):
    ndir, g, n = a_re.shape
    ch = b_re.shape[-1]
    gh = S5_HALF // ch
    nh = g // gh
    dt = jnp.exp(log_dt)[..., None]
    er, ei = a_re * dt, a_im * dt
    mag = jnp.exp(er)
    lb_re, lb_im = mag * jnp.cos(ei), mag * jnp.sin(ei)
    den = a_re * a_re + a_im * a_im
    xr, xi = lb_re - 1.0, lb_im
    co_re, co_im = (xr * a_re + xi * a_im) / den, (xi * a_re - xr * a_im) / den
    bb_re = co_re[..., None] * b_re - co_im[..., None] * b_im
    bb_im = co_re[..., None] * b_im + co_im[..., None] * b_re
    eye = jnp.eye(gh, dtype=F32)

    def blockdiag_in(z):
        z = z.reshape(ndir, nh, gh, n, ch)
        return jnp.einsum('dhgnj,gk->dhgjkn', z, eye).reshape(ndir, nh, gh * ch, gh * n)

    def blockdiag_out(z):
        z = z.reshape(ndir, nh, gh, ch, n)
        return jnp.einsum('dhgin,gk->dhkngi', z, eye).reshape(ndir, nh, gh * n, gh * ch)

    bmat = jnp.concatenate([blockdiag_in(bb_re), blockdiag_in(bb_im)], axis=-1).astype(BF16)
    cmat = jnp.concatenate([blockdiag_out(c_re), -blockdiag_out(c_im)], axis=-2).astype(BF16)

    def halves(z):
        return z.reshape(z.shape[:-2] + (nh, gh * n))

    steps = jnp.arange(S5_CHUNK, dtype=F32)
    expo = jnp.stack([steps, S5_CHUNK - 1 - steps])[:, :, None, None]

    def powers(sign):
        pr, pi = sign * expo * er[:, None], sign * expo * ei[:, None]
        pm = jnp.exp(pr)
        tab = jnp.concatenate([halves(pm * jnp.cos(pi)), halves(pm * jnp.sin(pi))], axis=-1)
        return jnp.swapaxes(tab, 1, 2)

    lam = jnp.concatenate([halves(lb_re), halves(lb_im)], axis=-1)[:, :, None, :]
    ti = np.arange(S5_CHUNK)
    tri = jnp.asarray(np.stack([ti[:, None] >= ti[None, :], ti[:, None] <= ti[None, :]]), BF16)
    return dict(bmat=bmat, cmat=cmat, p=powers(1.0), pinv=powers(-1.0), lam=lam, tri=tri)


def _s5_kernel(u_ref, bm_ref, cm_ref, p_ref, pinv_ref, lam_ref, tri_ref, h0_ref, y_ref, hT_ref, hs_ref, st_ref,
               *, nb, nchunks):
    d, c = pl.program_id(0), pl.program_id(3)
    L = S5_CHUNK
    half = p_ref.shape[-1] // 2

    @pl.when(c == 0)
    def _():
        st_ref[...] = h0_ref[...]

    u = u_ref[...].reshape(nb * L, u_ref.shape[-1]).astype(BF16)
    bu = jnp.dot(u, bm_ref[...], preferred_element_type=F32)
    p_re, p_im = p_ref[:, :half], p_ref[:, half:]
    q_re, q_im = pinv_ref[:, :half], pinv_ref[:, half:]
    lam_re, lam_im = lam_ref[:, :half], lam_ref[:, half:]
    tri = tri_ref[...]
    for i in range(nb):
        b_re, b_im = bu[i * L:(i + 1) * L, :half], bu[i * L:(i + 1) * L, half:]
        z = jnp.concatenate([b_re * q_re - b_im * q_im, b_re * q_im + b_im * q_re], axis=1)
        w = jnp.dot(tri, z.astype(BF16), preferred_element_type=F32)
        s_re, s_im = st_ref[i:i + 1, :half], st_ref[i:i + 1, half:]
        w_re = w[:, :half] + (lam_re * s_re - lam_im * s_im)
        w_im = w[:, half:] + (lam_re * s_im + lam_im * s_re)
        h = jnp.concatenate([p_re * w_re - p_im * w_im, p_re * w_im + p_im * w_re], axis=1)
        hs_ref[i * L:(i + 1) * L, :] = h.astype(BF16)
        st_ref[i:i + 1, :] = jnp.where(d == 0, h[L - 1:L, :], h[0:1, :])
    y = jnp.dot(hs_ref[...], cm_ref[...], preferred_element_type=F32)
    y_ref[...] = y.reshape(y_ref.shape)

    @pl.when(c == nchunks - 1)
    def _():
        hT_ref[...] = st_ref[...]


def s5_scan(p3, col0, seq0, n_seq, tabs, h0):
    _, t, _ = p3.shape
    ndir, nh, hc, ns2 = tabs['bmat'].shape
    L = S5_CHUNK
    nb = _tile(n_seq, 8)
    assert t % L == 0 and seq0 % nb == 0 and col0 % hc == 0
    nchunks = t // L
    sg0, cb0 = seq0 // nb, col0 // hc

    def tchunk(d, c):
        return c + d * (nchunks - 1 - 2 * c)

    tab = lambda r, cc: pl.BlockSpec((None, None, r, cc), lambda d, hf, sg, c: (d, hf, 0, 0))
    vmem = (4 * _nbytes((nb * L, hc), F32) + 4 * _nbytes((hc, ns2), BF16) + 4 * _nbytes((L, ns2), F32)
            + 3 * _nbytes((nb * L, ns2), F32) + 8 * _nbytes((L, ns2), F32))
    return pl.pallas_call(
        functools.partial(_s5_kernel, nb=nb, nchunks=nchunks), name="s5_scan",
        grid=(ndir, nh, n_seq // nb, nchunks),
        in_specs=[pl.BlockSpec((nb, L, hc), lambda d, hf, sg, c: (sg0 + sg, tchunk(d, c), cb0 + hf)),
                  tab(hc, ns2), tab(ns2, hc), tab(L, ns2), tab(L, ns2), tab(1, ns2),
                  pl.BlockSpec((None, L, L), lambda d, hf, sg, c: (d, 0, 0)),
                  pl.BlockSpec((None, None, nb, ns2), lambda d, hf, sg, c: (d, hf, sg, 0))],
        out_specs=[pl.BlockSpec((None, nb, L, hc), lambda d, hf, sg, c: (d, sg, tchunk(d, c), hf)),
                   pl.BlockSpec((None, None, nb, ns2), lambda d, hf, sg, c: (d, hf, sg, 0))],
        out_shape=[jax.ShapeDtypeStruct((ndir, n_seq, t, nh * hc), F32),
                   jax.ShapeDtypeStruct((ndir, nh, n_seq, ns2), F32)],
        scratch_shapes=[pltpu.VMEM((nb * L, ns2), BF16), pltpu.VMEM((nb, ns2), F32)],
        compiler_params=_params(("arbitrary",) * 4, vmem),
    )(p3, tabs['bmat'], tabs['cmat'], tabs['p'], tabs['pinv'], tabs['lam'], tabs['tri'], h0)


def _ssm_post_kernel(yh_ref, yt_ref, u_ref, d_ref, w_ref, o_ref, wbf_ref, *, head_blocks):
    i = pl.program_id(0)

    @pl.when(i == 0)
    def _():
        wbf_ref[...] = w_ref[...].astype(BF16)
    y2 = jnp.where(i < head_blocks, yh_ref[...], yt_ref[...])
    y = y2[0] + y2[1] + d_ref[...] * u_ref[...]
    g = jax.nn.gelu(y)
    z = jnp.dot(g.astype(BF16), wbf_ref[...], preferred_element_type=F32)
    o_ref[...] = (g * jax.nn.sigmoid(z)).astype(o_ref.dtype)


def ssm_post(y_head, y_tail, p, col0, d_vec, w_glu, layer, *, bm=1024):
    _, m_head, w = y_head.shape
    m = m_head + y_tail.shape[1]
    bm = _tile(math.gcd(m, m_head), bm)
    hb = m_head // bm
    assert col0 % w == 0
    vmem = 10 * _nbytes((bm, w), F32) * 2 + 3 * _nbytes((w, w), F32)
    return pl.pallas_call(
        functools.partial(_ssm_post_kernel, head_blocks=hb), name="ssm_post",
        grid=(m // bm,),
        in_specs=[pl.BlockSpec((2, bm, w), lambda i: (0, jnp.minimum(i, hb - 1), 0)),
                  pl.BlockSpec((2, bm, w), lambda i: (0, jnp.maximum(i - hb, 0), 0)),
                  pl.BlockSpec((bm, w), lambda i: (i, col0 // w)),
                  pl.BlockSpec((1, w), lambda i: (0, 0)),
                  pl.BlockSpec((None, w, w), lambda i: (layer, 0, 0))],
        out_specs=pl.BlockSpec((bm, w), lambda i: (i, 0)),
        out_shape=jax.ShapeDtypeStruct((m, w), BF16),
        scratch_shapes=[pltpu.VMEM((w, w), BF16)],
        compiler_params=_params(("arbitrary",), vmem),
    )(y_head, y_tail, p, d_vec.reshape(1, w), w_glu)


def _is_new_expert(be_ref, i):
    return (i == 0) | (be_ref[i] != be_ref[jnp.maximum(i - 1, 0)])


def _moe_glu_kernel(be_ref, nb_ref, x_ref, wg_ref, wu_ref, o_ref, wg_bf, wu_bf):
    i = pl.program_id(1)

    @pl.when(_is_new_expert(be_ref, i))
    def _():
        wg_bf[...] = wg_ref[...].astype(BF16)
        wu_bf[...] = wu_ref[...].astype(BF16)

    @pl.when(i < nb_ref[0])
    def _():
        x = x_ref[...]
        a = jnp.dot(x, wg_bf[...], preferred_element_type=F32)
        b = jnp.dot(x, wu_bf[...], preferred_element_type=F32)
        o_ref[...] = (a * jax.nn.sigmoid(a) * b).astype(o_ref.dtype)

    @pl.when(i >= nb_ref[0])
    def _():
        o_ref[...] = jnp.zeros_like(o_ref)


def moe_glu_up(x, wg, wu, layer, block_expert, nblocks, *, bm, bn=512):
    r, k = x.shape
    n = wg.shape[-1]
    assert r % bm == 0 and n % bn == 0
    wspec = pl.BlockSpec((None, None, k, bn), lambda j, i, be, nb: (layer, be[i], 0, j))
    vmem = (2 * _nbytes((bm, k), x.dtype) + 4 * _nbytes((k, bn), F32) + 2 * _nbytes((k, bn), BF16)
            + 2 * _nbytes((bm, bn), BF16) + 3 * _nbytes((bm, bn), F32))
    return pl.pallas_call(
        _moe_glu_kernel, name="moe_glu_up",
        grid_spec=pltpu.PrefetchScalarGridSpec(
            num_scalar_prefetch=2,
            grid=(n // bn, r // bm),
            in_specs=[pl.BlockSpec((bm, k), lambda j, i, be, nb: (i, 0)), wspec, wspec],
            out_specs=pl.BlockSpec((bm, bn), lambda j, i, be, nb: (i, j)),
            scratch_shapes=[pltpu.VMEM((k, bn), BF16), pltpu.VMEM((k, bn), BF16)]),
        out_shape=jax.ShapeDtypeStruct((r, n), BF16),
        compiler_params=_params(("arbitrary", "arbitrary"), vmem),
    )(block_expert, nblocks, x, wg, wu)


def _moe_down_kernel(be_ref, nb_ref, x_ref, w_ref, o_ref, w_bf):
    i = pl.program_id(1)

    @pl.when(_is_new_expert(be_ref, i))
    def _():
        w_bf[...] = w_ref[...].astype(BF16)

    @pl.when(i < nb_ref[0])
    def _():
        o_ref[...] = jnp.dot(x_ref[...], w_bf[...], preferred_element_type=F32)

    @pl.when(i >= nb_ref[0])
    def _():
        o_ref[...] = jnp.zeros_like(o_ref)


def moe_down(x, w, layer, block_expert, nblocks, *, bm, bn=512):
    r, k = x.shape
    n = w.shape[-1]
    assert r % bm == 0 and n % bn == 0
    vmem = (2 * _nbytes((bm, k), x.dtype) + 2 * _nbytes((k, bn), F32) + _nbytes((k, bn), BF16)
            + 2 * _nbytes((bm, bn), F32))
    return pl.pallas_call(
        _moe_down_kernel, name="moe_down",
        grid_spec=pltpu.PrefetchScalarGridSpec(
            num_scalar_prefetch=2,
            grid=(n // bn, r // bm),
            in_specs=[pl.BlockSpec((bm, k), lambda j, i, be, nb: (i, 0)),
                      pl.BlockSpec((None, None, k, bn), lambda j, i, be, nb: (layer, be[i], 0, j))],
            out_specs=pl.BlockSpec((bm, bn), lambda j, i, be, nb: (i, j)),
            scratch_shapes=[pltpu.VMEM((k, bn), BF16)]),
        out_shape=jax.ShapeDtypeStruct((r, n), F32),
        compiler_params=_params(("arbitrary", "arbitrary"), vmem),
    )(block_expert, nblocks, x, w)


def _qk(q, k, scale=1.0):
    if math.frexp(scale)[0] == 0.5:
        q, scale = q * jnp.asarray(scale, q.dtype), 1.0
    s = lax.dot_general(q, k, (((1,), (1,)), ((), ())), preferred_element_type=F32)
    return s if scale == 1.0 else s * scale


def _row_sinks(sink_ref, first_head, rep, rows_per_head):
    r = lax.broadcasted_iota(jnp.int32, (rep * rows_per_head, 1), 0)
    sink = jnp.full((rep * rows_per_head, 1), sink_ref[first_head], F32)
    for g in range(1, rep):
        sink = jnp.where(r >= g * rows_per_head, sink_ref[first_head + g], sink)
    return sink


def _stack_heads(q_ref, heads, d):
    parts = [q_ref[:, h * d:(h + 1) * d] for h in heads]
    return parts[0] if len(parts) == 1 else jnp.concatenate(parts, axis=0)


def _attn_kernel(*refs, heads, kv_heads, dq, dv, scale, use_sink, use_ctx):
    refs = list(refs)
    sink_ref = refs.pop(0) if use_sink else None
    q_ref, k_ref, v_ref = refs[:3]
    kc_ref, vc_ref = (refs[3], refs[4]) if use_ctx else (None, None)
    o_ref = refs[-1]
    rep = heads // kv_heads
    bq = q_ref.shape[0]
    pieces = [None] * heads
    for j in range(kv_heads):
        q = _stack_heads(q_ref, range(j * rep, (j + 1) * rep), dq)
        s = _qk(q, k_ref[:, j * dq:(j + 1) * dq], scale)
        m = jnp.max(s, axis=-1, keepdims=True)
        if use_ctx:
            s_c = _qk(q, kc_ref[:, j * dq:(j + 1) * dq].astype(BF16), scale)
            m = jnp.maximum(m, jnp.max(s_c, axis=-1, keepdims=True))
        if use_sink:
            sink = _row_sinks(sink_ref, j * rep, rep, bq)
            m = jnp.maximum(m, sink)
        p = jnp.exp(s - m)
        l = jnp.sum(p, axis=-1, keepdims=True)
        o = jnp.dot(p.astype(BF16), v_ref[:, j * dv:(j + 1) * dv].astype(BF16), preferred_element_type=F32)
        if use_ctx:
            p_c = jnp.exp(s_c - m)
            l = l + jnp.sum(p_c, axis=-1, keepdims=True)
            o = o + jnp.dot(p_c.astype(BF16), vc_ref[:, j * dv:(j + 1) * dv].astype(BF16), preferred_element_type=F32)
        if use_sink:
            l = l + jnp.exp(sink - m)
        o = o * (1.0 / l)
        for g in range(rep):
            pieces[j * rep + g] = o[g * bq:(g + 1) * bq]
    o_ref[...] = jnp.concatenate(pieces, axis=1).astype(o_ref.dtype)


def attention(q, k, v, *, heads, kv_heads, scale, sink=None, k_ctx=None, v_ctx=None, bq=512):
    b, t, qw = q.shape
    s = k.shape[1]
    dq, dv = qw // heads, v.shape[2] // kv_heads
    rep = heads // kv_heads
    bq = _tile(t, bq)
    use_sink, use_ctx = sink is not None, k_ctx is not None
    in_specs = [pl.BlockSpec((None, bq, qw), lambda bi, qi: (bi, qi, 0)),
                pl.BlockSpec((None, s, k.shape[2]), lambda bi, qi: (bi, 0, 0)),
                pl.BlockSpec((None, s, v.shape[2]), lambda bi, qi: (bi, 0, 0))]
    args = [q, k, v]
    c = 0
    if use_ctx:
        c = k_ctx.shape[1]
        in_specs += [pl.BlockSpec((None, c, k_ctx.shape[2]), lambda bi, qi: (bi, 0, 0)),
                     pl.BlockSpec((None, c, v_ctx.shape[2]), lambda bi, qi: (bi, 0, 0))]
        args += [k_ctx, v_ctx]
    if use_sink:
        in_specs = [pl.BlockSpec(memory_space=pltpu.SMEM)] + in_specs
        args = [sink.astype(F32)] + args
    vmem = (5 * _nbytes((rep * bq, s + c), F32) + 2 * _nbytes((bq, qw), BF16) + 2 * _nbytes((s, k.shape[2] + v.shape[2]), F32)
            + 3 * _nbytes((c, k.shape[2] + v.shape[2]), F32) + 6 * _nbytes((bq, heads * dv), F32))
    return pl.pallas_call(
        functools.partial(_attn_kernel, heads=heads, kv_heads=kv_heads, dq=dq, dv=dv, scale=scale,
                          use_sink=use_sink, use_ctx=use_ctx), name="attention",
        grid=(b, t // bq),
        in_specs=in_specs,
        out_specs=pl.BlockSpec((None, bq, heads * dv), lambda bi, qi: (bi, qi, 0)),
        out_shape=jax.ShapeDtypeStruct((b, t, heads * dv), BF16),
        compiler_params=_params(("arbitrary",) * 2, vmem),
    )(*args)


def _diff_attn_kernel(*refs, heads, scale, post_scale, use_ctx):
    lam_ref, q_ref, k_ref, v_ref = refs[:4]
    kc_ref, vc_ref = (refs[4], refs[5]) if use_ctx else (None, None)
    g_ref, o_ref = refs[-2:]
    d, dv = DIFF_HD, DIFF_VD

    def attend(c0, v, vc):
        q = q_ref[:, c0:c0 + d]
        s = _qk(q, k_ref[:, c0:c0 + d], scale)
        m = jnp.max(s, axis=-1, keepdims=True)
        if use_ctx:
            s_c = _qk(q, kc_ref[:, c0:c0 + d].astype(BF16), scale)
            m = jnp.maximum(m, jnp.max(s_c, axis=-1, keepdims=True))
        p = jnp.exp(s - m)
        l = jnp.sum(p, axis=-1, keepdims=True)
        o = jnp.dot(p.astype(BF16), v, preferred_element_type=F32)
        if use_ctx:
            p_c = jnp.exp(s_c - m)
            l = l + jnp.sum(p_c, axis=-1, keepdims=True)
            o = o + jnp.dot(p_c.astype(BF16), vc, preferred_element_type=F32)
        return o * (1.0 / l)

    pieces = []
    for h in range(heads):
        v = v_ref[:, h * dv:(h + 1) * dv].astype(BF16)
        vc = vc_ref[:, h * dv:(h + 1) * dv].astype(BF16) if use_ctx else None
        o = attend(2 * h * d, v, vc) - lam_ref[h] * attend((2 * h + 1) * d, v, vc)
        o = o * lax.rsqrt(jnp.mean(o * o, axis=-1, keepdims=True) + EPS) * g_ref[...]
        pieces.append(o * post_scale)
    o_ref[...] = jnp.concatenate(pieces, axis=1).astype(o_ref.dtype)


def diff_attention(q, k, v, lam, g_out, lambda_init, *, k_ctx=None, v_ctx=None, bq=512):
    b, t, qw = q.shape
    s, vw = k.shape[1], v.shape[2]
    heads = vw // DIFF_VD
    bq = _tile(t, bq)
    use_ctx = k_ctx is not None
    in_specs = [pl.BlockSpec(memory_space=pltpu.SMEM),
                pl.BlockSpec((None, bq, qw), lambda bi, qi: (bi, qi, 0)),
                pl.BlockSpec((None, s, qw), lambda bi, qi: (bi, 0, 0)),
                pl.BlockSpec((None, s, vw), lambda bi, qi: (bi, 0, 0))]
    args = [lam.astype(F32), q, k, v]
    c = 0
    if use_ctx:
        c = k_ctx.shape[1]
        in_specs += [pl.BlockSpec((None, c, qw), lambda bi, qi: (bi, 0, 0)),
                     pl.BlockSpec((None, c, vw), lambda bi, qi: (bi, 0, 0))]
        args += [k_ctx, v_ctx]
    in_specs.append(pl.BlockSpec((1, DIFF_VD), lambda bi, qi: (0, 0)))
    args.append(g_out.astype(F32).reshape(1, DIFF_VD))
    vmem = (8 * _nbytes((bq, s + c), F32) + 2 * _nbytes((bq, qw), BF16) + 2 * _nbytes((s, qw + vw), F32)
            + 3 * _nbytes((c, qw + vw), F32) + 6 * _nbytes((bq, vw), F32))
    return pl.pallas_call(
        functools.partial(_diff_attn_kernel, heads=heads, scale=DIFF_HD ** -0.5, post_scale=1.0 - lambda_init,
                          use_ctx=use_ctx), name="diff_attention",
        grid=(b, t // bq),
        in_specs=in_specs,
        out_specs=pl.BlockSpec((None, bq, vw), lambda bi, qi: (bi, qi, 0)),
        out_shape=jax.ShapeDtypeStruct((b, t, vw), BF16),
        compiler_params=_params(("arbitrary",) * 2, vmem),
    )(*args)


def _swa_lat_kernel(sink_ref, q_ref, kc_ref, vc_ref, k0_ref, k1_ref, k2_ref, v0_ref, v1_ref, v2_ref, o_ref,
                    *, scale, nblocks):
    n = pl.program_id(1)
    rows = SWA_REP * BLOCK
    r = lax.broadcasted_iota(jnp.int32, (rows, 3 * BLOCK), 0) & (BLOCK - 1)
    c = lax.broadcasted_iota(jnp.int32, (rows, 3 * BLOCK), 1)
    kpos = (n - 1) * BLOCK + c
    mask = (c - r >= BLOCK - WINDOW) & (c - r <= BLOCK + WINDOW) & (kpos >= 0) & (kpos < nblocks * BLOCK)
    pieces = [None] * SWA_HEADS
    for j in range(SWA_KV_HEADS):
        cs = slice(j * SWA_HD, (j + 1) * SWA_HD)
        q = _stack_heads(q_ref, range(j * SWA_REP, (j + 1) * SWA_REP), SWA_HD)
        kw = jnp.concatenate([k0_ref[:, cs], k1_ref[:, cs], k2_ref[:, cs]], axis=0)
        vw = jnp.concatenate([v0_ref[:, cs], v1_ref[:, cs], v2_ref[:, cs]], axis=0).astype(BF16)
        s_ctx = _qk(q, kc_ref[:, cs].astype(BF16), scale)
        s_loc = jnp.where(mask, _qk(q, kw, scale), NEG_INF)
        sink = _row_sinks(sink_ref, j * SWA_REP, SWA_REP, BLOCK)
        m = jnp.maximum(jnp.maximum(jnp.max(s_ctx, axis=-1, keepdims=True), jnp.max(s_loc, axis=-1, keepdims=True)), sink)
        p_ctx = jnp.exp(s_ctx - m)
        p_loc = jnp.exp(s_loc - m)
        l = jnp.sum(p_ctx, axis=-1, keepdims=True) + jnp.sum(p_loc, axis=-1, keepdims=True) + jnp.exp(sink - m)
        o = (jnp.dot(p_ctx.astype(BF16), vc_ref[:, cs].astype(BF16), preferred_element_type=F32)
             + jnp.dot(p_loc.astype(BF16), vw, preferred_element_type=F32)) * (1.0 / l)
        for g in range(SWA_REP):
            pieces[j * SWA_REP + g] = o[g * BLOCK:(g + 1) * BLOCK]
    o_ref[...] = jnp.concatenate(pieces, axis=1).astype(o_ref.dtype)


def swa_latent_attention(q, k, v, k_ctx, v_ctx, sink):
    b, t, qw = q.shape
    kw = k.shape[2]
    c = k_ctx.shape[1]
    nb = t // BLOCK
    pad = ((0, 0), (BLOCK, BLOCK), (0, 0))
    kp, vp = jnp.pad(k, pad), jnp.pad(v, pad)
    band = [pl.BlockSpec((None, BLOCK, kw), functools.partial(lambda bi, ni, o: (bi, ni + o, 0), o=o)) for o in range(3)]
    ctx = pl.BlockSpec((None, c, kw), lambda bi, ni: (bi, 0, 0))
    qspec = pl.BlockSpec((None, BLOCK, qw), lambda bi, ni: (bi, ni, 0))
    vmem = 8 * _nbytes((SWA_REP * BLOCK, c + 3 * BLOCK), F32) + 6 * _nbytes((c, kw), F32) + (4 << 20)
    return pl.pallas_call(
        functools.partial(_swa_lat_kernel, scale=SWA_HD ** -0.5, nblocks=nb), name="swa_latent_attention",
        grid=(b, nb),
        in_specs=[pl.BlockSpec(memory_space=pltpu.SMEM), qspec, ctx, ctx] + band + band,
        out_specs=qspec,
        out_shape=jax.ShapeDtypeStruct(q.shape, BF16),
        compiler_params=_params(("arbitrary",) * 2, vmem),
    )(sink.astype(F32), q, k_ctx, v_ctx, kp, kp, kp, vp, vp, vp)


def _state_to_halves(st):
    b, nd, _, g, n = st.shape
    gh = S5_HALF // SSM_CH
    z = st.reshape(b, nd, 2, g // gh, gh * n)
    return jnp.transpose(z, (1, 3, 0, 2, 4)).reshape(nd, g // gh, b, 2 * gh * n)


def _halves_to_state(h, groups):
    nd, nh, b, _ = h.shape
    z = h.reshape(nd, nh, b, 2, -1)
    return jnp.transpose(z, (2, 0, 3, 1, 4)).reshape(b, nd, 2, groups, SSM_N)


def _moe_ffn(logits, h_bf, P, j, bm):
    m = h_bf.shape[0]
    probs = jax.nn.softmax(logits, axis=-1)
    top_p, top_i = lax.top_k(probs, TOP_K)
    top_p = top_p / jnp.sum(top_p, axis=-1, keepdims=True)
    flat_e = top_i.reshape(-1).astype(jnp.int32)
    npairs = flat_e.shape[0]
    order = jnp.argsort(flat_e, stable=True).astype(jnp.int32)
    counts = jnp.sum(flat_e[:, None] == jnp.arange(N_EXPERTS, dtype=jnp.int32)[None, :], axis=0).astype(jnp.int32)
    blocks_per = (counts + bm - 1) // bm
    block_end = jnp.cumsum(blocks_per)
    pad_start = (block_end - blocks_per) * bm
    start = jnp.cumsum(counts) - counts
    sorted_e = flat_e[order]
    dest = pad_start[sorted_e] + jnp.arange(npairs, dtype=jnp.int32) - start[sorted_e]
    nrows = npairs + N_EXPERTS * bm
    pos = lax.sort_key_val(order, dest)[1].reshape(m, TOP_K)
    nblk = nrows // bm
    nused = block_end[-1:]
    block_expert = jnp.minimum(jnp.sum(jnp.arange(nblk, dtype=jnp.int32)[:, None] >= block_end[None, :], axis=1),
                               N_EXPERTS - 1).astype(jnp.int32)
    row = jnp.arange(nrows, dtype=jnp.int32)
    row_e = block_expert[row // bm]
    off = row - pad_start[row_e]
    rank = jnp.clip(start[row_e] + off, 0, npairs - 1)
    row_token = jnp.where(off < counts[row_e], order[rank] // TOP_K, 0)
    block_expert = jnp.where(jnp.arange(nblk) < nused[0], block_expert, block_expert[jnp.maximum(nused[0] - 1, 0)])
    x_sorted = jnp.take(h_bf, row_token, axis=0)
    act = moe_glu_up(x_sorted, P['w_moe_gate'], P['w_moe_up'], j, block_expert, nused, bm=bm, bn=1024)
    y_sorted = moe_down(act, P['w_moe_down'], j, block_expert, nused, bm=bm)
    out = jnp.zeros((m, y_sorted.shape[1]), F32)
    for kk in range(TOP_K):
        out = out + top_p[:, kk:kk + 1] * jnp.take(y_sorted, pos[:, kk], axis=0)
    return out


SWA_COLS = (SWA_HEADS + 2 * SWA_KV_HEADS) * SWA_HD
DIFF_COLS = 2 * DIFF_HEADS * 2 * DIFF_HD + DIFF_HEADS * DIFF_VD


def kernel(x_prompt, x_sample, cache_mla_ckv, cache_mla_krope, state_ssm, cache_swa_k, cache_swa_v, cache_diff_k, cache_diff_v, c, c_ctx, w_ada, b_ada, g_norm1, g_norm2, w_in, g_mla_qlat, g_mla_kvlat, w_mla_uq, w_mla_ukv, g_mla_q, g_mla_k, ssm_a_re, ssm_a_im, ssm_log_dt, ssm_b_re, ssm_b_im, ssm_c_re, ssm_c_im, ssm_d, w_ssm_glu, g_swa_q, g_swa_k, swa_sink, g_diff_q, g_diff_k, diff_lambda, g_diff_out, w_branch, w_out, w_ffn_gate, w_ffn_up, w_ffn_down, w_router, w_moe_gate, w_moe_up, w_moe_down):
    P = dict(ssm_a_re=ssm_a_re, ssm_a_im=ssm_a_im, ssm_log_dt=ssm_log_dt, ssm_b_re=ssm_b_re, ssm_b_im=ssm_b_im,
             ssm_c_re=ssm_c_re, ssm_c_im=ssm_c_im, ssm_d=ssm_d, w_router=w_router, w_moe_gate=w_moe_gate,
             w_moe_up=w_moe_up, w_moe_down=w_moe_down)
    bp, tp, d = x_prompt.shape
    bs, ts, _ = x_sample.shape
    depth = w_in.shape[0]
    past = cache_mla_ckv.shape[2]
    mp, ms = bp * tp, bs * ts
    m = mp + ms
    w = d // 4
    groups = w // SSM_CH
    x = jnp.concatenate([x_prompt.reshape(mp, d), x_sample.reshape(ms, d)], axis=0)

    seg = math.gcd(mp, ts)
    nseg = m // seg
    seg_mod = np.concatenate([np.zeros(mp // seg, np.int32), 1 + np.repeat(np.arange(bs, dtype=np.int32), ts // seg)])
    cvec = jnp.concatenate([c_ctx[None, :], c, jnp.zeros((8 - 1 - bs, d), F32)], axis=0)
    cvec = cvec * jax.nn.sigmoid(cvec)

    def per_seg(v):
        return v[seg_mod][:, None, :]

    def gated_add(xx, gt_seg, upd):
        return (xx.reshape(nseg, seg, d) + gt_seg * upd.reshape(nseg, seg, d)).reshape(m, d)

    rope_mla = _rope_tables(ts, MLA_QK, MLA_HEADS)
    rope_swa_q = rope_diff = _rope_tables(ts, SWA_HD, SWA_HEADS)
    rope_swa_k = _rope_tables(ts, SWA_HD, SWA_KV_HEADS)
    assert DIFF_HD == SWA_HD and 2 * DIFF_HEADS == SWA_HEADS

    outs = [[] for _ in range(7)]
    for l in range(depth):
        lambda_init = 0.8 - 0.6 * math.exp(-0.3 * l)
        mods = matmul(cvec, w_ada, (l,), bm=8, bn=1024) + b_ada[l][None, :]
        sh1, sc1, gt1, sh2, sc2, gt2 = (per_seg(v) for v in jnp.split(mods, 6, axis=-1))

        h = modulated_norm(x, g_norm1[l], sc1, sh1)
        p, col0 = in_projection(h, w_in, l)
        k_rope = matmul(h, w_in, (l,), bn=IN_SHIFT_UNIT, cols=(2 * w // IN_SHIFT_UNIT, 1))[:, :MLA_ROPE]

        def tm(z, bb, tt):
            return z.reshape(bb, tt, z.shape[-1])

        lat_rows = (mp, m, ts)

        ckv_bf, c_kv = head_norm(p, col0['ckv'], w, g_mla_kvlat[l], w, out_f32=True)
        q_lat = head_norm(p, col0['q'], w, g_mla_qlat[l], w)
        q_m = head_norm(matmul(q_lat, w_mla_uq, (l,)), 0, MLA_HEADS * MLA_QK, g_mla_q[l], MLA_QK,
                        rope=rope_mla, rope_rows=lat_rows)
        ckv_all = jnp.concatenate([ckv_bf, cache_mla_ckv[:, l].reshape(bs * past, w).astype(BF16)], axis=0)
        krope_all = jnp.concatenate([k_rope, cache_mla_krope[:, l].reshape(bs * past, MLA_ROPE)], axis=0)
        w_ukv = jnp.swapaxes(w_mla_ukv[l].reshape(w, MLA_HEADS, 2, MLA_NOPE), 1, 2).reshape(w, -1)
        kv = matmul(ckv_all, w_ukv, bm=512)
        k_m = head_norm(kv, 0, MLA_HEADS * MLA_NOPE, g_mla_k[l], MLA_QK, extra=krope_all,
                        rope=rope_mla, rope_rows=lat_rows)
        v_m = kv[:, MLA_HEADS * MLA_NOPE:]
        mla = dict(heads=MLA_HEADS, kv_heads=MLA_HEADS, scale=MLA_QK ** -0.5)
        o_mla_p = attention(tm(q_m[:mp], bp, tp), tm(k_m[:mp], bp, tp), tm(v_m[:mp], bp, tp), **mla)
        o_mla_s = attention(tm(q_m[mp:], bs, ts), tm(k_m[mp:m], bs, ts), tm(v_m[mp:m], bs, ts),
                            k_ctx=tm(k_m[m:], bs, past), v_ctx=tm(v_m[m:], bs, past), **mla)
        o_mla = (o_mla_p.reshape(mp, w), o_mla_s.reshape(ms, w))

        tabs = _s5_tables(ssm_a_re[l], ssm_a_im[l], ssm_log_dt[l], ssm_b_re[l], ssm_b_im[l], ssm_c_re[l], ssm_c_im[l])
        ncol = p.shape[1]
        h0_p = jnp.zeros((2, w // S5_HALF, bp, 2 * (S5_HALF // SSM_CH) * SSM_N), F32)
        y_p, hT_p = s5_scan(p.reshape(m // tp, tp, ncol), col0['ssm'], 0, bp, tabs, h0_p)
        y_s, _ = s5_scan(p.reshape(m // ts, ts, ncol), col0['ssm'], mp // ts, bs, tabs, _state_to_halves(state_ssm[:, l]))
        o_ssm = ssm_post(y_p.reshape(2, mp, w), y_s.reshape(2, ms, w), p, col0['ssm'], ssm_d[l], w_ssm_glu, l)
        ssm_state = _halves_to_state(hT_p, groups)

        nq = SWA_HEADS * SWA_HD
        nk = SWA_KV_HEADS * SWA_HD
        c_swa = col0['swa']
        q_s = head_norm(p, c_swa, nq, g_swa_q[l], SWA_HD, bw=256, rope=rope_swa_q, rope_rows=lat_rows)
        k_s, k_s32 = head_norm(p, c_swa + nq, nk, g_swa_k[l], SWA_HD, rope=rope_swa_k, rope_rows=lat_rows, out_f32=True)
        v_s = p[:, c_swa + nq + nk:c_swa + nq + 2 * nk]
        o_swa_p = attention(tm(q_s[:mp], bp, tp), tm(k_s[:mp], bp, tp), tm(v_s[:mp], bp, tp), heads=SWA_HEADS,
                            kv_heads=SWA_KV_HEADS, scale=SWA_HD ** -0.5, sink=swa_sink[l], bq=256)
        o_swa_s = swa_latent_attention(tm(q_s[mp:], bs, ts), tm(k_s[mp:], bs, ts), tm(v_s[mp:], bs, ts),
                                       cache_swa_k[:, l].reshape(bs, past, nk), cache_swa_v[:, l].reshape(bs, past, nk),
                                       swa_sink[l])
        o_swa = (o_swa_p.reshape(mp, w), o_swa_s.reshape(ms, w))

        ndq = DIFF_HEADS * 2 * DIFF_HD
        c_diff = col0['diff']
        q_d = head_norm(p, c_diff, ndq, g_diff_q[l], DIFF_HD, bw=256, rope=rope_diff, rope_rows=lat_rows)
        k_d, k_d32 = head_norm(p, c_diff + ndq, ndq, g_diff_k[l], DIFF_HD, bw=256, rope=rope_diff, rope_rows=lat_rows,
                               out_f32=True)
        v_d = p[:, c_diff + 2 * ndq:c_diff + 2 * ndq + DIFF_HEADS * DIFF_VD]
        lp = diff_lambda[l]
        lam = jnp.exp(jnp.sum(lp[0] * lp[1], -1)) - jnp.exp(jnp.sum(lp[2] * lp[3], -1)) + lambda_init
        o_diff_p = diff_attention(tm(q_d[:mp], bp, tp), tm(k_d[:mp], bp, tp), tm(v_d[:mp], bp, tp),
                                  lam, g_diff_out[l], lambda_init, bq=256)
        o_diff_s = diff_attention(tm(q_d[mp:], bs, ts), tm(k_d[mp:], bs, ts), tm(v_d[mp:], bs, ts), lam, g_diff_out[l],
                                  lambda_init, k_ctx=cache_diff_k[:, l].reshape(bs, past, ndq),
                                  v_ctx=cache_diff_v[:, l].reshape(bs, past, DIFF_HEADS * DIFF_VD), bq=256)
        o_diff = (o_diff_p.reshape(mp, w), o_diff_s.reshape(ms, w))

        merged = merge_branches((o_mla, o_ssm, o_swa, o_diff), p, col0['gate'], w_branch, l, bm=1024, bn=256)
        x = matmul_residual(merged, w_out, (l,), x, gt1, bm=1024)

        j = l // 2
        if l % 2 == 0:
            h2 = modulated_norm(x, g_norm2[l], sc2, sh2)
            act = glu_up(h2, w_ffn_gate, w_ffn_up, (j,))
            x = matmul_residual(act, w_ffn_down, (j,), x, gt2)
        else:
            h2, logits = modulated_norm(x, g_norm2[l], sc2, sh2, w_router, j)
            x = gated_add(x, gt2, _moe_ffn(logits, h2, P, j, bm=256))

        outs[0].append(c_kv[:mp].reshape(bp, tp, w))
        outs[1].append(k_rope[:mp].reshape(bp, tp, MLA_ROPE))
        outs[2].append(ssm_state)
        outs[3].append(k_s32[:mp].reshape(bp, tp, SWA_KV_HEADS, SWA_HD))
        outs[4].append(v_s[:mp].reshape(bp, tp, SWA_KV_HEADS, SWA_HD))
        outs[5].append(k_d32[:mp].reshape(bp, tp, DIFF_HEADS, 2, DIFF_HD))
        outs[6].append(v_d[:mp].reshape(bp, tp, DIFF_HEADS, DIFF_VD))

    return (x[:mp].reshape(bp, tp, d), x[mp:].reshape(bs, ts, d)) + tuple(jnp.stack(o, axis=1) for o in outs)
```

```python
import functools
import math

import numpy as np
import jax
import jax.numpy as jnp
from jax import lax
from jax.experimental import pallas as pl
from jax.experimental.pallas import tpu as pltpu

F32 = jnp.float32
BF16 = jnp.bfloat16
EPS = 1e-6
NEG_INF = -1e30
ROPE_THETA = 10000.0
GRID_W = 64
BLOCK = 128
N_BRANCH = 4
MLA_HEADS = 4
MLA_NOPE = 128
MLA_ROPE = 64
MLA_QK = MLA_NOPE + MLA_ROPE
MLA_V = 128
SSM_CH = 16
SSM_N = 64
SWA_HEADS = 8
SWA_KV_HEADS = 2
SWA_REP = SWA_HEADS // SWA_KV_HEADS
SWA_HD = 64
WINDOW = 128
DIFF_HEADS = 4
DIFF_HD = 64
DIFF_VD = 2 * DIFF_HD
N_EXPERTS = 8
TOP_K = 2

V7X_VMEM_REQUEST_CAP = 56 * 1024 * 1024


def _params(semantics, vmem_bytes):
    limit = int(min(max(vmem_bytes * 5 // 4 + (2 << 20), 16 << 20), V7X_VMEM_REQUEST_CAP))
    return pltpu.CompilerParams(dimension_semantics=semantics, vmem_limit_bytes=limit)


def _tile(n, want):
    t = min(n, want)
    while n % t:
        t -= 1
    return t


def _nbytes(shape, dtype):
    return int(np.prod(shape)) * jnp.dtype(dtype).itemsize


def _mm_kernel(x_ref, w_ref, o_ref, *scratch):
    if scratch:
        wbf_ref, = scratch

        @pl.when(pl.program_id(1) == 0)
        def _():
            wbf_ref[...] = w_ref[...].astype(BF16)
        w = wbf_ref[...]
    else:
        w = w_ref[...]
    o_ref[...] = jnp.dot(x_ref[...].astype(BF16), w, preferred_element_type=F32).astype(o_ref.dtype)


def matmul(x, w, lead=(), *, bm=1024, bn=512, out_dtype=F32, cols=None):
    m, k = x.shape
    assert w.shape[-2] == k
    cb0 = 0
    if cols is None:
        n = w.shape[-1]
        bn = _tile(n, bn)
    else:
        cb0, n = cols[0], cols[1] * bn
    bm = _tile(m, bm)
    assert m % bm == 0 and n % bn == 0, (m, n, bm, bn)
    lead = tuple(lead)
    nl = len(lead)
    cast = w.dtype != BF16
    scratch = [pltpu.VMEM((k, bn), BF16)] if cast else []
    vmem = (2 * _nbytes((bm, k), x.dtype) + 2 * _nbytes((k, bn), w.dtype)
            + 2 * _nbytes((bm, bn), out_dtype) + (_nbytes((k, bn), BF16) if cast else 0))
    return pl.pallas_call(
        _mm_kernel, name="matmul",
        grid=(n // bn, m // bm),
        in_specs=[pl.BlockSpec((bm, k), lambda j, i: (i, 0)),
                  pl.BlockSpec((None,) * nl + (k, bn), lambda j, i: lead + (0, cb0 + j))],
        out_specs=pl.BlockSpec((bm, bn), lambda j, i: (i, j)),
        out_shape=jax.ShapeDtypeStruct((m, n), out_dtype),
        scratch_shapes=scratch,
        compiler_params=_params(("arbitrary", "arbitrary"), vmem),
    )(x, w)


IN_TILE = 1024
IN_SHIFT_UNIT = 128
IN_CAST_ROWS = 512


def _inproj_kernel(a_idx, b_idx, shift_ref, valid_ref, x_ref, wa_ref, wb_ref, o_ref, wbf_ref, *, shift):
    j = pl.program_id(0)
    first = pl.program_id(1) == 0

    @pl.when(first & (shift_ref[j] == 0))
    def _():
        wbf_ref[...] = wa_ref[...].astype(BF16)

    @pl.when(first & (shift_ref[j] != 0))
    def _():
        keep = lax.broadcasted_iota(jnp.int32, (IN_CAST_ROWS, IN_TILE), 1) < valid_ref[j]
        for r in range(0, wa_ref.shape[0], IN_CAST_ROWS):
            rows = slice(r, r + IN_CAST_ROWS)
            wcat = jnp.concatenate([wa_ref[rows, :], wb_ref[rows, :]], axis=1)
            wbf_ref[rows, :] = jnp.where(keep, wcat[:, shift:shift + IN_TILE], 0.0).astype(BF16)

    o_ref[...] = jnp.dot(x_ref[...], wbf_ref[...], preferred_element_type=F32)


def in_projection(h, w_in, layer, *, bm=1024):
    m, k = h.shape
    ntot = w_in.shape[-1]
    w = k // 4
    lead = 2 * w
    rest = ntot - lead - MLA_ROPE
    assert lead % IN_TILE == 0 and MLA_ROPE < IN_SHIFT_UNIT and IN_TILE % IN_SHIFT_UNIT == 0 and k % IN_CAST_ROWS == 0
    n_lead, n_rest = lead // IN_TILE, -(-rest // IN_TILE)
    last_unit = (ntot - 1) // IN_SHIFT_UNIT
    a_idx = list(range(n_lead)) + [n_lead + t for t in range(n_rest)]
    b_idx = [0] * n_lead + [min((lead + IN_TILE * (t + 1)) // IN_SHIFT_UNIT, last_unit) for t in range(n_rest)]
    shifted = [0] * n_lead + [1] * n_rest
    valid = [IN_TILE] * n_lead + [min(IN_TILE, rest - IN_TILE * t) for t in range(n_rest)]
    ntiles = len(a_idx)
    col0 = dict(q=0, ckv=w, ssm=lead, swa=lead + w, diff=lead + w + SWA_COLS, gate=lead + w + SWA_COLS + DIFF_COLS)
    bm = _tile(m, bm)
    vmem = (2 * _nbytes((bm, k), BF16) + 2 * _nbytes((k, IN_TILE + IN_SHIFT_UNIT), F32) + _nbytes((k, IN_TILE), BF16)
            + 3 * _nbytes((IN_CAST_ROWS, IN_TILE + IN_SHIFT_UNIT), F32) + 2 * _nbytes((bm, IN_TILE), F32))
    p = pl.pallas_call(
        functools.partial(_inproj_kernel, shift=MLA_ROPE), name="in_projection",
        grid_spec=pltpu.PrefetchScalarGridSpec(
            num_scalar_prefetch=4,
            grid=(ntiles, m // bm),
            in_specs=[pl.BlockSpec((bm, k), lambda j, i, a, b, s, v: (i, 0)),
                      pl.BlockSpec((None, k, IN_TILE), lambda j, i, a, b, s, v: (layer, 0, a[j])),
                      pl.BlockSpec((None, k, IN_SHIFT_UNIT), lambda j, i, a, b, s, v: (layer, 0, b[j]))],
            out_specs=pl.BlockSpec((bm, IN_TILE), lambda j, i, a, b, s, v: (i, j)),
            scratch_shapes=[pltpu.VMEM((k, IN_TILE), BF16)]),
        out_shape=jax.ShapeDtypeStruct((m, ntiles * IN_TILE), F32),
        compiler_params=_params(("arbitrary", "arbitrary"), vmem),
    )(jnp.asarray(a_idx, jnp.int32), jnp.asarray(b_idx, jnp.int32), jnp.asarray(shifted, jnp.int32),
      jnp.asarray(valid, jnp.int32), h, w_in, w_in)
    return p, col0


def _glu_kernel(x_ref, wg_ref, wu_ref, o_ref, wg_bf, wu_bf):
    @pl.when(pl.program_id(1) == 0)
    def _():
        wg_bf[...] = wg_ref[...].astype(BF16)
        wu_bf[...] = wu_ref[...].astype(BF16)
    x = x_ref[...]
    a = jnp.dot(x, wg_bf[...], preferred_element_type=F32)
    b = jnp.dot(x, wu_bf[...], preferred_element_type=F32)
    o_ref[...] = (a * jax.nn.sigmoid(a) * b).astype(o_ref.dtype)


def glu_up(x, wg, wu, lead, *, bm=1024, bn=512):
    m, k = x.shape
    n = wg.shape[-1]
    bm, bn = _tile(m, bm), _tile(n, bn)
    assert m % bm == 0 and n % bn == 0
    nl = len(lead)
    wspec = pl.BlockSpec((None,) * nl + (k, bn), lambda j, i: lead + (0, j))
    vmem = (2 * _nbytes((bm, k), x.dtype) + 4 * _nbytes((k, bn), F32) + 2 * _nbytes((k, bn), BF16)
            + 2 * _nbytes((bm, bn), BF16) + 3 * _nbytes((bm, bn), F32))
    return pl.pallas_call(
        _glu_kernel, name="glu_up",
        grid=(n // bn, m // bm),
        in_specs=[pl.BlockSpec((bm, k), lambda j, i: (i, 0)), wspec, wspec],
        out_specs=pl.BlockSpec((bm, bn), lambda j, i: (i, j)),
        out_shape=jax.ShapeDtypeStruct((m, n), BF16),
        scratch_shapes=[pltpu.VMEM((k, bn), BF16), pltpu.VMEM((k, bn), BF16)],
        compiler_params=_params(("arbitrary", "arbitrary"), vmem),
    )(x, wg, wu)


def _mm_res_kernel(a_ref, w_ref, x_ref, gt_ref, o_ref, wbf_ref):
    @pl.when(pl.program_id(1) == 0)
    def _():
        wbf_ref[...] = w_ref[...].astype(BF16)
    upd = jnp.dot(a_ref[...], wbf_ref[...], preferred_element_type=F32)
    o_ref[...] = x_ref[...] + gt_ref[...] * upd


def matmul_residual(a, w, lead, x, gt_seg, *, bm=512, bn=512):
    m, k = a.shape
    n = w.shape[-1]
    seg = m // gt_seg.shape[0]
    bm, bn = _tile(seg, bm), _tile(n, bn)
    nl = len(lead)
    vmem = (2 * _nbytes((bm, k), a.dtype) + 2 * _nbytes((k, bn), F32) + _nbytes((k, bn), BF16)
            + 5 * _nbytes((bm, bn), F32))
    return pl.pallas_call(
        _mm_res_kernel, name="matmul_residual",
        grid=(n // bn, m // bm),
        in_specs=[pl.BlockSpec((bm, k), lambda j, i: (i, 0)),
                  pl.BlockSpec((None,) * nl + (k, bn), lambda j, i: lead + (0, j)),
                  pl.BlockSpec((bm, bn), lambda j, i: (i, j)),
                  pl.BlockSpec((None, 1, bn), lambda j, i: (i * bm // seg, 0, j))],
        out_specs=pl.BlockSpec((bm, bn), lambda j, i: (i, j)),
        out_shape=jax.ShapeDtypeStruct((m, n), F32),
        scratch_shapes=[pltpu.VMEM((k, bn), BF16)],
        compiler_params=_params(("arbitrary", "arbitrary"), vmem),
    )(a, w, x, gt_seg)


def _merge_kernel(*refs, split, head_blocks):
    refs = list(refs)
    i = pl.program_id(1)
    outs = []
    for is_split in split:
        if is_split:
            head, tail = refs.pop(0), refs.pop(0)
            outs.append(jnp.where(i < head_blocks, head[...], tail[...]))
        else:
            outs.append(refs.pop(0)[...])
    g_refs = refs[:N_BRANCH]
    wb_ref, out_ref, wbf_ref = refs[N_BRANCH:]

    @pl.when(i == 0)
    def _():
        wbf_ref[...] = wb_ref[...].astype(BF16)
    acc = None
    for n_ in range(N_BRANCH):
        up = jnp.dot(outs[n_], wbf_ref[n_], preferred_element_type=F32)
        t = jax.nn.sigmoid(g_refs[n_][...]) * up
        acc = t if acc is None else acc + t
    out_ref[...] = acc.astype(out_ref.dtype)


def merge_branches(branch_outs, p, gate_col0, w_branch, layer, *, bm=512, bn=512):
    m = p.shape[0]
    split = tuple(isinstance(o, tuple) for o in branch_outs)
    heads = {o[0].shape[0] for o in branch_outs if isinstance(o, tuple)}
    assert len(heads) <= 1
    m_head = heads.pop() if heads else m
    bw = (branch_outs[0][0] if split[0] else branch_outs[0]).shape[1]
    dm = w_branch.shape[-1]
    bm, bn = _tile(math.gcd(m, m_head), bm), _tile(dm, bn)
    assert gate_col0 % bn == 0
    hb = m_head // bm
    o_specs, o_args = [], []
    for o in branch_outs:
        if isinstance(o, tuple):
            o_specs += [pl.BlockSpec((bm, bw), lambda j, i: (jnp.minimum(i, hb - 1), 0)),
                        pl.BlockSpec((bm, bw), lambda j, i: (jnp.maximum(i - hb, 0), 0))]
            o_args += list(o)
        else:
            o_specs.append(pl.BlockSpec((bm, bw), lambda j, i: (i, 0)))
            o_args.append(o)
    g_specs = [pl.BlockSpec((bm, bn), functools.partial(lambda j, i, c: (i, c + j), c=(gate_col0 + n_ * dm) // bn))
               for n_ in range(N_BRANCH)]
    vmem = (N_BRANCH * (5 * _nbytes((bm, bw), BF16) + 2 * _nbytes((bm, bn), F32) + 2 * _nbytes((bw, bn), F32)
                        + _nbytes((bw, bn), BF16)) + 6 * _nbytes((bm, bn), F32))
    return pl.pallas_call(
        functools.partial(_merge_kernel, split=split, head_blocks=hb), name="merge_branches",
        grid=(dm // bn, m // bm),
        in_specs=o_specs + g_specs + [pl.BlockSpec((None, N_BRANCH, bw, bn), lambda j, i: (layer, 0, 0, j))],
        out_specs=pl.BlockSpec((bm, bn), lambda j, i: (i, j)),
        out_shape=jax.ShapeDtypeStruct((m, dm), BF16),
        scratch_shapes=[pltpu.VMEM((N_BRANCH, bw, bn), BF16)],
        compiler_params=_params(("arbitrary", "arbitrary"), vmem),
    )(*o_args, *([p] * N_BRANCH), w_branch)


def _modnorm_kernel(*refs, with_router):
    if with_router:
        x_ref, g_ref, sc_ref, sh_ref, wr_ref, h_ref, logit_ref = refs
    else:
        x_ref, g_ref, sc_ref, sh_ref, h_ref = refs
    x = x_ref[...]
    y = x * lax.rsqrt(jnp.mean(x * x, axis=-1, keepdims=True) + EPS) * g_ref[...]
    h = y * (1 + sc_ref[...]) + sh_ref[...]
    h_ref[...] = h.astype(h_ref.dtype)
    if with_router:
        logit_ref[...] = jnp.dot(h, wr_ref[...], precision=lax.Precision.HIGHEST, preferred_element_type=F32)


def modulated_norm(x, g, sc_seg, sh_seg, w_router=None, router_layer=0, *, bm=512):
    m, d = x.shape
    seg = m // sc_seg.shape[0]
    bm = _tile(seg, bm)
    mod_spec = pl.BlockSpec((None, 1, d), lambda i: (i * bm // seg, 0, 0))
    in_specs = [pl.BlockSpec((bm, d), lambda i: (i, 0)), pl.BlockSpec((1, d), lambda i: (0, 0)), mod_spec, mod_spec]
    out_specs = [pl.BlockSpec((bm, d), lambda i: (i, 0))]
    out_shape = [jax.ShapeDtypeStruct((m, d), BF16)]
    args = [x, g.reshape(1, d), sc_seg, sh_seg]
    if w_router is not None:
        ne = w_router.shape[-1]
        in_specs.append(pl.BlockSpec((None, d, ne), lambda i: (router_layer, 0, 0)))
        out_specs.append(pl.BlockSpec((bm, ne), lambda i: (i, 0)))
        out_shape.append(jax.ShapeDtypeStruct((m, ne), F32))
        args.append(w_router)
    vmem = 8 * _nbytes((bm, d), F32) + 2 * _nbytes((d, 128), F32)
    out = pl.pallas_call(
        functools.partial(_modnorm_kernel, with_router=w_router is not None), name="modulated_norm",
        grid=(m // bm,),
        in_specs=in_specs, out_specs=out_specs, out_shape=out_shape,
        compiler_params=_params(("arbitrary",), vmem),
    )(*args)
    return out if w_router is not None else out[0]


ROPE_D = 64
HEAD_PAD = 128


def _rope_tables(t, head_dim, heads):
    half, n_freq = ROPE_D // 2, ROPE_D // 4
    rows = t // GRID_W
    r, col = jnp.meshgrid(jnp.arange(rows, dtype=F32), jnp.arange(GRID_W, dtype=F32), indexing='ij')
    inv = ROPE_THETA ** (-jnp.arange(n_freq, dtype=F32) / n_freq)
    ang = jnp.concatenate([r.reshape(-1, 1) * inv, col.reshape(-1, 1) * inv], axis=-1)
    plain = head_dim - ROPE_D
    cos = jnp.concatenate([jnp.ones((t, plain), F32), jnp.cos(ang), jnp.cos(ang)], axis=-1)
    sin = jnp.concatenate([jnp.zeros((t, plain), F32), jnp.sin(ang), jnp.sin(ang)], axis=-1)
    m_plus = np.concatenate([np.zeros(plain), -np.ones(half), np.zeros(half)]).astype(np.float32)
    m_minus = np.concatenate([np.zeros(plain), np.zeros(half), np.ones(half)]).astype(np.float32)
    tile = lambda z: jnp.tile(jnp.asarray(z), (1,) * (jnp.ndim(z) - 1) + (heads,))
    return dict(cos=tile(cos), sin=tile(sin), m_plus=tile(m_plus[None, :]), m_minus=tile(m_minus[None, :]))


def _dot_hi_lo(a, b_bf16):
    hi = a.astype(BF16)
    lo = (a - hi.astype(F32)).astype(BF16)
    return (jnp.dot(hi, b_bf16, preferred_element_type=F32) + jnp.dot(lo, b_bf16, preferred_element_type=F32))


def _headnorm_kernel(*refs, d_head, nope, use_rope, rope_blocks, out_f32):
    it = iter(refs)
    x_ref = next(it)
    extra_ref = next(it) if nope else None
    e_ref, et_ref, g_ref = next(it), next(it), next(it)
    if use_rope:
        cos_ref, sin_ref, mp_ref, mm_ref = next(it), next(it), next(it), next(it)
    obf_ref = next(it)
    x = x_ref[...]
    if nope:
        ex = extra_ref[...]
        x = jnp.concatenate([part for h in range(x.shape[1] // nope) for part in (x[:, h * nope:(h + 1) * nope], ex)],
                            axis=1)
    ss = _dot_hi_lo(x * x, e_ref[...])
    scale = lax.rsqrt(ss / d_head + EPS)
    y = x * _dot_hi_lo(scale, et_ref[...]) * g_ref[...]
    if out_f32:
        next(it)[...] = y
    if not use_rope:
        obf_ref[...] = y.astype(obf_ref.dtype)
        return
    i = pl.program_id(0)
    is_rope = (i >= rope_blocks[0]) & (i < rope_blocks[1])

    @pl.when(is_rope)
    def _():
        bw = y.shape[1]
        partner = (pltpu.roll(y, bw - ROPE_D // 2, 1) * mp_ref[...] + pltpu.roll(y, ROPE_D // 2, 1) * mm_ref[...])
        obf_ref[...] = (y * cos_ref[...] + partner * sin_ref[...]).astype(obf_ref.dtype)

    @pl.when(jnp.logical_not(is_rope))
    def _():
        obf_ref[...] = y.astype(obf_ref.dtype)


def head_norm(x, col0, width, gains, d_head, *, bw=None, extra=None, rope=None, rope_rows=None, out_f32=False, bm=512):
    m = x.shape[0]
    nope = 0
    if extra is not None:
        e = extra.shape[1]
        nope = d_head - e
        heads = width // nope
        bw_in, bw = width, heads * d_head
        wout = bw
    else:
        bw = bw or width
        bw_in, wout = bw, width
    assert col0 % bw_in == 0 and wout % bw == 0 and bw % d_head == 0
    nh = bw // d_head
    seg = np.arange(bw) // d_head
    e_mat = (seg[:, None] == np.arange(HEAD_PAD)[None, :]).astype(np.float32)
    g = jnp.tile(gains.astype(F32).reshape(-1), wout // gains.size).reshape(1, wout)
    use_rope = rope is not None
    bm = _tile(functools.reduce(math.gcd, rope_rows, m) if use_rope else m, bm if bw > 512 else 2 * bm)
    in_specs = [pl.BlockSpec((bm, bw_in), lambda i, j: (i, col0 // bw_in + j))]
    args = [x]
    if extra is not None:
        in_specs.append(pl.BlockSpec((bm, extra.shape[1]), lambda i, j: (i, 0)))
        args.append(extra)
    in_specs += [pl.BlockSpec((bw, HEAD_PAD), lambda i, j: (0, 0)), pl.BlockSpec((HEAD_PAD, bw), lambda i, j: (0, 0)),
                 pl.BlockSpec((1, bw), lambda i, j: (0, j))]
    args += [jnp.asarray(e_mat, BF16), jnp.asarray(e_mat.T, BF16), g]
    rope_blocks = (0, 0)
    if use_rope:
        r0, r1, period = rope_rows
        assert r0 % bm == 0 and r1 % bm == 0 and period % bm == 0 and nh * d_head == bw
        rope_blocks = (r0 // bm, r1 // bm)
        nper, nrb = period // bm, (r1 - r0) // bm
        tspec = pl.BlockSpec((bm, bw), lambda i, j: (jnp.clip(i - r0 // bm, 0, nrb - 1) % nper, j))
        mspec = pl.BlockSpec((1, bw), lambda i, j: (0, j))
        in_specs += [tspec, tspec, mspec, mspec]
        args += [rope['cos'], rope['sin'], rope['m_plus'], rope['m_minus']]
    out_specs = [pl.BlockSpec((bm, bw), lambda i, j: (i, j))]
    out_shape = [jax.ShapeDtypeStruct((m, wout), BF16)]
    if out_f32:
        out_specs.append(pl.BlockSpec((bm, bw), lambda i, j: (i, j)))
        out_shape.append(jax.ShapeDtypeStruct((m, wout), F32))
    vmem = 14 * _nbytes((bm, bw), F32) + 4 * _nbytes((bw, HEAD_PAD), F32)
    out = pl.pallas_call(
        functools.partial(_headnorm_kernel, d_head=d_head, nope=nope, use_rope=use_rope, rope_blocks=rope_blocks,
                          out_f32=out_f32), name="head_norm",
        grid=(m // bm, wout // bw),
        in_specs=in_specs, out_specs=out_specs, out_shape=out_shape,
        compiler_params=_params(("arbitrary", "arbitrary"), vmem),
    )(*args)
    return out if out_f32 else out[0]


S5_CHUNK = 64
S5_HALF = 256


def _s5_tables(a_re, a_im, log_dt, b_re, b_im, c_re, c_im):
    ndir, g, n = a_re.shape
    ch = b_re.shape[-1]
    gh = S5_HALF // ch
    nh = g // gh
    dt = jnp.exp(log_dt)[..., None]
    er, ei = a_re * dt, a_im * dt
    mag = jnp.exp(er)
    lb_re, lb_im = mag * jnp.cos(ei), mag * jnp.sin(ei)
    den = a_re * a_re + a_im * a_im
    xr, xi = lb_re - 1.0, lb_im
    co_re, co_im = (xr * a_re + xi * a_im) / den, (xi * a_re - xr * a_im) / den
    bb_re = co_re[..., None] * b_re - co_im[..., None] * b_im
    bb_im = co_re[..., None] * b_im + co_im[..., None] * b_re
    eye = jnp.eye(gh, dtype=F32)

    def blockdiag_in(z):
        z = z.reshape(ndir, nh, gh, n, ch)
        return jnp.einsum('dhgnj,gk->dhgjkn', z, eye).reshape(ndir, nh, gh * ch, gh * n)

    def blockdiag_out(z):
        z = z.reshape(ndir, nh, gh, ch, n)
        return jnp.einsum('dhgin,gk->dhkngi', z, eye).reshape(ndir, nh, gh * n, gh * ch)

    bmat = jnp.concatenate([blockdiag_in(bb_re), blockdiag_in(bb_im)], axis=-1).astype(BF16)
    cmat = jnp.concatenate([blockdiag_out(c_re), -blockdiag_out(c_im)], axis=-2).astype(BF16)

    def halves(z):
        return z.reshape(z.shape[:-2] + (nh, gh * n))

    steps = jnp.arange(S5_CHUNK, dtype=F32)
    expo = jnp.stack([steps, S5_CHUNK - 1 - steps])[:, :, None, None]

    def powers(sign):
        pr, pi = sign * expo * er[:, None], sign * expo * ei[:, None]
        pm = jnp.exp(pr)
        tab = jnp.concatenate([halves(pm * jnp.cos(pi)), halves(pm * jnp.sin(pi))], axis=-1)
        return jnp.swapaxes(tab, 1, 2)

    lam = jnp.concatenate([halves(lb_re), halves(lb_im)], axis=-1)[:, :, None, :]
    ti = np.arange(S5_CHUNK)
    tri = jnp.asarray(np.stack([ti[:, None] >= ti[None, :], ti[:, None] <= ti[None, :]]), BF16)
    return dict(bmat=bmat, cmat=cmat, p=powers(1.0), pinv=powers(-1.0), lam=lam, tri=tri)


def _s5_kernel(u_ref, bm_ref, cm_ref, p_ref, pinv_ref, lam_ref, tri_ref, h0_ref, y_ref, hT_ref, hs_ref, st_ref,
               *, nb, nchunks):
    d, c = pl.program_id(0), pl.program_id(3)
    L = S5_CHUNK
    half = p_ref.shape[-1] // 2

    @pl.when(c == 0)
    def _():
        st_ref[...] = h0_ref[...]

    u = u_ref[...].reshape(nb * L, u_ref.shape[-1]).astype(BF16)
    bu = jnp.dot(u, bm_ref[...], preferred_element_type=F32)
    p_re, p_im = p_ref[:, :half], p_ref[:, half:]
    q_re, q_im = pinv_ref[:, :half], pinv_ref[:, half:]
    lam_re, lam_im = lam_ref[:, :half], lam_ref[:, half:]
    tri = tri_ref[...]
    for i in range(nb):
        b_re, b_im = bu[i * L:(i + 1) * L, :half], bu[i * L:(i + 1) * L, half:]
        z = jnp.concatenate([b_re * q_re - b_im * q_im, b_re * q_im + b_im * q_re], axis=1)
        w = jnp.dot(tri, z.astype(BF16), preferred_element_type=F32)
        s_re, s_im = st_ref[i:i + 1, :half], st_ref[i:i + 1, half:]
        w_re = w[:, :half] + (lam_re * s_re - lam_im * s_im)
        w_im = w[:, half:] + (lam_re * s_im + lam_im * s_re)
        h = jnp.concatenate([p_re * w_re - p_im * w_im, p_re * w_im + p_im * w_re], axis=1)
        hs_ref[i * L:(i + 1) * L, :] = h.astype(BF16)
        st_ref[i:i + 1, :] = jnp.where(d == 0, h[L - 1:L, :], h[0:1, :])
    y = jnp.dot(hs_ref[...], cm_ref[...], preferred_element_type=F32)
    y_ref[...] = y.reshape(y_ref.shape)

    @pl.when(c == nchunks - 1)
    def _():
        hT_ref[...] = st_ref[...]


def s5_scan(p3, col0, seq0, n_seq, tabs, h0):
    _, t, _ = p3.shape
    ndir, nh, hc, ns2 = tabs['bmat'].shape
    L = S5_CHUNK
    nb = _tile(n_seq, 8)
    assert t % L == 0 and seq0 % nb == 0 and col0 % hc == 0
    nchunks = t // L
    sg0, cb0 = seq0 // nb, col0 // hc

    def tchunk(d, c):
        return c + d * (nchunks - 1 - 2 * c)

    tab = lambda r, cc: pl.BlockSpec((None, None, r, cc), lambda d, hf, sg, c: (d, hf, 0, 0))
    vmem = (4 * _nbytes((nb * L, hc), F32) + 4 * _nbytes((hc, ns2), BF16) + 4 * _nbytes((L, ns2), F32)
            + 3 * _nbytes((nb * L, ns2), F32) + 8 * _nbytes((L, ns2), F32))
    return pl.pallas_call(
        functools.partial(_s5_kernel, nb=nb, nchunks=nchunks), name="s5_scan",
        grid=(ndir, nh, n_seq // nb, nchunks),
        in_specs=[pl.BlockSpec((nb, L, hc), lambda d, hf, sg, c: (sg0 + sg, tchunk(d, c), cb0 + hf)),
                  tab(hc, ns2), tab(ns2, hc), tab(L, ns2), tab(L, ns2), tab(1, ns2),
                  pl.BlockSpec((None, L, L), lambda d, hf, sg, c: (d, 0, 0)),
                  pl.BlockSpec((None, None, nb, ns2), lambda d, hf, sg, c: (d, hf, sg, 0))],
        out_specs=[pl.BlockSpec((None, nb, L, hc), lambda d, hf, sg, c: (d, sg, tchunk(d, c), hf)),
                   pl.BlockSpec((None, None, nb, ns2), lambda d, hf, sg, c: (d, hf, sg, 0))],
        out_shape=[jax.ShapeDtypeStruct((ndir, n_seq, t, nh * hc), F32),
                   jax.ShapeDtypeStruct((ndir, nh, n_seq, ns2), F32)],
        scratch_shapes=[pltpu.VMEM((nb * L, ns2), BF16), pltpu.VMEM((nb, ns2), F32)],
        compiler_params=_params(("arbitrary",) * 4, vmem),
    )(p3, tabs['bmat'], tabs['cmat'], tabs['p'], tabs['pinv'], tabs['lam'], tabs['tri'], h0)


def _ssm_post_kernel(yh_ref, yt_ref, u_ref, d_ref, w_ref, o_ref, wbf_ref, *, head_blocks):
    i = pl.program_id(0)

    @pl.when(i == 0)
    def _():
        wbf_ref[...] = w_ref[...].astype(BF16)
    y2 = jnp.where(i < head_blocks, yh_ref[...], yt_ref[...])
    y = y2[0] + y2[1] + d_ref[...] * u_ref[...]
    g = jax.nn.gelu(y)
    z = jnp.dot(g.astype(BF16), wbf_ref[...], preferred_element_type=F32)
    o_ref[...] = (g * jax.nn.sigmoid(z)).astype(o_ref.dtype)


def ssm_post(y_head, y_tail, p, col0, d_vec, w_glu, layer, *, bm=1024):
    _, m_head, w = y_head.shape
    m = m_head + y_tail.shape[1]
    bm = _tile(math.gcd(m, m_head), bm)
    hb = m_head // bm
    assert col0 % w == 0
    vmem = 10 * _nbytes((bm, w), F32) * 2 + 3 * _nbytes((w, w), F32)
    return pl.pallas_call(
        functools.partial(_ssm_post_kernel, head_blocks=hb), name="ssm_post",
        grid=(m // bm,),
        in_specs=[pl.BlockSpec((2, bm, w), lambda i: (0, jnp.minimum(i, hb - 1), 0)),
                  pl.BlockSpec((2, bm, w), lambda i: (0, jnp.maximum(i - hb, 0), 0)),
                  pl.BlockSpec((bm, w), lambda i: (i, col0 // w)),
                  pl.BlockSpec((1, w), lambda i: (0, 0)),
                  pl.BlockSpec((None, w, w), lambda i: (layer, 0, 0))],
        out_specs=pl.BlockSpec((bm, w), lambda i: (i, 0)),
        out_shape=jax.ShapeDtypeStruct((m, w), BF16),
        scratch_shapes=[pltpu.VMEM((w, w), BF16)],
        compiler_params=_params(("arbitrary",), vmem),
    )(y_head, y_tail, p, d_vec.reshape(1, w), w_glu)


def _is_new_expert(be_ref, i):
    return (i == 0) | (be_ref[i] != be_ref[jnp.maximum(i - 1, 0)])


def _moe_glu_kernel(be_ref, nb_ref, x_ref, wg_ref, wu_ref, o_ref, wg_bf, wu_bf):
    i = pl.program_id(1)

    @pl.when(_is_new_expert(be_ref, i))
    def _():
        wg_bf[...] = wg_ref[...].astype(BF16)
        wu_bf[...] = wu_ref[...].astype(BF16)

    @pl.when(i < nb_ref[0])
    def _():
        x = x_ref[...]
        a = jnp.dot(x, wg_bf[...], preferred_element_type=F32)
        b = jnp.dot(x, wu_bf[...], preferred_element_type=F32)
        o_ref[...] = (a * jax.nn.sigmoid(a) * b).astype(o_ref.dtype)

    @pl.when(i >= nb_ref[0])
    def _():
        o_ref[...] = jnp.zeros_like(o_ref)


def moe_glu_up(x, wg, wu, layer, block_expert, nblocks, *, bm, bn=512):
    r, k = x.shape
    n = wg.shape[-1]
    assert r % bm == 0 and n % bn == 0
    wspec = pl.BlockSpec((None, None, k, bn), lambda j, i, be, nb: (layer, be[i], 0, j))
    vmem = (2 * _nbytes((bm, k), x.dtype) + 4 * _nbytes((k, bn), F32) + 2 * _nbytes((k, bn), BF16)
            + 2 * _nbytes((bm, bn), BF16) + 3 * _nbytes((bm, bn), F32))
    return pl.pallas_call(
        _moe_glu_kernel, name="moe_glu_up",
        grid_spec=pltpu.PrefetchScalarGridSpec(
            num_scalar_prefetch=2,
            grid=(n // bn, r // bm),
            in_specs=[pl.BlockSpec((bm, k), lambda j, i, be, nb: (i, 0)), wspec, wspec],
            out_specs=pl.BlockSpec((bm, bn), lambda j, i, be, nb: (i, j)),
            scratch_shapes=[pltpu.VMEM((k, bn), BF16), pltpu.VMEM((k, bn), BF16)]),
        out_shape=jax.ShapeDtypeStruct((r, n), BF16),
        compiler_params=_params(("arbitrary", "arbitrary"), vmem),
    )(block_expert, nblocks, x, wg, wu)


def _moe_down_kernel(be_ref, nb_ref, x_ref, w_ref, o_ref, w_bf):
    i = pl.program_id(1)

    @pl.when(_is_new_expert(be_ref, i))
    def _():
        w_bf[...] = w_ref[...].astype(BF16)

    @pl.when(i < nb_ref[0])
    def _():
        o_ref[...] = jnp.dot(x_ref[...], w_bf[...], preferred_element_type=F32)

    @pl.when(i >= nb_ref[0])
    def _():
        o_ref[...] = jnp.zeros_like(o_ref)


def moe_down(x, w, layer, block_expert, nblocks, *, bm, bn=512):
    r, k = x.shape
    n = w.shape[-1]
    assert r % bm == 0 and n % bn == 0
    vmem = (2 * _nbytes((bm, k), x.dtype) + 2 * _nbytes((k, bn), F32) + _nbytes((k, bn), BF16)
            + 2 * _nbytes((bm, bn), F32))
    return pl.pallas_call(
        _moe_down_kernel, name="moe_down",
        grid_spec=pltpu.PrefetchScalarGridSpec(
            num_scalar_prefetch=2,
            grid=(n // bn, r // bm),
            in_specs=[pl.BlockSpec((bm, k), lambda j, i, be, nb: (i, 0)),
                      pl.BlockSpec((None, None, k, bn), lambda j, i, be, nb: (layer, be[i], 0, j))],
            out_specs=pl.BlockSpec((bm, bn), lambda j, i, be, nb: (i, j)),
            scratch_shapes=[pltpu.VMEM((k, bn), BF16)]),
        out_shape=jax.ShapeDtypeStruct((r, n), F32),
        compiler_params=_params(("arbitrary", "arbitrary"), vmem),
    )(block_expert, nblocks, x, w)


def _qk(q, k, scale=1.0):
    if math.frexp(scale)[0] == 0.5:
        q, scale = q * jnp.asarray(scale, q.dtype), 1.0
    s = lax.dot_general(q, k, (((1,), (1,)), ((), ())), preferred_element_type=F32)
    return s if scale == 1.0 else s * scale


def _row_sinks(sink_ref, first_head, rep, rows_per_head):
    r = lax.broadcasted_iota(jnp.int32, (rep * rows_per_head, 1), 0)
    sink = jnp.full((rep * rows_per_head, 1), sink_ref[first_head], F32)
    for g in range(1, rep):
        sink = jnp.where(r >= g * rows_per_head, sink_ref[first_head + g], sink)
    return sink


def _stack_heads(q_ref, heads, d):
    parts = [q_ref[:, h * d:(h + 1) * d] for h in heads]
    return parts[0] if len(parts) == 1 else jnp.concatenate(parts, axis=0)


def _attn_kernel(*refs, heads, kv_heads, dq, dv, scale, use_sink, use_ctx):
    refs = list(refs)
    sink_ref = refs.pop(0) if use_sink else None
    q_ref, k_ref, v_ref = refs[:3]
    kc_ref, vc_ref = (refs[3], refs[4]) if use_ctx else (None, None)
    o_ref = refs[-1]
    rep = heads // kv_heads
    bq = q_ref.shape[0]
    pieces = [None] * heads
    for j in range(kv_heads):
        q = _stack_heads(q_ref, range(j * rep, (j + 1) * rep), dq)
        s = _qk(q, k_ref[:, j * dq:(j + 1) * dq], scale)
        m = jnp.max(s, axis=-1, keepdims=True)
        if use_ctx:
            s_c = _qk(q, kc_ref[:, j * dq:(j + 1) * dq].astype(BF16), scale)
            m = jnp.maximum(m, jnp.max(s_c, axis=-1, keepdims=True))
        if use_sink:
            sink = _row_sinks(sink_ref, j * rep, rep, bq)
            m = jnp.maximum(m, sink)
        p = jnp.exp(s - m)
        l = jnp.sum(p, axis=-1, keepdims=True)
        o = jnp.dot(p.astype(BF16), v_ref[:, j * dv:(j + 1) * dv].astype(BF16), preferred_element_type=F32)
        if use_ctx:
            p_c = jnp.exp(s_c - m)
            l = l + jnp.sum(p_c, axis=-1, keepdims=True)
            o = o + jnp.dot(p_c.astype(BF16), vc_ref[:, j * dv:(j + 1) * dv].astype(BF16), preferred_element_type=F32)
        if use_sink:
            l = l + jnp.exp(sink - m)
        o = o * (1.0 / l)
        for g in range(rep):
            pieces[j * rep + g] = o[g * bq:(g + 1) * bq]
    o_ref[...] = jnp.concatenate(pieces, axis=1).astype(o_ref.dtype)


def attention(q, k, v, *, heads, kv_heads, scale, sink=None, k_ctx=None, v_ctx=None, bq=512):
    b, t, qw = q.shape
    s = k.shape[1]
    dq, dv = qw // heads, v.shape[2] // kv_heads
    rep = heads // kv_heads
    bq = _tile(t, bq)
    use_sink, use_ctx = sink is not None, k_ctx is not None
    in_specs = [pl.BlockSpec((None, bq, qw), lambda bi, qi: (bi, qi, 0)),
                pl.BlockSpec((None, s, k.shape[2]), lambda bi, qi: (bi, 0, 0)),
                pl.BlockSpec((None, s, v.shape[2]), lambda bi, qi: (bi, 0, 0))]
    args = [q, k, v]
    c = 0
    if use_ctx:
        c = k_ctx.shape[1]
        in_specs += [pl.BlockSpec((None, c, k_ctx.shape[2]), lambda bi, qi: (bi, 0, 0)),
                     pl.BlockSpec((None, c, v_ctx.shape[2]), lambda bi, qi: (bi, 0, 0))]
        args += [k_ctx, v_ctx]
    if use_sink:
        in_specs = [pl.BlockSpec(memory_space=pltpu.SMEM)] + in_specs
        args = [sink.astype(F32)] + args
    vmem = (5 * _nbytes((rep * bq, s + c), F32) + 2 * _nbytes((bq, qw), BF16) + 2 * _nbytes((s, k.shape[2] + v.shape[2]), F32)
            + 3 * _nbytes((c, k.shape[2] + v.shape[2]), F32) + 6 * _nbytes((bq, heads * dv), F32))
    return pl.pallas_call(
        functools.partial(_attn_kernel, heads=heads, kv_heads=kv_heads, dq=dq, dv=dv, scale=scale,
                          use_sink=use_sink, use_ctx=use_ctx), name="attention",
        grid=(b, t // bq),
        in_specs=in_specs,
        out_specs=pl.BlockSpec((None, bq, heads * dv), lambda bi, qi: (bi, qi, 0)),
        out_shape=jax.ShapeDtypeStruct((b, t, heads * dv), BF16),
        compiler_params=_params(("arbitrary",) * 2, vmem),
    )(*args)


def _diff_attn_kernel(*refs, heads, scale, post_scale, use_ctx):
    lam_ref, q_ref, k_ref, v_ref = refs[:4]
    kc_ref, vc_ref = (refs[4], refs[5]) if use_ctx else (None, None)
    g_ref, o_ref = refs[-2:]
    d, dv = DIFF_HD, DIFF_VD

    def attend(c0, v, vc):
        q = q_ref[:, c0:c0 + d]
        s = _qk(q, k_ref[:, c0:c0 + d], scale)
        m = jnp.max(s, axis=-1, keepdims=True)
        if use_ctx:
            s_c = _qk(q, kc_ref[:, c0:c0 + d].astype(BF16), scale)
            m = jnp.maximum(m, jnp.max(s_c, axis=-1, keepdims=True))
        p = jnp.exp(s - m)
        l = jnp.sum(p, axis=-1, keepdims=True)
        o = jnp.dot(p.astype(BF16), v, preferred_element_type=F32)
        if use_ctx:
            p_c = jnp.exp(s_c - m)
            l = l + jnp.sum(p_c, axis=-1, keepdims=True)
            o = o + jnp.dot(p_c.astype(BF16), vc, preferred_element_type=F32)
        return o * (1.0 / l)

    pieces = []
    for h in range(heads):
        v = v_ref[:, h * dv:(h + 1) * dv].astype(BF16)
        vc = vc_ref[:, h * dv:(h + 1) * dv].astype(BF16) if use_ctx else None
        o = attend(2 * h * d, v, vc) - lam_ref[h] * attend((2 * h + 1) * d, v, vc)
        o = o * lax.rsqrt(jnp.mean(o * o, axis=-1, keepdims=True) + EPS) * g_ref[...]
        pieces.append(o * post_scale)
    o_ref[...] = jnp.concatenate(pieces, axis=1).astype(o_ref.dtype)


def diff_attention(q, k, v, lam, g_out, lambda_init, *, k_ctx=None, v_ctx=None, bq=512):
    b, t, qw = q.shape
    s, vw = k.shape[1], v.shape[2]
    heads = vw // DIFF_VD
    bq = _tile(t, bq)
    use_ctx = k_ctx is not None
    in_specs = [pl.BlockSpec(memory_space=pltpu.SMEM),
                pl.BlockSpec((None, bq, qw), lambda bi, qi: (bi, qi, 0)),
                pl.BlockSpec((None, s, qw), lambda bi, qi: (bi, 0, 0)),
                pl.BlockSpec((None, s, vw), lambda bi, qi: (bi, 0, 0))]
    args = [lam.astype(F32), q, k, v]
    c = 0
    if use_ctx:
        c = k_ctx.shape[1]
        in_specs += [pl.BlockSpec((None, c, qw), lambda bi, qi: (bi, 0, 0)),
                     pl.BlockSpec((None, c, vw), lambda bi, qi: (bi, 0, 0))]
        args += [k_ctx, v_ctx]
    in_specs.append(pl.BlockSpec((1, DIFF_VD), lambda bi, qi: (0, 0)))
    args.append(g_out.astype(F32).reshape(1, DIFF_VD))
    vmem = (8 * _nbytes((bq, s + c), F32) + 2 * _nbytes((bq, qw), BF16) + 2 * _nbytes((s, qw + vw), F32)
            + 3 * _nbytes((c, qw + vw), F32) + 6 * _nbytes((bq, vw), F32))
    return pl.pallas_call(
        functools.partial(_diff_attn_kernel, heads=heads, scale=DIFF_HD ** -0.5, post_scale=1.0 - lambda_init,
                          use_ctx=use_ctx), name="diff_attention",
        grid=(b, t // bq),
        in_specs=in_specs,
        out_specs=pl.BlockSpec((None, bq, vw), lambda bi, qi: (bi, qi, 0)),
        out_shape=jax.ShapeDtypeStruct((b, t, vw), BF16),
        compiler_params=_params(("arbitrary",) * 2, vmem),
    )(*args)


def _swa_lat_kernel(sink_ref, q_ref, kc_ref, vc_ref, k0_ref, k1_ref, k2_ref, v0_ref, v1_ref, v2_ref, o_ref,
                    *, scale, nblocks):
    n = pl.program_id(1)
    rows = SWA_REP * BLOCK
    r = lax.broadcasted_iota(jnp.int32, (rows, 3 * BLOCK), 0) & (BLOCK - 1)
    c = lax.broadcasted_iota(jnp.int32, (rows, 3 * BLOCK), 1)
    kpos = (n - 1) * BLOCK + c
    mask = (c - r >= BLOCK - WINDOW) & (c - r <= BLOCK + WINDOW) & (kpos >= 0) & (kpos < nblocks * BLOCK)
    pieces = [None] * SWA_HEADS
    for j in range(SWA_KV_HEADS):
        cs = slice(j * SWA_HD, (j + 1) * SWA_HD)
        q = _stack_heads(q_ref, range(j * SWA_REP, (j + 1) * SWA_REP), SWA_HD)
        kw = jnp.concatenate([k0_ref[:, cs], k1_ref[:, cs], k2_ref[:, cs]], axis=0)
        vw = jnp.concatenate([v0_ref[:, cs], v1_ref[:, cs], v2_ref[:, cs]], axis=0).astype(BF16)
        s_ctx = _qk(q, kc_ref[:, cs].astype(BF16), scale)
        s_loc = jnp.where(mask, _qk(q, kw, scale), NEG_INF)
        sink = _row_sinks(sink_ref, j * SWA_REP, SWA_REP, BLOCK)
        m = jnp.maximum(jnp.maximum(jnp.max(s_ctx, axis=-1, keepdims=True), jnp.max(s_loc, axis=-1, keepdims=True)), sink)
        p_ctx = jnp.exp(s_ctx - m)
        p_loc = jnp.exp(s_loc - m)
        l = jnp.sum(p_ctx, axis=-1, keepdims=True) + jnp.sum(p_loc, axis=-1, keepdims=True) + jnp.exp(sink - m)
        o = (jnp.dot(p_ctx.astype(BF16), vc_ref[:, cs].astype(BF16), preferred_element_type=F32)
             + jnp.dot(p_loc.astype(BF16), vw, preferred_element_type=F32)) * (1.0 / l)
        for g in range(SWA_REP):
            pieces[j * SWA_REP + g] = o[g * BLOCK:(g + 1) * BLOCK]
    o_ref[...] = jnp.concatenate(pieces, axis=1).astype(o_ref.dtype)


def swa_latent_attention(q, k, v, k_ctx, v_ctx, sink):
    b, t, qw = q.shape
    kw = k.shape[2]
    c = k_ctx.shape[1]
    nb = t // BLOCK
    pad = ((0, 0), (BLOCK, BLOCK), (0, 0))
    kp, vp = jnp.pad(k, pad), jnp.pad(v, pad)
    band = [pl.BlockSpec((None, BLOCK, kw), functools.partial(lambda bi, ni, o: (bi, ni + o, 0), o=o)) for o in range(3)]
    ctx = pl.BlockSpec((None, c, kw), lambda bi, ni: (bi, 0, 0))
    qspec = pl.BlockSpec((None, BLOCK, qw), lambda bi, ni: (bi, ni, 0))
    vmem = 8 * _nbytes((SWA_REP * BLOCK, c + 3 * BLOCK), F32) + 6 * _nbytes((c, kw), F32) + (4 << 20)
    return pl.pallas_call(
        functools.partial(_swa_lat_kernel, scale=SWA_HD ** -0.5, nblocks=nb), name="swa_latent_attention",
        grid=(b, nb),
        in_specs=[pl.BlockSpec(memory_space=pltpu.SMEM), qspec, ctx, ctx] + band + band,
        out_specs=qspec,
        out_shape=jax.ShapeDtypeStruct(q.shape, BF16),
        compiler_params=_params(("arbitrary",) * 2, vmem),
    )(sink.astype(F32), q, k_ctx, v_ctx, kp, kp, kp, vp, vp, vp)


def _state_to_halves(st):
    b, nd, _, g, n = st.shape
    gh = S5_HALF // SSM_CH
    z = st.reshape(b, nd, 2, g // gh, gh * n)
    return jnp.transpose(z, (1, 3, 0, 2, 4)).reshape(nd, g // gh, b, 2 * gh * n)


def _halves_to_state(h, groups):
    nd, nh, b, _ = h.shape
    z = h.reshape(nd, nh, b, 2, -1)
    return jnp.transpose(z, (2, 0, 3, 1, 4)).reshape(b, nd, 2, groups, SSM_N)


def _moe_ffn(logits, h_bf, P, j, bm):
    m = h_bf.shape[0]
    probs = jax.nn.softmax(logits, axis=-1)
    top_p, top_i = lax.top_k(probs, TOP_K)
    top_p = top_p / jnp.sum(top_p, axis=-1, keepdims=True)
    flat_e = top_i.reshape(-1).astype(jnp.int32)
    npairs = flat_e.shape[0]
    order = jnp.argsort(flat_e, stable=True).astype(jnp.int32)
    counts = jnp.sum(flat_e[:, None] == jnp.arange(N_EXPERTS, dtype=jnp.int32)[None, :], axis=0).astype(jnp.int32)
    blocks_per = (counts + bm - 1) // bm
    block_end = jnp.cumsum(blocks_per)
    pad_start = (block_end - blocks_per) * bm
    start = jnp.cumsum(counts) - counts
    sorted_e = flat_e[order]
    dest = pad_start[sorted_e] + jnp.arange(npairs, dtype=jnp.int32) - start[sorted_e]
    nrows = npairs + N_EXPERTS * bm
    pos = lax.sort_key_val(order, dest)[1].reshape(m, TOP_K)
    nblk = nrows // bm
    nused = block_end[-1:]
    block_expert = jnp.minimum(jnp.sum(jnp.arange(nblk, dtype=jnp.int32)[:, None] >= block_end[None, :], axis=1),
                               N_EXPERTS - 1).astype(jnp.int32)
    row = jnp.arange(nrows, dtype=jnp.int32)
    row_e = block_expert[row // bm]
    off = row - pad_start[row_e]
    rank = jnp.clip(start[row_e] + off, 0, npairs - 1)
    row_token = jnp.where(off < counts[row_e], order[rank] // TOP_K, 0)
    block_expert = jnp.where(jnp.arange(nblk) < nused[0], block_expert, block_expert[jnp.maximum(nused[0] - 1, 0)])
    x_sorted = jnp.take(h_bf, row_token, axis=0)
    act = moe_glu_up(x_sorted, P['w_moe_gate'], P['w_moe_up'], j, block_expert, nused, bm=bm, bn=512)
    y_sorted = moe_down(act, P['w_moe_down'], j, block_expert, nused, bm=bm)
    out = jnp.zeros((m, y_sorted.shape[1]), F32)
    for kk in range(TOP_K):
        out = out + top_p[:, kk:kk + 1] * jnp.take(y_sorted, pos[:, kk], axis=0)
    return out


SWA_COLS = (SWA_HEADS + 2 * SWA_KV_HEADS) * SWA_HD
DIFF_COLS = 2 * DIFF_HEADS * 2 * DIFF_HD + DIFF_HEADS * DIFF_VD


def kernel(x_prompt, x_sample, cache_mla_ckv, cache_mla_krope, state_ssm, cache_swa_k, cache_swa_v, cache_diff_k, cache_diff_v, c, c_ctx, w_ada, b_ada, g_norm1, g_norm2, w_in, g_mla_qlat, g_mla_kvlat, w_mla_uq, w_mla_ukv, g_mla_q, g_mla_k, ssm_a_re, ssm_a_im, ssm_log_dt, ssm_b_re, ssm_b_im, ssm_c_re, ssm_c_im, ssm_d, w_ssm_glu, g_swa_q, g_swa_k, swa_sink, g_diff_q, g_diff_k, diff_lambda, g_diff_out, w_branch, w_out, w_ffn_gate, w_ffn_up, w_ffn_down, w_router, w_moe_gate, w_moe_up, w_moe_down):
    P = dict(ssm_a_re=ssm_a_re, ssm_a_im=ssm_a_im, ssm_log_dt=ssm_log_dt, ssm_b_re=ssm_b_re, ssm_b_im=ssm_b_im,
             ssm_c_re=ssm_c_re, ssm_c_im=ssm_c_im, ssm_d=ssm_d, w_router=w_router, w_moe_gate=w_moe_gate,
             w_moe_up=w_moe_up, w_moe_down=w_moe_down)
    bp, tp, d = x_prompt.shape
    bs, ts, _ = x_sample.shape
    depth = w_in.shape[0]
    past = cache_mla_ckv.shape[2]
    mp, ms = bp * tp, bs * ts
    m = mp + ms
    w = d // 4
    groups = w // SSM_CH
    x = jnp.concatenate([x_prompt.reshape(mp, d), x_sample.reshape(ms, d)], axis=0)

    seg = math.gcd(mp, ts)
    nseg = m // seg
    seg_mod = np.concatenate([np.zeros(mp // seg, np.int32), 1 + np.repeat(np.arange(bs, dtype=np.int32), ts // seg)])
    cvec = jnp.concatenate([c_ctx[None, :], c, jnp.zeros((8 - 1 - bs, d), F32)], axis=0)
    cvec = cvec * jax.nn.sigmoid(cvec)

    def per_seg(v):
        return v[seg_mod][:, None, :]

    def gated_add(xx, gt_seg, upd):
        return (xx.reshape(nseg, seg, d) + gt_seg * upd.reshape(nseg, seg, d)).reshape(m, d)

    rope_mla = _rope_tables(ts, MLA_QK, MLA_HEADS)
    rope_swa_q = rope_diff = _rope_tables(ts, SWA_HD, SWA_HEADS)
    rope_swa_k = _rope_tables(ts, SWA_HD, SWA_KV_HEADS)
    assert DIFF_HD == SWA_HD and 2 * DIFF_HEADS == SWA_HEADS

    outs = [[] for _ in range(7)]
    for l in range(depth):
        lambda_init = 0.8 - 0.6 * math.exp(-0.3 * l)
        mods = matmul(cvec, w_ada, (l,), bm=8, bn=1024) + b_ada[l][None, :]
        sh1, sc1, gt1, sh2, sc2, gt2 = (per_seg(v) for v in jnp.split(mods, 6, axis=-1))

        h = modulated_norm(x, g_norm1[l], sc1, sh1)
        p, col0 = in_projection(h, w_in, l)
        k_rope = matmul(h, w_in, (l,), bn=IN_SHIFT_UNIT, cols=(2 * w // IN_SHIFT_UNIT, 1))[:, :MLA_ROPE]

        def tm(z, bb, tt):
            return z.reshape(bb, tt, z.shape[-1])

        lat_rows = (mp, m, ts)

        ckv_bf, c_kv = head_norm(p, col0['ckv'], w, g_mla_kvlat[l], w, out_f32=True)
        q_lat = head_norm(p, col0['q'], w, g_mla_qlat[l], w)
        q_m = head_norm(matmul(q_lat, w_mla_uq, (l,)), 0, MLA_HEADS * MLA_QK, g_mla_q[l], MLA_QK,
                        rope=rope_mla, rope_rows=lat_rows)
        ckv_all = jnp.concatenate([ckv_bf, cache_mla_ckv[:, l].reshape(bs * past, w).astype(BF16)], axis=0)
        krope_all = jnp.concatenate([k_rope, cache_mla_krope[:, l].reshape(bs * past, MLA_ROPE)], axis=0)
        w_ukv = jnp.swapaxes(w_mla_ukv[l].reshape(w, MLA_HEADS, 2, MLA_NOPE), 1, 2).reshape(w, -1)
        kv = matmul(ckv_all, w_ukv, bm=512)
        k_m = head_norm(kv, 0, MLA_HEADS * MLA_NOPE, g_mla_k[l], MLA_QK, extra=krope_all,
                        rope=rope_mla, rope_rows=lat_rows)
        v_m = kv[:, MLA_HEADS * MLA_NOPE:]
        mla = dict(heads=MLA_HEADS, kv_heads=MLA_HEADS, scale=MLA_QK ** -0.5)
        o_mla_p = attention(tm(q_m[:mp], bp, tp), tm(k_m[:mp], bp, tp), tm(v_m[:mp], bp, tp), **mla)
        o_mla_s = attention(tm(q_m[mp:], bs, ts), tm(k_m[mp:m], bs, ts), tm(v_m[mp:m], bs, ts),
                            k_ctx=tm(k_m[m:], bs, past), v_ctx=tm(v_m[m:], bs, past), **mla)
        o_mla = (o_mla_p.reshape(mp, w), o_mla_s.reshape(ms, w))

        tabs = _s5_tables(ssm_a_re[l], ssm_a_im[l], ssm_log_dt[l], ssm_b_re[l], ssm_b_im[l], ssm_c_re[l], ssm_c_im[l])
        ncol = p.shape[1]
        h0_p = jnp.zeros((2, w // S5_HALF, bp, 2 * (S5_HALF // SSM_CH) * SSM_N), F32)
        y_p, hT_p = s5_scan(p.reshape(m // tp, tp, ncol), col0['ssm'], 0, bp, tabs, h0_p)
        y_s, _ = s5_scan(p.reshape(m // ts, ts, ncol), col0['ssm'], mp // ts, bs, tabs, _state_to_halves(state_ssm[:, l]))
        o_ssm = ssm_post(y_p.reshape(2, mp, w), y_s.reshape(2, ms, w), p, col0['ssm'], ssm_d[l], w_ssm_glu, l)
        ssm_state = _halves_to_state(hT_p, groups)

        nq = SWA_HEADS * SWA_HD
        nk = SWA_KV_HEADS * SWA_HD
        c_swa = col0['swa']
        q_s = head_norm(p, c_swa, nq, g_swa_q[l], SWA_HD, bw=256, rope=rope_swa_q, rope_rows=lat_rows)
        k_s, k_s32 = head_norm(p, c_swa + nq, nk, g_swa_k[l], SWA_HD, rope=rope_swa_k, rope_rows=lat_rows, out_f32=True)
        v_s = p[:, c_swa + nq + nk:c_swa + nq + 2 * nk]
        o_swa_p = attention(tm(q_s[:mp], bp, tp), tm(k_s[:mp], bp, tp), tm(v_s[:mp], bp, tp), heads=SWA_HEADS,
                            kv_heads=SWA_KV_HEADS, scale=SWA_HD ** -0.5, sink=swa_sink[l], bq=256)
        o_swa_s = swa_latent_attention(tm(q_s[mp:], bs, ts), tm(k_s[mp:], bs, ts), tm(v_s[mp:], bs, ts),
                                       cache_swa_k[:, l].reshape(bs, past, nk), cache_swa_v[:, l].reshape(bs, past, nk),
                                       swa_sink[l])
        o_swa = (o_swa_p.reshape(mp, w), o_swa_s.reshape(ms, w))

        ndq = DIFF_HEADS * 2 * DIFF_HD
        c_diff = col0['diff']
        q_d = head_norm(p, c_diff, ndq, g_diff_q[l], DIFF_HD, bw=256, rope=rope_diff, rope_rows=lat_rows)
        k_d, k_d32 = head_norm(p, c_diff + ndq, ndq, g_diff_k[l], DIFF_HD, bw=256, rope=rope_diff, rope_rows=lat_rows,
                               out_f32=True)
        v_d = p[:, c_diff + 2 * ndq:c_diff + 2 * ndq + DIFF_HEADS * DIFF_VD]
        lp = diff_lambda[l]
        lam = jnp.exp(jnp.sum(lp[0] * lp[1], -1)) - jnp.exp(jnp.sum(lp[2] * lp[3], -1)) + lambda_init
        o_diff_p = diff_attention(tm(q_d[:mp], bp, tp), tm(k_d[:mp], bp, tp), tm(v_d[:mp], bp, tp),
                                  lam, g_diff_out[l], lambda_init, bq=256)
        o_diff_s = diff_attention(tm(q_d[mp:], bs, ts), tm(k_d[mp:], bs, ts), tm(v_d[mp:], bs, ts), lam, g_diff_out[l],
                                  lambda_init, k_ctx=cache_diff_k[:, l].reshape(bs, past, ndq),
                                  v_ctx=cache_diff_v[:, l].reshape(bs, past, DIFF_HEADS * DIFF_VD), bq=256)
        o_diff = (o_diff_p.reshape(mp, w), o_diff_s.reshape(ms, w))

        merged = merge_branches((o_mla, o_ssm, o_swa, o_diff), p, col0['gate'], w_branch, l, bm=1024, bn=256)
        x = matmul_residual(merged, w_out, (l,), x, gt1, bm=1024)

        j = l // 2
        if l % 2 == 0:
            h2 = modulated_norm(x, g_norm2[l], sc2, sh2)
            act = glu_up(h2, w_ffn_gate, w_ffn_up, (j,))
            x = matmul_residual(act, w_ffn_down, (j,), x, gt2)
        else:
            h2, logits = modulated_norm(x, g_norm2[l], sc2, sh2, w_router, j)
            x = gated_add(x, gt2, _moe_ffn(logits, h2, P, j, bm=512))

        outs[0].append(c_kv[:mp].reshape(bp, tp, w))
        outs[1].append(k_rope[:mp].reshape(bp, tp, MLA_ROPE))
        outs[2].append(ssm_state)
        outs[3].append(k_s32[:mp].reshape(bp, tp, SWA_KV_HEADS, SWA_HD))
        outs[4].append(v_s[:mp].reshape(bp, tp, SWA_KV_HEADS, SWA_HD))
        outs[5].append(k_d32[:mp].reshape(bp, tp, DIFF_HEADS, 2, DIFF_HD))
        outs[6].append(v_d[:mp].reshape(bp, tp, DIFF_HEADS, DIFF_VD))

    return (x[:mp].reshape(bp, tp, d), x[mp:].reshape(bs, ts, d)) + tuple(jnp.stack(o, axis=1) for o in outs)
```
